```python
import jax, jax.numpy as jnp
from jax import lax
import numpy as np

D_MODEL = 1024
BATCH = 4
SEQ = 4096
DEPTH = 4

SB_HEADS = 8
SB_HEAD_DIM = 64
D_ATTN = SB_HEADS * SB_HEAD_DIM
Q_BLOCK = 128
D_RNN = D_MODEL // 2
RG_BLOCKS = 8
RG_BLOCK_DIM = D_RNN // RG_BLOCKS
CONV_WIDTH = 4
RG_C = 8.0
IN_COLS = 3 * D_ATTN + 2 * D_RNN + 2 * D_MODEL
SPLIT_POINTS = (D_ATTN, 2 * D_ATTN, 3 * D_ATTN, 3 * D_ATTN + D_RNN,
                3 * D_ATTN + 2 * D_RNN, 3 * D_ATTN + 2 * D_RNN + D_MODEL)
N_GROUPS = 4
EXPERTS_PER_GROUP = 4
N_EXPERTS = N_GROUPS * EXPERTS_PER_GROUP
TOP_K_IN_GROUP = 2
D_EXPERT = 512
PLE_DIM = 256
EPS = 1e-6

kernel_name = "hybrid_sb_rglru_hmoe_trunk"


def rms_norm(x, g):
    xf = x.astype(jnp.float32)
    y = xf * lax.rsqrt(jnp.mean(xf * xf, axis=-1, keepdims=True) + EPS)
    return (y * g.astype(jnp.float32)).astype(x.dtype)


def stick_breaking_attention(q, k, v):
    B, S, _ = q.shape
    nb = S // Q_BLOCK
    scale = SB_HEAD_DIM ** -0.5
    q = q.reshape(B, S, SB_HEADS, SB_HEAD_DIM).transpose(0, 2, 1, 3)
    k = k.reshape(B, S, SB_HEADS, SB_HEAD_DIM).transpose(0, 2, 1, 3)
    v = v.reshape(B, S, SB_HEADS, SB_HEAD_DIM).transpose(0, 2, 1, 3)
    q_blocks = q.reshape(B, SB_HEADS, nb, Q_BLOCK, SB_HEAD_DIM).transpose(2, 0, 1, 3, 4)
    key_pos = jnp.arange(S)

    def one_block(args):
        q_blk, start = args
        z = jnp.einsum('bhqd,bhkd->bhqk', q_blk, k).astype(jnp.float32) * scale
        q_pos = start + jnp.arange(Q_BLOCK)
        causal = key_pos[None, :] < q_pos[:, None]
        log_not = jnp.where(causal, jax.nn.log_sigmoid(-z), 0.0)
        suffix = lax.cumsum(log_not, axis=3, reverse=True) - log_not
        w = jnp.where(causal, jnp.exp(jax.nn.log_sigmoid(z) + suffix), 0.0)
        return jnp.einsum('bhqk,bhkd->bhqd', w.astype(v.dtype), v)

    starts = jnp.arange(nb) * Q_BLOCK
    o = lax.map(one_block, (q_blocks, starts))
    return o.transpose(1, 0, 3, 2, 4).reshape(B, S, D_ATTN)


def _linear_recurrence(c1, c2):
    a1, b1 = c1
    a2, b2 = c2
    return (a1 * a2, a2 * b1 + b2)


def rg_lru_branch(xr, xg, conv_w, conv_b, w_a, b_a, w_x, b_x, lam):
    B, S, _ = xr.shape
    xpad = jnp.pad(xr, ((0, 0), (CONV_WIDTH - 1, 0), (0, 0)))
    xc = conv_b
    for j in range(CONV_WIDTH):
        xc = xc + conv_w[j] * xpad[:, j:j + S]
    blocks = xc.reshape(B, S, RG_BLOCKS, RG_BLOCK_DIM)
    r = jax.nn.sigmoid(jnp.einsum('bsgi,gij->bsgj', blocks, w_a).reshape(B, S, D_RNN) + b_a)
    ig = jax.nn.sigmoid(jnp.einsum('bsgi,gij->bsgj', blocks, w_x).reshape(B, S, D_RNN) + b_x)
    log_a = -RG_C * jax.nn.softplus(-lam.astype(jnp.float32)) * r.astype(jnp.float32)
    a = jnp.exp(log_a)
    b = jnp.sqrt(-jnp.expm1(2.0 * log_a)) * (ig * xc).astype(jnp.float32)
    _, h = lax.associative_scan(_linear_recurrence, (a, b), axis=1)
    return h.astype(xr.dtype) * jax.nn.gelu(xg)


def hierarchical_moe(h, w_group, b_group, w_expert, b_expert, w_gate, w_up, w_down):
    B, S, D = h.shape
    ht = h.reshape(B * S, D)
    g_logits = (ht @ w_group + b_group).astype(jnp.float32)
    p_group = jax.nn.softmax(g_logits, axis=-1)
    g_idx = jnp.argmax(g_logits, axis=-1)
    gate_group = jnp.take_along_axis(p_group, g_idx[:, None], axis=1)[:, 0]
    e_logits = (ht @ w_expert + b_expert).astype(jnp.float32)
    e_logits = e_logits.reshape(-1, N_GROUPS, EXPERTS_PER_GROUP)
    sel = jnp.take_along_axis(e_logits, g_idx[:, None, None], axis=1)[:, 0]
    top_vals, top_idx = lax.top_k(sel, TOP_K_IN_GROUP)
    w_in_group = jax.nn.softmax(top_vals, axis=-1)
    within = jnp.einsum('tk,tke->te', w_in_group,
                        jax.nn.one_hot(top_idx, EXPERTS_PER_GROUP, dtype=jnp.float32))
    combine = (jax.nn.one_hot(g_idx, N_GROUPS, dtype=jnp.float32)[:, :, None]
               * within[:, None, :] * gate_group[:, None, None])
    combine = combine.reshape(-1, N_EXPERTS).astype(h.dtype)
    y = jnp.zeros_like(ht)
    for e in range(N_EXPERTS):
        he = jax.nn.silu(ht @ w_gate[e]) * (ht @ w_up[e])
        y = y + combine[:, e:e + 1] * (he @ w_down[e])
    return y.reshape(B, S, D)


def setup_inputs(seed: int = 0) -> dict:
    key = jax.random.key(seed)
    ks = jax.random.split(key, 32)

    def nrm(k, shape, scale):
        return jax.random.normal(k, shape, jnp.float32) * scale

    u = jax.random.uniform(ks[11], (DEPTH, D_RNN), jnp.float32, minval=0.9, maxval=0.999)
    s = u ** (1.0 / RG_C)
    rg_lambda = jnp.log(s) - jnp.log1p(-s)
    return {
        "x": nrm(ks[0], (BATCH, SEQ, D_MODEL), 1.0),
        "p": nrm(ks[1], (DEPTH, BATCH, SEQ, PLE_DIM), 1.0),
        "norm_mix": 1.0 + nrm(ks[2], (DEPTH, D_MODEL), 0.05),
        "w_in": nrm(ks[3], (DEPTH, D_MODEL, IN_COLS), D_MODEL ** -0.5),
        "conv_w": nrm(ks[4], (DEPTH, CONV_WIDTH, D_RNN), CONV_WIDTH ** -0.5),
        "conv_b": nrm(ks[5], (DEPTH, D_RNN), 0.01),
        "w_rg_a": nrm(ks[6], (DEPTH, RG_BLOCKS, RG_BLOCK_DIM, RG_BLOCK_DIM), RG_BLOCK_DIM ** -0.5),
        "b_rg_a": nrm(ks[7], (DEPTH, D_RNN), 0.01),
        "w_rg_x": nrm(ks[8], (DEPTH, RG_BLOCKS, RG_BLOCK_DIM, RG_BLOCK_DIM), RG_BLOCK_DIM ** -0.5),
        "b_rg_x": nrm(ks[9], (DEPTH, D_RNN), 0.01),
        "rg_lambda": rg_lambda,
        "w_attn_o": nrm(ks[12], (DEPTH, D_ATTN, D_MODEL), D_ATTN ** -0.5),
        "w_rnn_o": nrm(ks[13], (DEPTH, D_RNN, D_MODEL), D_RNN ** -0.5),
        "w_out": nrm(ks[14], (DEPTH, D_MODEL, D_MODEL), D_MODEL ** -0.5),
        "norm_moe": 1.0 + nrm(ks[15], (DEPTH, D_MODEL), 0.05),
        "w_router_group": nrm(ks[16], (DEPTH, D_MODEL, N_GROUPS), D_MODEL ** -0.5),
        "b_router_group": nrm(ks[17], (DEPTH, N_GROUPS), 0.01),
        "w_router_expert": nrm(ks[18], (DEPTH, D_MODEL, N_EXPERTS), D_MODEL ** -0.5),
        "b_router_expert": nrm(ks[19], (DEPTH, N_EXPERTS), 0.01),
        "w_exp_gate": nrm(ks[20], (DEPTH, N_EXPERTS, D_MODEL, D_EXPERT), D_MODEL ** -0.5),
        "w_exp_up": nrm(ks[21], (DEPTH, N_EXPERTS, D_MODEL, D_EXPERT), D_MODEL ** -0.5),
        "w_exp_down": nrm(ks[22], (DEPTH, N_EXPERTS, D_EXPERT, D_MODEL), D_EXPERT ** -0.5),
        "norm_ple": 1.0 + nrm(ks[23], (DEPTH, D_MODEL), 0.05),
        "w_ple_gate": nrm(ks[24], (DEPTH, D_MODEL, D_MODEL), D_MODEL ** -0.5),
        "w_ple_proj": nrm(ks[25], (DEPTH, PLE_DIM, D_MODEL), PLE_DIM ** -0.5),
        "norm_final": 1.0 + nrm(ks[26], (D_MODEL,), 0.05),
    }


def reference(x, p, norm_mix, w_in, conv_w, conv_b, w_rg_a, b_rg_a, w_rg_x, b_rg_x,
              rg_lambda, w_attn_o, w_rnn_o, w_out, norm_moe, w_router_group,
              b_router_group, w_router_expert, b_router_expert, w_exp_gate, w_exp_up,
              w_exp_down, norm_ple, w_ple_gate, w_ple_proj, norm_final):
    for i in range(DEPTH):
        h = rms_norm(x, norm_mix[i])
        proj = h @ w_in[i]
        q, k, v, xr, xg, gate_attn, gate_rnn = jnp.split(proj, SPLIT_POINTS, axis=-1)
        y_attn = stick_breaking_attention(q, k, v) @ w_attn_o[i]
        y_rnn = rg_lru_branch(xr, xg, conv_w[i], conv_b[i], w_rg_a[i], b_rg_a[i],
                              w_rg_x[i], b_rg_x[i], rg_lambda[i]) @ w_rnn_o[i]
        merged = jax.nn.sigmoid(gate_attn) * y_attn + jax.nn.sigmoid(gate_rnn) * y_rnn
        x = x + merged @ w_out[i]
        h = rms_norm(x, norm_moe[i])
        x = x + hierarchical_moe(h, w_router_group[i], b_router_group[i],
                                 w_router_expert[i], b_router_expert[i],
                                 w_exp_gate[i], w_exp_up[i], w_exp_down[i])
        gate = jax.nn.sigmoid(rms_norm(x, norm_ple[i]) @ w_ple_gate[i])
        x = x + gate * (p[i] @ w_ple_proj[i])
    return rms_norm(x, norm_final)
```

```python
import functools

import jax
import jax.numpy as jnp
from jax import lax
from jax.experimental import pallas as pl
from jax.experimental.pallas import tpu as pltpu

F32 = jnp.float32
BF16 = jnp.bfloat16
EPS = 1e-6

N_HEADS = 8
HEAD_DIM = 64
N_GROUPS = 4
EXPERTS_PER_GROUP = 4
N_EXPERTS = N_GROUPS * EXPERTS_PER_GROUP
PAIRS_PER_GROUP = 6
N_CLASSES = N_GROUPS * PAIRS_PER_GROUP
CLASS_ROWS = 32
CONV_WIDTH = 4
RG_C = 8.0

LANES = 128
SUBLANES = 8
VMEM_LIMIT = 56 * 1024 * 1024

ROW_TILE = 512
ATT_TQ = 512
ATT_TK = 256
RNN_TS = 256
MOE_TILE = 256
DMA_TILE = 256
ROUTE_LANES = 128

_NT = (((1,), (1,)), ((), ()))


def _params(*sem):
    return pltpu.CompilerParams(dimension_semantics=sem, vmem_limit_bytes=VMEM_LIMIT)


def _rms(x, g):
    ms = jnp.mean(x * x, axis=-1, keepdims=True)
    return x * lax.rsqrt(ms + EPS) * g


def _sigmoid(x):
    return 1.0 / (1.0 + jnp.exp(-x))


def _dot(a, b):
    return jnp.dot(a, b, preferred_element_type=F32)


def _inproj_kernel(x_ref, g_ref, w_ref, o_ref, *, col_chunk):
    h = _rms(x_ref[...], g_ref[...]).astype(BF16)
    for n in range(0, o_ref.shape[1], col_chunk):
        o_ref[:, n:n + col_chunk] = _dot(h, w_ref[:, n:n + col_chunk]).astype(BF16)


def _inproj(x, g, w):
    t, d = x.shape
    n = w.shape[1]
    return pl.pallas_call(
        functools.partial(_inproj_kernel, col_chunk=512),
        out_shape=jax.ShapeDtypeStruct((t, n), BF16),
        grid=(t // ROW_TILE,),
        in_specs=[
            pl.BlockSpec((ROW_TILE, d), lambda i: (i, 0)),
            pl.BlockSpec((1, d), lambda i: (0, 0)),
            pl.BlockSpec((d, n), lambda i: (0, 0)),
        ],
        out_specs=pl.BlockSpec((ROW_TILE, n), lambda i: (i, 0)),
        compiler_params=_params("arbitrary"),
        name="inproj",
    )(x, g, w)


def _attn_kernel(q_ref, k_ref, v_ref, u_ref, o_ref, acc_ref, carry_ref, *, tq, tk):
    qi = pl.program_id(2)
    lane = lax.broadcasted_iota(jnp.int32, (1, LANES), 1)
    first_head = lane < HEAD_DIM
    q = q_ref[...] * jnp.asarray(HEAD_DIM ** -0.5, BF16)
    qz = jnp.zeros_like(q)
    q_heads = (jnp.where(first_head, q, qz), jnp.where(first_head, qz, q))
    acc_ref[...] = jnp.zeros_like(acc_ref)
    carry_ref[...] = jnp.zeros_like(carry_ref)
    row = qi * tq + lax.broadcasted_iota(jnp.int32, (tq, tk), 0)
    col = lax.broadcasted_iota(jnp.int32, (tq, tk), 1)
    u = u_ref[...]
    nsteps = (qi + 1) * (tq // tk)

    def body(step, c):
        ks = pl.multiple_of((nsteps - 1 - step) * tk, tk)
        kb = k_ref[pl.ds(ks, tk), :]
        vb = v_ref[pl.ds(ks, tk), :]
        vz = jnp.zeros_like(vb)
        v_heads = (jnp.where(first_head, vb, vz), jnp.where(first_head, vz, vb))
        causal = (col + ks) < row
        pv = None
        for h in range(2):
            z = lax.dot_general(q_heads[h], kb, _NT, preferred_element_type=F32)
            sp = jnp.maximum(z, 0.0) + jnp.log(1.0 + jnp.exp(-jnp.abs(z)))
            sp = jnp.where(causal, sp, 0.0)
            sp_hi = sp.astype(BF16)
            sp_lo = (sp - sp_hi.astype(F32)).astype(BF16)
            suffix = _dot(sp_hi, u) + _dot(sp_lo, u)
            carry = carry_ref[h]
            arg = z - suffix - jnp.tile(carry, (1, tk // LANES))
            w = jnp.where(causal, jnp.exp(arg), 0.0).astype(BF16)
            d = _dot(w, v_heads[h])
            pv = d if pv is None else pv + d
            carry_ref[h] = carry + suffix[:, 0:1]
        acc_ref[...] += pv
        return c

    lax.fori_loop(0, nsteps, body, 0)
    o_ref[...] = acc_ref[...].astype(BF16)


def _attention(proj, u, *, q_blk, k_blk, v_blk):
    b, s, _ = proj.shape
    hp = N_HEADS * HEAD_DIM // LANES
    return pl.pallas_call(
        functools.partial(_attn_kernel, tq=ATT_TQ, tk=ATT_TK),
        out_shape=jax.ShapeDtypeStruct((b, s, N_HEADS * HEAD_DIM), BF16),
        grid=(b, hp, s // ATT_TQ),
        in_specs=[
            pl.BlockSpec((None, ATT_TQ, LANES), lambda bi, hi, qi: (bi, qi, q_blk + hi)),
            pl.BlockSpec((None, s, LANES), lambda bi, hi, qi: (bi, 0, k_blk + hi)),
            pl.BlockSpec((None, s, LANES), lambda bi, hi, qi: (bi, 0, v_blk + hi)),
            pl.BlockSpec((ATT_TK, ATT_TK), lambda bi, hi, qi: (0, 0)),
        ],
        out_specs=pl.BlockSpec((None, ATT_TQ, LANES), lambda bi, hi, qi: (bi, qi, hi)),
        scratch_shapes=[
            pltpu.VMEM((ATT_TQ, LANES), F32),
            pltpu.VMEM((2, ATT_TQ, LANES), F32),
        ],
        compiler_params=_params("arbitrary", "arbitrary", "arbitrary"),
        name="sb_attention",
    )(proj, proj, proj, u)


def _rglru_kernel(xr_ref, xg_ref, cw_ref, cb_ref, wa_ref, ba_ref, wx_ref, bx_ref, lam_ref,
                  o_ref, tail_ref, h_ref, a_scr, b_scr, *, ts):
    @pl.when(pl.program_id(1) == 0)
    def _():
        tail_ref[...] = jnp.zeros_like(tail_ref)
        h_ref[...] = jnp.zeros_like(h_ref)

    xr = xr_ref[...].astype(F32)
    ext = jnp.concatenate([tail_ref[...], xr], axis=0)
    tail_ref[...] = xr[ts - SUBLANES:, :]
    xc = cb_ref[...]
    for j in range(CONV_WIDTH):
        lo = SUBLANES - (CONV_WIDTH - 1) + j
        xc = xc + cw_ref[j:j + 1, :] * ext[lo:lo + ts, :]
    xcb = xc.astype(BF16)
    r = _sigmoid(_dot(xcb, wa_ref[...]) + ba_ref[...])
    ig = _sigmoid(_dot(xcb, wx_ref[...]) + bx_ref[...])
    lam = lam_ref[...]
    softplus_neg_lam = jnp.maximum(-lam, 0.0) + jnp.log(1.0 + jnp.exp(-jnp.abs(lam)))
    log_a = (-RG_C * softplus_neg_lam) * r
    a = jnp.exp(log_a)
    a_scr[...] = a
    b_scr[...] = jnp.sqrt(1.0 - a * a) * (ig * xc)

    def step(t, h):
        h = a_scr[pl.ds(t, 1), :] * h + b_scr[pl.ds(t, 1), :]
        b_scr[pl.ds(t, 1), :] = h
        return h

    h_ref[...] = lax.fori_loop(0, ts, step, h_ref[...], unroll=8)
    xg = xg_ref[...].astype(F32)
    gelu = 0.5 * xg * (1.0 + jnp.tanh(0.7978845608028654 * (xg + 0.044715 * (xg * xg * xg))))
    o_ref[...] = (b_scr[...] * gelu).astype(BF16)


def _rglru(proj, cw, cb, wa, ba, wx, bx, lam, *, xr_blk, xg_blk):
    b, s, _ = proj.shape
    dr = cw.shape[1]
    vec = lambda: pl.BlockSpec((1, dr), lambda bi, si: (0, 0))
    return pl.pallas_call(
        functools.partial(_rglru_kernel, ts=RNN_TS),
        out_shape=jax.ShapeDtypeStruct((b, s, dr), BF16),
        grid=(b, s // RNN_TS),
        in_specs=[
            pl.BlockSpec((None, RNN_TS, dr), lambda bi, si: (bi, si, xr_blk)),
            pl.BlockSpec((None, RNN_TS, dr), lambda bi, si: (bi, si, xg_blk)),
            pl.BlockSpec((CONV_WIDTH, dr), lambda bi, si: (0, 0)),
            vec(),
            pl.BlockSpec((dr, dr), lambda bi, si: (0, 0)),
            vec(),
            pl.BlockSpec((dr, dr), lambda bi, si: (0, 0)),
            vec(),
            vec(),
        ],
        out_specs=pl.BlockSpec((None, RNN_TS, dr), lambda bi, si: (bi, si, 0)),
        scratch_shapes=[
            pltpu.VMEM((SUBLANES, dr), F32),
            pltpu.VMEM((1, dr), F32),
            pltpu.VMEM((RNN_TS, dr), F32),
            pltpu.VMEM((RNN_TS, dr), F32),
        ],
        compiler_params=_params("arbitrary", "arbitrary"),
        name="rglru",
    )(proj, proj, cw, cb, wa, ba, wx, bx, lam)


def _first_argmax(vals):
    m = vals[0]
    for v in vals[1:]:
        m = jnp.maximum(m, v)
    idx = jnp.full_like(m, float(len(vals) - 1))
    for k in range(len(vals) - 2, -1, -1):
        idx = jnp.where(vals[k] == m, float(k), idx)
    return m, idx


def _merge_kernel(x_ref, at_ref, rn_ref, ga_ref, gr_ref, wao_ref, wro_ref, wo_ref, g_ref,
                  wrh_ref, wrl_ref, br_ref, x1_ref, hr_ref, cls_ref):
    d = x_ref.shape[1]
    ya = _dot(at_ref[...], wao_ref[...])
    yr = _dot(rn_ref[...], wro_ref[...])
    merged = _sigmoid(ga_ref[...].astype(F32)) * ya + _sigmoid(gr_ref[...].astype(F32)) * yr
    x1 = x_ref[...] + _dot(merged.astype(BF16), wo_ref[...])
    x1_ref[...] = x1
    h = _rms(x1, g_ref[...])
    hr_ref[:, :d] = h

    h_hi = h.astype(BF16)
    h_lo = (h - h_hi.astype(F32)).astype(BF16)
    wrh = wrh_ref[...]
    lt = (lax.dot_general(wrh, h_hi, _NT, preferred_element_type=F32)
          + lax.dot_general(wrh, h_lo, _NT, preferred_element_type=F32)
          + lax.dot_general(wrl_ref[...], h_hi, _NT, preferred_element_type=F32))
    lt = lt + br_ref[...]
    g = [lt[k:k + 1, :] for k in range(N_GROUPS)]
    e = [lt[N_GROUPS + k:N_GROUPS + k + 1, :] for k in range(N_EXPERTS)]

    gmax, gi = _first_argmax(g)
    den = jnp.exp(g[0] - gmax)
    for k in range(1, N_GROUPS):
        den = den + jnp.exp(g[k] - gmax)
    gate = 1.0 / den
    sel = []
    for j in range(EXPERTS_PER_GROUP):
        s = e[(N_GROUPS - 1) * EXPERTS_PER_GROUP + j]
        for k in range(N_GROUPS - 2, -1, -1):
            s = jnp.where(gi == float(k), e[k * EXPERTS_PER_GROUP + j], s)
        sel.append(s)
    m1, i1 = _first_argmax(sel)
    sel2 = [jnp.where(i1 == float(j), -jnp.inf, sel[j]) for j in range(EXPERTS_PER_GROUP)]
    m2, i2 = _first_argmax(sel2)
    t = jnp.exp(m2 - m1)
    w1 = gate * (1.0 / (1.0 + t))
    w2 = gate * (t / (1.0 + t))
    first_is_lo = i1 < i2
    a = jnp.minimum(i1, i2)
    b = jnp.maximum(i1, i2)
    w_lo = jnp.where(first_is_lo, w1, w2)
    w_hi = jnp.where(first_is_lo, w2, w1)
    pair = jnp.where(a == 0.0, b - 1.0, jnp.where(a == 1.0, b + 1.0, 5.0))
    cls = gi * float(PAIRS_PER_GROUP) + pair

    n = cls.shape[1]
    cls_ref[...] = jnp.concatenate([cls, jnp.zeros((SUBLANES - 1, n), F32)], axis=0)
    rows = jnp.concatenate([w_lo, w_hi, jnp.zeros((ROUTE_LANES - 2, n), F32)], axis=0)
    hr_ref[:, d:] = rows.T


def _merge(x, attn, rnn, proj, wao, wro, wo, g, wrh, wrl, br, *, ga_blk, gr_blk):
    t, d = x.shape
    da = attn.shape[1]
    dr = rnn.shape[1]
    tm = ROW_TILE
    full = lambda a: pl.BlockSpec(a.shape, lambda i: (0,) * a.ndim)
    return pl.pallas_call(
        _merge_kernel,
        out_shape=(
            jax.ShapeDtypeStruct((t, d), F32),
            jax.ShapeDtypeStruct((t, d + ROUTE_LANES), F32),
            jax.ShapeDtypeStruct((SUBLANES, t), F32),
        ),
        grid=(t // tm,),
        in_specs=[
            pl.BlockSpec((tm, d), lambda i: (i, 0)),
            pl.BlockSpec((tm, da), lambda i: (i, 0)),
            pl.BlockSpec((tm, dr), lambda i: (i, 0)),
            pl.BlockSpec((tm, d), lambda i: (i, ga_blk)),
            pl.BlockSpec((tm, d), lambda i: (i, gr_blk)),
            full(wao), full(wro), full(wo), full(g), full(wrh), full(wrl), full(br),
        ],
        out_specs=(
            pl.BlockSpec((tm, d), lambda i: (i, 0)),
            pl.BlockSpec((tm, d + ROUTE_LANES), lambda i: (i, 0)),
            pl.BlockSpec((SUBLANES, tm), lambda i: (0, i)),
        ),
        compiler_params=_params("arbitrary"),
        name="merge_router",
    )(x, attn, rnn, proj, proj, wao, wro, wo, g, wrh, wrl, br)


def _rank_kernel(cls_ref, lt_ref, pos_ref, tinfo_ref, cnt_ref, off_ref, run_ref, *, tm):
    ph = pl.program_id(0)
    bi = pl.program_id(1)
    cls = cls_ref[0:1, :]
    crow = lax.broadcasted_iota(jnp.int32, (CLASS_ROWS, tm), 0).astype(F32)
    member = crow == cls
    onehot = jnp.where(member, 1.0, 0.0)
    block_count = jnp.sum(onehot, axis=1, keepdims=True)

    @pl.when(jnp.logical_and(ph == 0, bi == 0))
    def _():
        cnt_ref[...] = jnp.zeros_like(cnt_ref)

    @pl.when(ph == 0)
    def _():
        cnt_ref[...] += block_count

    @pl.when(jnp.logical_and(ph == 1, bi == 0))
    def _():
        ntile = jnp.floor((cnt_ref[...] + float(MOE_TILE - 1)) * (1.0 / MOE_TILE))
        rid = lax.broadcasted_iota(jnp.int32, (CLASS_ROWS, LANES), 0)
        toff = jnp.zeros((CLASS_ROWS, LANES), F32)
        for c in range(1, CLASS_ROWS):
            toff = toff + jnp.where(rid >= c, ntile[c - 1:c, :], 0.0)
        off_ref[...] = toff * float(MOE_TILE)
        run_ref[...] = jnp.zeros_like(run_ref)
        ti = lax.broadcasted_iota(jnp.int32, (CLASS_ROWS, LANES), 1).astype(F32)
        inside = jnp.where(ti >= toff, jnp.where(ti < toff + ntile, 1.0, 0.0), 0.0)
        c = rid.astype(F32)
        grp = (jnp.where(c >= 6.0, 1.0, 0.0) + jnp.where(c >= 12.0, 1.0, 0.0)
               + jnp.where(c >= 18.0, 1.0, 0.0))
        pair = c - float(PAIRS_PER_GROUP) * grp
        a = jnp.where(pair >= 3.0, 1.0, 0.0) + jnp.where(pair >= 5.0, 1.0, 0.0)
        b = jnp.where(pair < 3.0, pair + 1.0, jnp.where(pair < 5.0, pair - 1.0, 3.0))
        e_lo = jnp.sum(inside * (float(EXPERTS_PER_GROUP) * grp + a), axis=0, keepdims=True)
        e_hi = jnp.sum(inside * (float(EXPERTS_PER_GROUP) * grp + b), axis=0, keepdims=True)
        valid = jnp.sum(inside, axis=0, keepdims=True)
        info = jnp.concatenate([e_lo, e_hi, valid, jnp.zeros((SUBLANES - 3, LANES), F32)], axis=0)
        tinfo_ref[...] = info.astype(jnp.int32)

    @pl.when(ph == 1)
    def _():
        before = _dot(onehot.astype(BF16), lt_ref[...])
        val = before + run_ref[:, 0:1] + off_ref[:, 0:1]
        pos = jnp.sum(jnp.where(member, val, 0.0), axis=0, keepdims=True)
        pos_ref[...] = pos.astype(jnp.int32)
        run_ref[...] += block_count


def _rank(cls, strict_lt):
    t = cls.shape[1]
    tm = ROW_TILE
    return pl.pallas_call(
        functools.partial(_rank_kernel, tm=tm),
        out_shape=(
            jax.ShapeDtypeStruct((1, t), jnp.int32),
            jax.ShapeDtypeStruct((SUBLANES, LANES), jnp.int32),
        ),
        grid=(2, t // tm),
        in_specs=[
            pl.BlockSpec((SUBLANES, tm), lambda ph, bi: (0, bi)),
            pl.BlockSpec((tm, tm), lambda ph, bi: (0, 0)),
        ],
        out_specs=(
            pl.BlockSpec((1, tm), lambda ph, bi: (0, bi * ph)),
            pl.BlockSpec((SUBLANES, LANES), lambda ph, bi: (0, 0)),
        ),
        scratch_shapes=[pltpu.VMEM((CLASS_ROWS, LANES), F32)] * 3,
        compiler_params=_params("arbitrary", "arbitrary"),
        name="rank_tokens",
    )(cls, strict_lt)


def _scatter_kernel(pos_ref, h_ref, init_ref, o_ref, sem, *, tm):
    del init_ref

    def row_copy(r, p):
        return pltpu.make_async_copy(h_ref.at[pl.ds(r, 1)], o_ref.at[pl.ds(p, 1)], sem)

    def start(r, c):
        row_copy(r, pos_ref[0, 0, r]).start()
        return c

    def wait(r, c):
        row_copy(0, 0).wait()
        return c

    lax.fori_loop(0, tm, start, 0, unroll=8)
    lax.fori_loop(0, tm, wait, 0, unroll=8)


def _scatter_rows(pos, rows, init):
    t, w = rows.shape
    tm = DMA_TILE
    return pl.pallas_call(
        functools.partial(_scatter_kernel, tm=tm),
        out_shape=jax.ShapeDtypeStruct(init.shape, init.dtype),
        grid=(t // tm,),
        in_specs=[
            pl.BlockSpec((1, 1, tm), lambda i: (i, 0, 0), memory_space=pltpu.SMEM),
            pl.BlockSpec((tm, w), lambda i: (i, 0)),
            pl.BlockSpec(memory_space=pl.ANY),
        ],
        out_specs=pl.BlockSpec(memory_space=pl.ANY),
        scratch_shapes=[pltpu.SemaphoreType.DMA(())],
        input_output_aliases={2: 0},
        compiler_params=_params("arbitrary"),
        name="scatter_rows",
    )(pos.reshape(t // tm, 1, tm), rows, init)


def _ffn_kernel(info_ref, x_ref, wgl_ref, wul_ref, wdl_ref, wgh_ref, wuh_ref, wdh_ref, o_ref):
    i = pl.program_id(0)
    d = o_ref.shape[1]

    @pl.when(info_ref[2, i] == 0)
    def _():
        o_ref[...] = jnp.zeros_like(o_ref)

    @pl.when(info_ref[2, i] != 0)
    def _():
        xt = x_ref[...]
        x = xt[:, :d].astype(BF16)
        y = None
        for k, (wg, wu, wd) in enumerate(((wgl_ref, wul_ref, wdl_ref), (wgh_ref, wuh_ref, wdh_ref))):
            gte = _dot(x, wg[...])
            up = _dot(x, wu[...])
            he = (gte * _sigmoid(gte)) * up
            part = xt[:, d + k:d + k + 1] * _dot(he.astype(BF16), wd[...])
            y = part if y is None else y + part
        o_ref[...] = y


def _expert_ffn(tinfo, xs, wg, wu, wd):
    npad, w = xs.shape
    _, d, de = wg.shape
    tm = MOE_TILE
    w_in = lambda row: pl.BlockSpec((None, d, de), lambda i, info: (info[row, i], 0, 0))
    w_out = lambda row: pl.BlockSpec((None, de, d), lambda i, info: (info[row, i], 0, 0))
    return pl.pallas_call(
        _ffn_kernel,
        out_shape=jax.ShapeDtypeStruct((npad, d), F32),
        grid_spec=pltpu.PrefetchScalarGridSpec(
            num_scalar_prefetch=1,
            grid=(npad // tm,),
            in_specs=[
                pl.BlockSpec((tm, w), lambda i, info: (i, 0)),
                w_in(0), w_in(0), w_out(0), w_in(1), w_in(1), w_out(1),
            ],
            out_specs=pl.BlockSpec((tm, d), lambda i, info: (i, 0)),
        ),
        compiler_params=_params("arbitrary"),
        name="expert_ffn",
    )(tinfo, xs, wg, wu, wd, wg, wu, wd)


def _ple_kernel(pos_ref, x_ref, p_ref, ys_ref, g_ref, wg_ref, wp_ref, gf_ref, o_ref, ybuf, sem,
                *, tm, final):
    def row_copy(r, p):
        return pltpu.make_async_copy(ys_ref.at[pl.ds(p, 1)], ybuf.at[pl.ds(r, 1)], sem)

    def start(r, c):
        row_copy(r, pos_ref[0, 0, r]).start()
        return c

    def wait(r, c):
        row_copy(0, 0).wait()
        return c

    lax.fori_loop(0, tm, start, 0, unroll=8)
    lax.fori_loop(0, tm, wait, 0, unroll=8)
    x2 = x_ref[...] + ybuf[...]
    gate = _sigmoid(_dot(_rms(x2, g_ref[...]).astype(BF16), wg_ref[...]))
    x3 = x2 + gate * _dot(p_ref[...].astype(BF16), wp_ref[...])
    o_ref[...] = _rms(x3, gf_ref[...]) if final else x3


def _ple(pos, x, p, ys, g, wg, wp, gf, *, final):
    t, d = x.shape
    dp = p.shape[1]
    tm = DMA_TILE
    full = lambda a: pl.BlockSpec(a.shape, lambda i: (0,) * a.ndim)
    return pl.pallas_call(
        functools.partial(_ple_kernel, tm=tm, final=final),
        out_shape=jax.ShapeDtypeStruct((t, d), F32),
        grid=(t // tm,),
        in_specs=[
            pl.BlockSpec((1, 1, tm), lambda i: (i, 0, 0), memory_space=pltpu.SMEM),
            pl.BlockSpec((tm, d), lambda i: (i, 0)),
            pl.BlockSpec((tm, dp), lambda i: (i, 0)),
            pl.BlockSpec(memory_space=pl.ANY),
            full(g), full(wg), full(wp), full(gf),
        ],
        out_specs=pl.BlockSpec((tm, d), lambda i: (i, 0)),
        scratch_shapes=[pltpu.VMEM((tm, d), F32), pltpu.SemaphoreType.DMA(())],
        compiler_params=_params("arbitrary"),
        name="gather_ple",
    )(pos.reshape(t // tm, 1, tm), x, p, ys, g, wg, wp, gf)


def _block_diag(w):
    g, n, _ = w.shape
    eye = jnp.eye(g, dtype=w.dtype)
    return (eye[:, None, :, None] * w[:, :, None, :]).reshape(g * n, g * n)


def kernel(x, p, norm_mix, w_in, conv_w, conv_b, w_rg_a, b_rg_a, w_rg_x, b_rg_x, rg_lambda,
           w_attn_o, w_rnn_o, w_out, norm_moe, w_router_group, b_router_group, w_router_expert,
           b_router_expert, w_exp_gate, w_exp_up, w_exp_down, norm_ple, w_ple_gate, w_ple_proj,
           norm_final):
    bsz, seq, d = x.shape
    depth = w_in.shape[0]
    t = bsz * seq
    d_attn = N_HEADS * HEAD_DIM
    d_rnn = conv_w.shape[2]
    assert d_attn == d_rnn and d == 2 * d_attn
    assert t % ROW_TILE == 0 and seq % ATT_TQ == 0 and seq % RNN_TS == 0 and t % DMA_TILE == 0

    qkv_end = 3 * d_attn
    rnn_end = qkv_end + 2 * d_rnn
    w_in_r = jnp.concatenate([w_in[:, :, rnn_end:], w_in[:, :, :rnn_end]], axis=2).astype(BF16)
    ga_blk, gr_blk = 0, 1
    q_col = 2 * d
    q_blk, k_blk, v_blk = (q_col // LANES, (q_col + d_attn) // LANES, (q_col + 2 * d_attn) // LANES)
    xr_blk, xg_blk = (q_col + 3 * d_attn) // d_rnn, (q_col + 3 * d_attn + d_rnn) // d_rnn

    tri_incl = jnp.tril(jnp.ones((ATT_TK, ATT_TK), F32)).astype(BF16)
    strict_lt = jnp.triu(jnp.ones((ROW_TILE, ROW_TILE), F32), k=1).astype(BF16)

    n_tiles = (t + N_CLASSES * (MOE_TILE - 1)) // MOE_TILE + 1
    assert n_tiles <= LANES
    npad = n_tiles * MOE_TILE
    sorted_init = jnp.zeros((npad, d + ROUTE_LANES), F32)

    row = lambda v: v.reshape(1, -1)
    x2d = x.reshape(t, d)
    for i in range(depth):
        proj = _inproj(x2d, row(norm_mix[i]), w_in_r[i])
        proj3 = proj.reshape(bsz, seq, -1)
        attn = _attention(proj3, tri_incl, q_blk=q_blk, k_blk=k_blk, v_blk=v_blk)
        rnn = _rglru(proj3, conv_w[i], row(conv_b[i]),
                     _block_diag(w_rg_a[i]).astype(BF16), row(b_rg_a[i]),
                     _block_diag(w_rg_x[i]).astype(BF16), row(b_rg_x[i]),
                     row(rg_lambda[i]), xr_blk=xr_blk, xg_blk=xg_blk)

        w_r = jnp.concatenate([w_router_group[i], w_router_expert[i]], axis=1).T
        w_r = jnp.pad(w_r, ((0, CLASS_ROWS - w_r.shape[0]), (0, 0)))
        w_r_hi = w_r.astype(BF16)
        w_r_lo = (w_r - w_r_hi.astype(F32)).astype(BF16)
        b_r = jnp.concatenate([b_router_group[i], b_router_expert[i]])
        b_r = jnp.pad(b_r, (0, CLASS_ROWS - b_r.shape[0])).reshape(CLASS_ROWS, 1)
        x1, routed, cls = _merge(
            x2d, attn.reshape(t, d_attn), rnn.reshape(t, d_rnn), proj,
            w_attn_o[i].astype(BF16), w_rnn_o[i].astype(BF16), w_out[i].astype(BF16),
            row(norm_moe[i]), w_r_hi, w_r_lo, b_r, ga_blk=ga_blk, gr_blk=gr_blk)

        pos, tinfo = _rank(cls, strict_lt)
        pos = pos.reshape(t)
        xs = _scatter_rows(pos, routed, sorted_init)
        ys = _expert_ffn(tinfo, xs, w_exp_gate[i].astype(BF16), w_exp_up[i].astype(BF16),
                         w_exp_down[i].astype(BF16))
        x2d = _ple(pos, x1, p[i].reshape(t, -1), ys, row(norm_ple[i]),
                   w_ple_gate[i].astype(BF16), w_ple_proj[i].astype(BF16), row(norm_final),
                   final=(i == depth - 1))
    return x2d.reshape(bsz, seq, d)
```

```python
import functools

import jax
import jax.numpy as jnp
from jax import lax
from jax.experimental import pallas as pl
from jax.experimental.pallas import tpu as pltpu

F32 = jnp.float32
BF16 = jnp.bfloat16
EPS = 1e-6

N_HEADS = 8
HEAD_DIM = 64
N_GROUPS = 4
EXPERTS_PER_GROUP = 4
N_EXPERTS = N_GROUPS * EXPERTS_PER_GROUP
PAIRS_PER_GROUP = 6
N_CLASSES = N_GROUPS * PAIRS_PER_GROUP
CLASS_ROWS = 32
CONV_WIDTH = 4
RG_C = 8.0

LANES = 128
SUBLANES = 8
VMEM_LIMIT = 56 * 1024 * 1024

ROW_TILE = 512
ATT_TQ = 512
ATT_TK = 256
RNN_TS = 256
MOE_TILE = 256
DMA_TILE = 256
ROUTE_LANES = 128

_NT = (((1,), (1,)), ((), ()))
LOG2E = 1.4426950408889634
UNDERFLOW_EXPONENT = 151.0 / LOG2E


def _params(*sem):
    return pltpu.CompilerParams(dimension_semantics=sem, vmem_limit_bytes=VMEM_LIMIT)


def _rms(x, g):
    ms = jnp.mean(x * x, axis=-1, keepdims=True)
    return x * lax.rsqrt(ms + EPS) * g


def _sigmoid(x):
    return 1.0 / (1.0 + jnp.exp(-x))


def _dot(a, b):
    return jnp.dot(a, b, preferred_element_type=F32)


def _inproj_kernel(x_ref, g_ref, w_ref, o_ref, *, col_chunk):
    h = _rms(x_ref[...], g_ref[...]).astype(BF16)
    for n in range(0, o_ref.shape[1], col_chunk):
        o_ref[:, n:n + col_chunk] = _dot(h, w_ref[:, n:n + col_chunk]).astype(BF16)


def _inproj(x, g, w):
    t, d = x.shape
    n = w.shape[1]
    return pl.pallas_call(
        functools.partial(_inproj_kernel, col_chunk=512),
        out_shape=jax.ShapeDtypeStruct((t, n), BF16),
        grid=(t // ROW_TILE,),
        in_specs=[
            pl.BlockSpec((ROW_TILE, d), lambda i: (i, 0)),
            pl.BlockSpec((1, d), lambda i: (0, 0)),
            pl.BlockSpec((d, n), lambda i: (0, 0)),
        ],
        out_specs=pl.BlockSpec((ROW_TILE, n), lambda i: (i, 0)),
        compiler_params=_params("arbitrary"),
        name="inproj",
    )(x, g, w)


def _attn_tile(q_heads, kb, vb, u, acc_ref, carry_ref, rows, causal):
    tk = kb.shape[0]
    lane = lax.broadcasted_iota(jnp.int32, (1, LANES), 1)
    first_head = lane < HEAD_DIM
    vz = jnp.zeros_like(vb)
    v_heads = (jnp.where(first_head, vb, vz), jnp.where(first_head, vz, vb))
    pv = None
    for h in range(2):
        z = lax.dot_general(q_heads[h], kb, _NT, preferred_element_type=F32)
        sp = jnp.maximum(z, 0.0) + jnp.log(1.0 + jnp.exp2(jnp.abs(z) * (-LOG2E)))
        if causal is not None:
            sp = jnp.where(causal, sp, 0.0)
        suffix = _dot(sp.astype(BF16), u)
        carry = carry_ref[h, rows, :]
        w = jnp.exp2((z - suffix - jnp.tile(carry, (1, tk // LANES))) * LOG2E)
        if causal is not None:
            w = jnp.where(causal, w, 0.0)
        d = _dot(w.astype(BF16), v_heads[h])
        pv = d if pv is None else pv + d
        carry_ref[h, rows, :] = carry + suffix[:, 0:1]
    acc_ref[rows, :] += pv


def _attn_kernel(q_ref, k_ref, v_ref, u_ref, o_ref, acc_ref, carry_ref, *, tq, tk):
    qi = pl.program_id(2)
    lane = lax.broadcasted_iota(jnp.int32, (1, LANES), 1)
    first_head = lane < HEAD_DIM
    q = q_ref[...] * jnp.asarray(HEAD_DIM ** -0.5, BF16)
    qz = jnp.zeros_like(q)
    q_heads = (jnp.where(first_head, q, qz), jnp.where(first_head, qz, q))
    acc_ref[...] = jnp.zeros_like(acc_ref)
    carry_ref[...] = jnp.zeros_like(carry_ref)
    u = u_ref[...]
    q0 = qi * tq

    for d in range(tq // tk - 1, -1, -1):
        r0 = d * tk
        rows = slice(r0, tq)
        ks = pl.multiple_of(q0 + r0, tk)
        row = lax.broadcasted_iota(jnp.int32, (tq - r0, tk), 0)
        col = lax.broadcasted_iota(jnp.int32, (tq - r0, tk), 1)
        _attn_tile([qh[rows] for qh in q_heads], k_ref[pl.ds(ks, tk), :], v_ref[pl.ds(ks, tk), :],
                   u, acc_ref, carry_ref, rows, col < row)

    nfull = qi * (tq // tk)

    def more(state):
        step, min_carry = state
        return jnp.logical_and(step < nfull, min_carry < UNDERFLOW_EXPONENT)

    def body(state):
        step, _ = state
        ks = pl.multiple_of((nfull - 1 - step) * tk, tk)
        _attn_tile(q_heads, k_ref[pl.ds(ks, tk), :], v_ref[pl.ds(ks, tk), :],
                   u, acc_ref, carry_ref, slice(0, tq), None)
        return step + 1, jnp.min(carry_ref[...])

    lax.while_loop(more, body, (jnp.int32(0), jnp.min(carry_ref[...])))
    o_ref[...] = acc_ref[...].astype(BF16)


def _attention(proj, u, *, q_blk, k_blk, v_blk):
    b, s, _ = proj.shape
    hp = N_HEADS * HEAD_DIM // LANES
    return pl.pallas_call(
        functools.partial(_attn_kernel, tq=ATT_TQ, tk=ATT_TK),
        out_shape=jax.ShapeDtypeStruct((b, s, N_HEADS * HEAD_DIM), BF16),
        grid=(b, hp, s // ATT_TQ),
        in_specs=[
            pl.BlockSpec((None, ATT_TQ, LANES), lambda bi, hi, qi: (bi, qi, q_blk + hi)),
            pl.BlockSpec((None, s, LANES), lambda bi, hi, qi: (bi, 0, k_blk + hi)),
            pl.BlockSpec((None, s, LANES), lambda bi, hi, qi: (bi, 0, v_blk + hi)),
            pl.BlockSpec((ATT_TK, ATT_TK), lambda bi, hi, qi: (0, 0)),
        ],
        out_specs=pl.BlockSpec((None, ATT_TQ, LANES), lambda bi, hi, qi: (bi, qi, hi)),
        scratch_shapes=[
            pltpu.VMEM((ATT_TQ, LANES), F32),
            pltpu.VMEM((2, ATT_TQ, LANES), F32),
        ],
        compiler_params=_params("arbitrary", "arbitrary", "arbitrary"),
        name="sb_attention",
    )(proj, proj, proj, u)


def _rglru_kernel(xr_ref, xg_ref, cw_ref, cb_ref, wa_ref, ba_ref, wx_ref, bx_ref, lam_ref,
                  o_ref, tail_ref, h_ref, a_scr, b_scr, *, ts):
    @pl.when(pl.program_id(1) == 0)
    def _():
        tail_ref[...] = jnp.zeros_like(tail_ref)
        h_ref[...] = jnp.zeros_like(h_ref)

    xr = xr_ref[...].astype(F32)
    ext = jnp.concatenate([tail_ref[...], xr], axis=0)
    tail_ref[...] = xr[ts - SUBLANES:, :]
    xc = cb_ref[...]
    for j in range(CONV_WIDTH):
        lo = SUBLANES - (CONV_WIDTH - 1) + j
        xc = xc + cw_ref[j:j + 1, :] * ext[lo:lo + ts, :]
    xcb = xc.astype(BF16)
    r = _sigmoid(_dot(xcb, wa_ref[...]) + ba_ref[...])
    ig = _sigmoid(_dot(xcb, wx_ref[...]) + bx_ref[...])
    lam = lam_ref[...]
    softplus_neg_lam = jnp.maximum(-lam, 0.0) + jnp.log(1.0 + jnp.exp(-jnp.abs(lam)))
    log_a = (-RG_C * softplus_neg_lam) * r
    a = jnp.exp(log_a)
    a_scr[...] = a
    b_scr[...] = jnp.sqrt(1.0 - a * a) * (ig * xc)

    def step(t, h):
        h = a_scr[pl.ds(t, 1), :] * h + b_scr[pl.ds(t, 1), :]
        b_scr[pl.ds(t, 1), :] = h
        return h

    h_ref[...] = lax.fori_loop(0, ts, step, h_ref[...], unroll=8)
    xg = xg_ref[...].astype(F32)
    gelu = 0.5 * xg * (1.0 + jnp.tanh(0.7978845608028654 * (xg + 0.044715 * (xg * xg * xg))))
    o_ref[...] = (b_scr[...] * gelu).astype(BF16)


def _rglru(proj, cw, cb, wa, ba, wx, bx, lam, *, xr_blk, xg_blk):
    b, s, _ = proj.shape
    dr = cw.shape[1]
    vec = lambda: pl.BlockSpec((1, dr), lambda bi, si: (0, 0))
    return pl.pallas_call(
        functools.partial(_rglru_kernel, ts=RNN_TS),
        out_shape=jax.ShapeDtypeStruct((b, s, dr), BF16),
        grid=(b, s // RNN_TS),
        in_specs=[
            pl.BlockSpec((None, RNN_TS, dr), lambda bi, si: (bi, si, xr_blk)),
            pl.BlockSpec((None, RNN_TS, dr), lambda bi, si: (bi, si, xg_blk)),
            pl.BlockSpec((CONV_WIDTH, dr), lambda bi, si: (0, 0)),
            vec(),
            pl.BlockSpec((dr, dr), lambda bi, si: (0, 0)),
            vec(),
            pl.BlockSpec((dr, dr), lambda bi, si: (0, 0)),
            vec(),
            vec(),
        ],
        out_specs=pl.BlockSpec((None, RNN_TS, dr), lambda bi, si: (bi, si, 0)),
        scratch_shapes=[
            pltpu.VMEM((SUBLANES, dr), F32),
            pltpu.VMEM((1, dr), F32),
            pltpu.VMEM((RNN_TS, dr), F32),
            pltpu.VMEM((RNN_TS, dr), F32),
        ],
        compiler_params=_params("arbitrary", "arbitrary"),
        name="rglru",
    )(proj, proj, cw, cb, wa, ba, wx, bx, lam)


def _first_argmax(vals):
    m = vals[0]
    for v in vals[1:]:
        m = jnp.maximum(m, v)
    idx = jnp.full_like(m, float(len(vals) - 1))
    for k in range(len(vals) - 2, -1, -1):
        idx = jnp.where(vals[k] == m, float(k), idx)
    return m, idx


def _merge_kernel(x_ref, at_ref, rn_ref, ga_ref, gr_ref, wao_ref, wro_ref, wo_ref, g_ref,
                  wrh_ref, wrl_ref, br_ref, x1_ref, hr_ref, cls_ref):
    d = x_ref.shape[1]
    ya = _dot(at_ref[...], wao_ref[...])
    yr = _dot(rn_ref[...], wro_ref[...])
    merged = _sigmoid(ga_ref[...].astype(F32)) * ya + _sigmoid(gr_ref[...].astype(F32)) * yr
    x1 = x_ref[...] + _dot(merged.astype(BF16), wo_ref[...])
    x1_ref[...] = x1
    h = _rms(x1, g_ref[...])
    hr_ref[:, :d] = h

    h_hi = h.astype(BF16)
    h_lo = (h - h_hi.astype(F32)).astype(BF16)
    wrh = wrh_ref[...]
    lt = (lax.dot_general(wrh, h_hi, _NT, preferred_element_type=F32)
          + lax.dot_general(wrh, h_lo, _NT, preferred_element_type=F32)
          + lax.dot_general(wrl_ref[...], h_hi, _NT, preferred_element_type=F32))
    lt = lt + br_ref[...]
    g = [lt[k:k + 1, :] for k in range(N_GROUPS)]
    e = [lt[N_GROUPS + k:N_GROUPS + k + 1, :] for k in range(N_EXPERTS)]

    gmax, gi = _first_argmax(g)
    den = jnp.exp(g[0] - gmax)
    for k in range(1, N_GROUPS):
        den = den + jnp.exp(g[k] - gmax)
    gate = 1.0 / den
    sel = []
    for j in range(EXPERTS_PER_GROUP):
        s = e[(N_GROUPS - 1) * EXPERTS_PER_GROUP + j]
        for k in range(N_GROUPS - 2, -1, -1):
            s = jnp.where(gi == float(k), e[k * EXPERTS_PER_GROUP + j], s)
        sel.append(s)
    m1, i1 = _first_argmax(sel)
    sel2 = [jnp.where(i1 == float(j), -jnp.inf, sel[j]) for j in range(EXPERTS_PER_GROUP)]
    m2, i2 = _first_argmax(sel2)
    t = jnp.exp(m2 - m1)
    w1 = gate * (1.0 / (1.0 + t))
    w2 = gate * (t / (1.0 + t))
    first_is_lo = i1 < i2
    a = jnp.minimum(i1, i2)
    b = jnp.maximum(i1, i2)
    w_lo = jnp.where(first_is_lo, w1, w2)
    w_hi = jnp.where(first_is_lo, w2, w1)
    pair = jnp.where(a == 0.0, b - 1.0, jnp.where(a == 1.0, b + 1.0, 5.0))
    cls = gi * float(PAIRS_PER_GROUP) + pair

    n = cls.shape[1]
    cls_ref[...] = jnp.concatenate([cls, jnp.zeros((SUBLANES - 1, n), F32)], axis=0)
    rows = jnp.concatenate([w_lo, w_hi, jnp.zeros((ROUTE_LANES - 2, n), F32)], axis=0)
    hr_ref[:, d:] = rows.T


def _merge(x, attn, rnn, proj, wao, wro, wo, g, wrh, wrl, br, *, ga_blk, gr_blk):
    t, d = x.shape
    da = attn.shape[1]
    dr = rnn.shape[1]
    tm = ROW_TILE
    full = lambda a: pl.BlockSpec(a.shape, lambda i: (0,) * a.ndim)
    return pl.pallas_call(
        _merge_kernel,
        out_shape=(
            jax.ShapeDtypeStruct((t, d), F32),
            jax.ShapeDtypeStruct((t, d + ROUTE_LANES), F32),
            jax.ShapeDtypeStruct((SUBLANES, t), F32),
        ),
        grid=(t // tm,),
        in_specs=[
            pl.BlockSpec((tm, d), lambda i: (i, 0)),
            pl.BlockSpec((tm, da), lambda i: (i, 0)),
            pl.BlockSpec((tm, dr), lambda i: (i, 0)),
            pl.BlockSpec((tm, d), lambda i: (i, ga_blk)),
            pl.BlockSpec((tm, d), lambda i: (i, gr_blk)),
            full(wao), full(wro), full(wo), full(g), full(wrh), full(wrl), full(br),
        ],
        out_specs=(
            pl.BlockSpec((tm, d), lambda i: (i, 0)),
            pl.BlockSpec((tm, d + ROUTE_LANES), lambda i: (i, 0)),
            pl.BlockSpec((SUBLANES, tm), lambda i: (0, i)),
        ),
        compiler_params=_params("arbitrary"),
        name="merge_router",
    )(x, attn, rnn, proj, proj, wao, wro, wo, g, wrh, wrl, br)


def _rank_kernel(cls_ref, lt_ref, pos_ref, tinfo_ref, cnt_ref, off_ref, run_ref, *, tm):
    ph = pl.program_id(0)
    bi = pl.program_id(1)
    cls = cls_ref[0:1, :]
    crow = lax.broadcasted_iota(jnp.int32, (CLASS_ROWS, tm), 0).astype(F32)
    member = crow == cls
    onehot = jnp.where(member, 1.0, 0.0)
    block_count = jnp.sum(onehot, axis=1, keepdims=True)

    @pl.when(jnp.logical_and(ph == 0, bi == 0))
    def _():
        cnt_ref[...] = jnp.zeros_like(cnt_ref)

    @pl.when(ph == 0)
    def _():
        cnt_ref[...] += block_count

    @pl.when(jnp.logical_and(ph == 1, bi == 0))
    def _():
        ntile = jnp.floor((cnt_ref[...] + float(MOE_TILE - 1)) * (1.0 / MOE_TILE))
        rid = lax.broadcasted_iota(jnp.int32, (CLASS_ROWS, LANES), 0)
        toff = jnp.zeros((CLASS_ROWS, LANES), F32)
        for c in range(1, CLASS_ROWS):
            toff = toff + jnp.where(rid >= c, ntile[c - 1:c, :], 0.0)
        off_ref[...] = toff * float(MOE_TILE)
        run_ref[...] = jnp.zeros_like(run_ref)
        ti = lax.broadcasted_iota(jnp.int32, (CLASS_ROWS, LANES), 1).astype(F32)
        inside = jnp.where(ti >= toff, jnp.where(ti < toff + ntile, 1.0, 0.0), 0.0)
        c = rid.astype(F32)
        grp = (jnp.where(c >= 6.0, 1.0, 0.0) + jnp.where(c >= 12.0, 1.0, 0.0)
               + jnp.where(c >= 18.0, 1.0, 0.0))
        pair = c - float(PAIRS_PER_GROUP) * grp
        a = jnp.where(pair >= 3.0, 1.0, 0.0) + jnp.where(pair >= 5.0, 1.0, 0.0)
        b = jnp.where(pair < 3.0, pair + 1.0, jnp.where(pair < 5.0, pair - 1.0, 3.0))
        e_lo = jnp.sum(inside * (float(EXPERTS_PER_GROUP) * grp + a), axis=0, keepdims=True)
        e_hi = jnp.sum(inside * (float(EXPERTS_PER_GROUP) * grp + b), axis=0, keepdims=True)
        valid = jnp.sum(inside, axis=0, keepdims=True)
        info = jnp.concatenate([e_lo, e_hi, valid, jnp.zeros((SUBLANES - 3, LANES), F32)], axis=0)
        tinfo_ref[...] = info.astype(jnp.int32)

    @pl.when(ph == 1)
    def _():
        before = _dot(onehot.astype(BF16), lt_ref[...])
        val = before + run_ref[:, 0:1] + off_ref[:, 0:1]
        pos = jnp.sum(jnp.where(member, val, 0.0), axis=0, keepdims=True)
        pos_ref[...] = pos.astype(jnp.int32)
        run_ref[...] += block_count


def _rank(cls, strict_lt):
    t = cls.shape[1]
    tm = ROW_TILE
    return pl.pallas_call(
        functools.partial(_rank_kernel, tm=tm),
        out_shape=(
            jax.ShapeDtypeStruct((1, t), jnp.int32),
            jax.ShapeDtypeStruct((SUBLANES, LANES), jnp.int32),
        ),
        grid=(2, t // tm),
        in_specs=[
            pl.BlockSpec((SUBLANES, tm), lambda ph, bi: (0, bi)),
            pl.BlockSpec((tm, tm), lambda ph, bi: (0, 0)),
        ],
        out_specs=(
            pl.BlockSpec((1, tm), lambda ph, bi: (0, bi * ph)),
            pl.BlockSpec((SUBLANES, LANES), lambda ph, bi: (0, 0)),
        ),
        scratch_shapes=[pltpu.VMEM((CLASS_ROWS, LANES), F32)] * 3,
        compiler_params=_params("arbitrary", "arbitrary"),
        name="rank_tokens",
    )(cls, strict_lt)


def _scatter_kernel(pos_ref, h_ref, init_ref, o_ref, sem, *, tm):
    del init_ref

    def row_copy(r, p):
        return pltpu.make_async_copy(h_ref.at[pl.ds(r, 1)], o_ref.at[pl.ds(p, 1)], sem)

    def start(r, c):
        row_copy(r, pos_ref[0, 0, r]).start()
        return c

    def wait(r, c):
        row_copy(0, 0).wait()
        return c

    lax.fori_loop(0, tm, start, 0, unroll=8)
    lax.fori_loop(0, tm, wait, 0, unroll=8)


def _scatter_rows(pos, rows, init):
    t, w = rows.shape
    tm = DMA_TILE
    return pl.pallas_call(
        functools.partial(_scatter_kernel, tm=tm),
        out_shape=jax.ShapeDtypeStruct(init.shape, init.dtype),
        grid=(t // tm,),
        in_specs=[
            pl.BlockSpec((1, 1, tm), lambda i: (i, 0, 0), memory_space=pltpu.SMEM),
            pl.BlockSpec((tm, w), lambda i: (i, 0)),
            pl.BlockSpec(memory_space=pl.ANY),
        ],
        out_specs=pl.BlockSpec(memory_space=pl.ANY),
        scratch_shapes=[pltpu.SemaphoreType.DMA(())],
        input_output_aliases={2: 0},
        compiler_params=_params("arbitrary"),
        name="scatter_rows",
    )(pos.reshape(t // tm, 1, tm), rows, init)


def _ffn_kernel(info_ref, x_ref, wgl_ref, wul_ref, wdl_ref, wgh_ref, wuh_ref, wdh_ref, o_ref):
    i = pl.program_id(0)
    d = o_ref.shape[1]

    @pl.when(info_ref[2, i] == 0)
    def _():
        o_ref[...] = jnp.zeros_like(o_ref)

    @pl.when(info_ref[2, i] != 0)
    def _():
        xt = x_ref[...]
        x = xt[:, :d].astype(BF16)
        y = None
        for k, (wg, wu, wd) in enumerate(((wgl_ref, wul_ref, wdl_ref), (wgh_ref, wuh_ref, wdh_ref))):
            gte = _dot(x, wg[...])
            up = _dot(x, wu[...])
            he = (gte * _sigmoid(gte)) * up
            part = xt[:, d + k:d + k + 1] * _dot(he.astype(BF16), wd[...])
            y = part if y is None else y + part
        o_ref[...] = y


def _expert_ffn(tinfo, xs, wg, wu, wd):
    npad, w = xs.shape
    _, d, de = wg.shape
    tm = MOE_TILE
    w_in = lambda row: pl.BlockSpec((None, d, de), lambda i, info: (info[row, i], 0, 0))
    w_out = lambda row: pl.BlockSpec((None, de, d), lambda i, info: (info[row, i], 0, 0))
    return pl.pallas_call(
        _ffn_kernel,
        out_shape=jax.ShapeDtypeStruct((npad, d), F32),
        grid_spec=pltpu.PrefetchScalarGridSpec(
            num_scalar_prefetch=1,
            grid=(npad // tm,),
            in_specs=[
                pl.BlockSpec((tm, w), lambda i, info: (i, 0)),
                w_in(0), w_in(0), w_out(0), w_in(1), w_in(1), w_out(1),
            ],
            out_specs=pl.BlockSpec((tm, d), lambda i, info: (i, 0)),
        ),
        compiler_params=_params("arbitrary"),
        name="expert_ffn",
    )(tinfo, xs, wg, wu, wd, wg, wu, wd)


def _ple_kernel(pos_ref, x_ref, p_ref, ys_ref, g_ref, wg_ref, wp_ref, gf_ref, o_ref, ybuf, sem,
                *, tm, final):
    def row_copy(r, p):
        return pltpu.make_async_copy(ys_ref.at[pl.ds(p, 1)], ybuf.at[pl.ds(r, 1)], sem)

    def start(r, c):
        row_copy(r, pos_ref[0, 0, r]).start()
        return c

    def wait(r, c):
        row_copy(0, 0).wait()
        return c

    lax.fori_loop(0, tm, start, 0, unroll=8)
    lax.fori_loop(0, tm, wait, 0, unroll=8)
    x2 = x_ref[...] + ybuf[...]
    gate = _sigmoid(_dot(_rms(x2, g_ref[...]).astype(BF16), wg_ref[...]))
    x3 = x2 + gate * _dot(p_ref[...].astype(BF16), wp_ref[...])
    o_ref[...] = _rms(x3, gf_ref[...]) if final else x3


def _ple(pos, x, p, ys, g, wg, wp, gf, *, final):
    t, d = x.shape
    dp = p.shape[1]
    tm = DMA_TILE
    full = lambda a: pl.BlockSpec(a.shape, lambda i: (0,) * a.ndim)
    return pl.pallas_call(
        functools.partial(_ple_kernel, tm=tm, final=final),
        out_shape=jax.ShapeDtypeStruct((t, d), F32),
        grid=(t // tm,),
        in_specs=[
            pl.BlockSpec((1, 1, tm), lambda i: (i, 0, 0), memory_space=pltpu.SMEM),
            pl.BlockSpec((tm, d), lambda i: (i, 0)),
            pl.BlockSpec((tm, dp), lambda i: (i, 0)),
            pl.BlockSpec(memory_space=pl.ANY),
            full(g), full(wg), full(wp), full(gf),
        ],
        out_specs=pl.BlockSpec((tm, d), lambda i: (i, 0)),
        scratch_shapes=[pltpu.VMEM((tm, d), F32), pltpu.SemaphoreType.DMA(())],
        compiler_params=_params("arbitrary"),
        name="gather_ple",
    )(pos.reshape(t // tm, 1, tm), x, p, ys, g, wg, wp, gf)


def _block_diag(w):
    g, n, _ = w.shape
    eye = jnp.eye(g, dtype=w.dtype)
    return (eye[:, None, :, None] * w[:, :, None, :]).reshape(g * n, g * n)


def kernel(x, p, norm_mix, w_in, conv_w, conv_b, w_rg_a, b_rg_a, w_rg_x, b_rg_x, rg_lambda,
           w_attn_o, w_rnn_o, w_out, norm_moe, w_router_group, b_router_group, w_router_expert,
           b_router_expert, w_exp_gate, w_exp_up, w_exp_down, norm_ple, w_ple_gate, w_ple_proj,
           norm_final):
    bsz, seq, d = x.shape
    depth = w_in.shape[0]
    t = bsz * seq
    d_attn = N_HEADS * HEAD_DIM
    d_rnn = conv_w.shape[2]
    assert d_attn == d_rnn and d == 2 * d_attn
    assert t % ROW_TILE == 0 and seq % ATT_TQ == 0 and seq % RNN_TS == 0 and t % DMA_TILE == 0

    qkv_end = 3 * d_attn
    rnn_end = qkv_end + 2 * d_rnn
    w_in_r = jnp.concatenate([w_in[:, :, rnn_end:], w_in[:, :, :rnn_end]], axis=2).astype(BF16)
    ga_blk, gr_blk = 0, 1
    q_col = 2 * d
    q_blk, k_blk, v_blk = (q_col // LANES, (q_col + d_attn) // LANES, (q_col + 2 * d_attn) // LANES)
    xr_blk, xg_blk = (q_col + 3 * d_attn) // d_rnn, (q_col + 3 * d_attn + d_rnn) // d_rnn

    tri_incl = jnp.tril(jnp.ones((ATT_TK, ATT_TK), F32)).astype(BF16)
    strict_lt = jnp.triu(jnp.ones((ROW_TILE, ROW_TILE), F32), k=1).astype(BF16)

    n_tiles = (t + N_CLASSES * (MOE_TILE - 1)) // MOE_TILE + 1
    assert n_tiles <= LANES
    npad = n_tiles * MOE_TILE
    sorted_init = jnp.zeros((npad, d + ROUTE_LANES), F32)

    row = lambda v: v.reshape(1, -1)
    x2d = x.reshape(t, d)
    for i in range(depth):
        proj = _inproj(x2d, row(norm_mix[i]), w_in_r[i])
        proj3 = proj.reshape(bsz, seq, -1)
        attn = _attention(proj3, tri_incl, q_blk=q_blk, k_blk=k_blk, v_blk=v_blk)
        rnn = _rglru(proj3, conv_w[i], row(conv_b[i]),
                     _block_diag(w_rg_a[i]).astype(BF16), row(b_rg_a[i]),
                     _block_diag(w_rg_x[i]).astype(BF16), row(b_rg_x[i]),
                     row(rg_lambda[i]), xr_blk=xr_blk, xg_blk=xg_blk)

        w_r = jnp.concatenate([w_router_group[i], w_router_expert[i]], axis=1).T
        w_r = jnp.pad(w_r, ((0, CLASS_ROWS - w_r.shape[0]), (0, 0)))
        w_r_hi = w_r.astype(BF16)
        w_r_lo = (w_r - w_r_hi.astype(F32)).astype(BF16)
        b_r = jnp.concatenate([b_router_group[i], b_router_expert[i]])
        b_r = jnp.pad(b_r, (0, CLASS_ROWS - b_r.shape[0])).reshape(CLASS_ROWS, 1)
        x1, routed, cls = _merge(
            x2d, attn.reshape(t, d_attn), rnn.reshape(t, d_rnn), proj,
            w_attn_o[i].astype(BF16), w_rnn_o[i].astype(BF16), w_out[i].astype(BF16),
            row(norm_moe[i]), w_r_hi, w_r_lo, b_r, ga_blk=ga_blk, gr_blk=gr_blk)

        pos, tinfo = _rank(cls, strict_lt)
        pos = pos.reshape(t)
        xs = _scatter_rows(pos, routed, sorted_init)
        ys = _expert_ffn(tinfo, xs, w_exp_gate[i].astype(BF16), w_exp_up[i].astype(BF16),
                         w_exp_down[i].astype(BF16))
        x2d = _ple(pos, x1, p[i].reshape(t, -1), ys, row(norm_ple[i]),
                   w_ple_gate[i].astype(BF16), w_ple_proj[i].astype(BF16), row(norm_final),
                   final=(i == depth - 1))
    return x2d.reshape(bsz, seq, d)
```

```python
import functools

import jax
import jax.numpy as jnp
from jax import lax
from jax.experimental import pallas as pl
from jax.experimental.pallas import tpu as pltpu

F32 = jnp.float32
BF16 = jnp.bfloat16
EPS = 1e-6

N_HEADS = 8
HEAD_DIM = 64
N_GROUPS = 4
EXPERTS_PER_GROUP = 4
N_EXPERTS = N_GROUPS * EXPERTS_PER_GROUP
PAIRS_PER_GROUP = 6
N_CLASSES = N_GROUPS * PAIRS_PER_GROUP
CLASS_ROWS = 32
CONV_WIDTH = 4
RG_C = 8.0

LANES = 128
SUBLANES = 8
VMEM_LIMIT = 56 * 1024 * 1024

ROW_TILE = 512
ATT_TQ = 512
ATT_TK = 256
RNN_TS = 256
MOE_TILE = 256
SCATTER_TILE = 1024
PLE_TILE = 256
ROUTE_LANES = 128
DMA_UNROLL = 8
COL_CHUNK = 512

_NT = (((1,), (1,)), ((), ()))
LOG2E = 1.4426950408889634
UNDERFLOW_EXPONENT = 151.0 / LOG2E


def _params(*sem):
    return pltpu.CompilerParams(dimension_semantics=sem, vmem_limit_bytes=VMEM_LIMIT)


def _layer_spec(arr, layer):
    nd = arr.ndim
    return pl.BlockSpec((None,) + arr.shape[1:], lambda *_: (layer,) + (0,) * (nd - 1))


def _rms(x, g):
    ms = jnp.mean(x * x, axis=-1, keepdims=True)
    return x * lax.rsqrt(ms + EPS) * g


def _sigmoid(x):
    return 1.0 / (1.0 + jnp.exp(-x))


def _dot(a, b):
    return jnp.dot(a, b, preferred_element_type=F32)


def _in_projection(x, g, w_ref, o_ref, col_map):
    h = _rms(x, g).astype(BF16)
    for src, dst in col_map:
        o_ref[:, dst:dst + COL_CHUNK] = _dot(h, w_ref[:, src:src + COL_CHUNK]).astype(BF16)


def _inproj_kernel(x_ref, g_ref, w_ref, o_ref, *, col_map):
    _in_projection(x_ref[...], g_ref[...], w_ref, o_ref, col_map)


def _inproj(x, g, w, layer, col_map):
    t, d = x.shape
    n = w.shape[2]
    return pl.pallas_call(
        functools.partial(_inproj_kernel, col_map=col_map),
        out_shape=jax.ShapeDtypeStruct((t, n), BF16),
        grid=(t // ROW_TILE,),
        in_specs=[
            pl.BlockSpec((ROW_TILE, d), lambda i: (i, 0)),
            _layer_spec(g, layer),
            _layer_spec(w, layer),
        ],
        out_specs=pl.BlockSpec((ROW_TILE, n), lambda i: (i, 0)),
        compiler_params=_params("arbitrary"),
        name="inproj",
    )(x, g, w)


def _attn_tile(q_heads, kb, vb, u, acc_ref, carry_ref, rows, causal):
    tk = kb.shape[0]
    lane = lax.broadcasted_iota(jnp.int32, (1, LANES), 1)
    first_head = lane < HEAD_DIM
    vz = jnp.zeros_like(vb)
    v_heads = (jnp.where(first_head, vb, vz), jnp.where(first_head, vz, vb))
    pv = None
    for h in range(2):
        z = lax.dot_general(q_heads[h], kb, _NT, preferred_element_type=F32)
        sp = jnp.maximum(z, 0.0) + jnp.log(1.0 + jnp.exp2(jnp.abs(z) * (-LOG2E)))
        if causal is not None:
            sp = jnp.where(causal, sp, 0.0)
        suffix = _dot(sp.astype(BF16), u)
        carry = carry_ref[h, rows, :]
        w = jnp.exp2((z - suffix - jnp.tile(carry, (1, tk // LANES))) * LOG2E)
        if causal is not None:
            w = jnp.where(causal, w, 0.0)
        d = _dot(w.astype(BF16), v_heads[h])
        pv = d if pv is None else pv + d
        carry_ref[h, rows, :] = carry + suffix[:, 0:1]
    acc_ref[rows, :] += pv


def _attn_kernel(q_ref, k_ref, v_ref, u_ref, o_ref, acc_ref, carry_ref, *, tq, tk):
    qi = pl.program_id(2)
    lane = lax.broadcasted_iota(jnp.int32, (1, LANES), 1)
    first_head = lane < HEAD_DIM
    q = q_ref[...] * jnp.asarray(HEAD_DIM ** -0.5, BF16)
    qz = jnp.zeros_like(q)
    q_heads = (jnp.where(first_head, q, qz), jnp.where(first_head, qz, q))
    acc_ref[...] = jnp.zeros_like(acc_ref)
    carry_ref[...] = jnp.zeros_like(carry_ref)
    u = u_ref[...]
    q0 = qi * tq

    for d in range(tq // tk - 1, -1, -1):
        r0 = d * tk
        rows = slice(r0, tq)
        ks = pl.multiple_of(q0 + r0, tk)
        row = lax.broadcasted_iota(jnp.int32, (tq - r0, tk), 0)
        col = lax.broadcasted_iota(jnp.int32, (tq - r0, tk), 1)
        _attn_tile([qh[rows] for qh in q_heads], k_ref[pl.ds(ks, tk), :], v_ref[pl.ds(ks, tk), :],
                   u, acc_ref, carry_ref, rows, col < row)

    nfull = qi * (tq // tk)

    def more(state):
        step, min_carry = state
        return jnp.logical_and(step < nfull, min_carry < UNDERFLOW_EXPONENT)

    def body(state):
        step, _ = state
        ks = pl.multiple_of((nfull - 1 - step) * tk, tk)
        _attn_tile(q_heads, k_ref[pl.ds(ks, tk), :], v_ref[pl.ds(ks, tk), :],
                   u, acc_ref, carry_ref, slice(0, tq), None)
        return step + 1, jnp.min(carry_ref[...])

    lax.while_loop(more, body, (jnp.int32(0), jnp.min(carry_ref[...])))
    o_ref[...] = acc_ref[...].astype(BF16)


def _attention(proj, u, *, q_blk, k_blk, v_blk):
    b, s, _ = proj.shape
    hp = N_HEADS * HEAD_DIM // LANES
    return pl.pallas_call(
        functools.partial(_attn_kernel, tq=ATT_TQ, tk=ATT_TK),
        out_shape=jax.ShapeDtypeStruct((b, s, N_HEADS * HEAD_DIM), BF16),
        grid=(b, hp, s // ATT_TQ),
        in_specs=[
            pl.BlockSpec((None, ATT_TQ, LANES), lambda bi, hi, qi: (bi, qi, q_blk + hi)),
            pl.BlockSpec((None, s, LANES), lambda bi, hi, qi: (bi, 0, k_blk + hi)),
            pl.BlockSpec((None, s, LANES), lambda bi, hi, qi: (bi, 0, v_blk + hi)),
            pl.BlockSpec((ATT_TK, ATT_TK), lambda bi, hi, qi: (0, 0)),
        ],
        out_specs=pl.BlockSpec((None, ATT_TQ, LANES), lambda bi, hi, qi: (bi, qi, hi)),
        scratch_shapes=[
            pltpu.VMEM((ATT_TQ, LANES), F32),
            pltpu.VMEM((2, ATT_TQ, LANES), F32),
        ],
        compiler_params=_params("arbitrary", "arbitrary", "arbitrary"),
        name="sb_attention",
    )(proj, proj, proj, u)


def _rglru_kernel(xr_ref, xg_ref, cw_ref, cb_ref, wa_ref, ba_ref, wx_ref, bx_ref, lam_ref,
                  o_ref, tail_ref, h_ref, a_scr, b_scr, *, ts):
    @pl.when(pl.program_id(1) == 0)
    def _():
        tail_ref[...] = jnp.zeros_like(tail_ref)
        h_ref[...] = jnp.zeros_like(h_ref)

    xr = xr_ref[...].astype(F32)
    ext = jnp.concatenate([tail_ref[...], xr], axis=0)
    tail_ref[...] = xr[ts - SUBLANES:, :]
    xc = cb_ref[...]
    for j in range(CONV_WIDTH):
        lo = SUBLANES - (CONV_WIDTH - 1) + j
        xc = xc + cw_ref[j:j + 1, :] * ext[lo:lo + ts, :]
    xcb = xc.astype(BF16)
    r = _sigmoid(_dot(xcb, wa_ref[...]) + ba_ref[...])
    ig = _sigmoid(_dot(xcb, wx_ref[...]) + bx_ref[...])
    lam = lam_ref[...]
    softplus_neg_lam = jnp.maximum(-lam, 0.0) + jnp.log(1.0 + jnp.exp(-jnp.abs(lam)))
    log_a = (-RG_C * softplus_neg_lam) * r
    a = jnp.exp(log_a)
    a_scr[...] = a
    b_scr[...] = jnp.sqrt(1.0 - a * a) * (ig * xc)

    def step(t, h):
        h = a_scr[pl.ds(t, 1), :] * h + b_scr[pl.ds(t, 1), :]
        b_scr[pl.ds(t, 1), :] = h
        return h

    h_ref[...] = lax.fori_loop(0, ts, step, h_ref[...], unroll=8)
    xg = xg_ref[...].astype(F32)
    gelu = 0.5 * xg * (1.0 + jnp.tanh(0.7978845608028654 * (xg + 0.044715 * (xg * xg * xg))))
    o_ref[...] = (b_scr[...] * gelu).astype(BF16)


def _rglru(proj, cw, cb, wa, ba, wx, bx, lam, layer, *, xr_blk, xg_blk):
    b, s, _ = proj.shape
    dr = cw.shape[2]
    return pl.pallas_call(
        functools.partial(_rglru_kernel, ts=RNN_TS),
        out_shape=jax.ShapeDtypeStruct((b, s, dr), BF16),
        grid=(b, s // RNN_TS),
        in_specs=[
            pl.BlockSpec((None, RNN_TS, dr), lambda bi, si: (bi, si, xr_blk)),
            pl.BlockSpec((None, RNN_TS, dr), lambda bi, si: (bi, si, xg_blk)),
            _layer_spec(cw, layer), _layer_spec(cb, layer),
            _layer_spec(wa, layer), _layer_spec(ba, layer),
            _layer_spec(wx, layer), _layer_spec(bx, layer),
            _layer_spec(lam, layer),
        ],
        out_specs=pl.BlockSpec((None, RNN_TS, dr), lambda bi, si: (bi, si, 0)),
        scratch_shapes=[
            pltpu.VMEM((SUBLANES, dr), F32),
            pltpu.VMEM((1, dr), F32),
            pltpu.VMEM((RNN_TS, dr), F32),
            pltpu.VMEM((RNN_TS, dr), F32),
        ],
        compiler_params=_params("arbitrary", "arbitrary"),
        name="rglru",
    )(proj, proj, cw, cb, wa, ba, wx, bx, lam)


def _first_argmax(vals):
    m = vals[0]
    for v in vals[1:]:
        m = jnp.maximum(m, v)
    idx = jnp.full_like(m, float(len(vals) - 1))
    for k in range(len(vals) - 2, -1, -1):
        idx = jnp.where(vals[k] == m, float(k), idx)
    return m, idx


def _merge_kernel(x_ref, at_ref, rn_ref, ga_ref, gr_ref, wao_ref, wro_ref, wo_ref, g_ref,
                  wrh_ref, wrl_ref, br_ref, x1_ref, hr_ref, cls_ref):
    d = x_ref.shape[1]
    ya = _dot(at_ref[...], wao_ref[...])
    yr = _dot(rn_ref[...], wro_ref[...])
    merged = _sigmoid(ga_ref[...].astype(F32)) * ya + _sigmoid(gr_ref[...].astype(F32)) * yr
    x1 = x_ref[...] + _dot(merged.astype(BF16), wo_ref[...])
    x1_ref[...] = x1
    h = _rms(x1, g_ref[...])
    hr_ref[:, :d] = h

    h_hi = h.astype(BF16)
    h_lo = (h - h_hi.astype(F32)).astype(BF16)
    wrh = wrh_ref[...]
    lt = (lax.dot_general(wrh, h_hi, _NT, preferred_element_type=F32)
          + lax.dot_general(wrh, h_lo, _NT, preferred_element_type=F32)
          + lax.dot_general(wrl_ref[...], h_hi, _NT, preferred_element_type=F32))
    lt = lt + br_ref[...]
    g = [lt[k:k + 1, :] for k in range(N_GROUPS)]
    e = [lt[N_GROUPS + k:N_GROUPS + k + 1, :] for k in range(N_EXPERTS)]

    gmax, gi = _first_argmax(g)
    den = jnp.exp(g[0] - gmax)
    for k in range(1, N_GROUPS):
        den = den + jnp.exp(g[k] - gmax)
    gate = 1.0 / den
    sel = []
    for j in range(EXPERTS_PER_GROUP):
        s = e[(N_GROUPS - 1) * EXPERTS_PER_GROUP + j]
        for k in range(N_GROUPS - 2, -1, -1):
            s = jnp.where(gi == float(k), e[k * EXPERTS_PER_GROUP + j], s)
        sel.append(s)
    m1, i1 = _first_argmax(sel)
    sel2 = [jnp.where(i1 == float(j), -jnp.inf, sel[j]) for j in range(EXPERTS_PER_GROUP)]
    m2, i2 = _first_argmax(sel2)
    t = jnp.exp(m2 - m1)
    w1 = gate * (1.0 / (1.0 + t))
    w2 = gate * (t / (1.0 + t))
    first_is_lo = i1 < i2
    a = jnp.minimum(i1, i2)
    b = jnp.maximum(i1, i2)
    w_lo = jnp.where(first_is_lo, w1, w2)
    w_hi = jnp.where(first_is_lo, w2, w1)
    pair = jnp.where(a == 0.0, b - 1.0, jnp.where(a == 1.0, b + 1.0, 5.0))
    cls = gi * float(PAIRS_PER_GROUP) + pair

    n = cls.shape[1]
    cls_ref[...] = jnp.concatenate([cls, jnp.zeros((SUBLANES - 1, n), F32)], axis=0)
    rows = jnp.concatenate([w_lo, w_hi, jnp.zeros((ROUTE_LANES - 2, n), F32)], axis=0)
    hr_ref[:, d:] = rows.T


def _merge(x, attn, rnn, proj, wao, wro, wo, g, wrh, wrl, br, layer, *, ga_blk, gr_blk):
    t, d = x.shape
    da = attn.shape[1]
    dr = rnn.shape[1]
    tm = ROW_TILE
    return pl.pallas_call(
        _merge_kernel,
        out_shape=(
            jax.ShapeDtypeStruct((t, d), F32),
            jax.ShapeDtypeStruct((t, d + ROUTE_LANES), F32),
            jax.ShapeDtypeStruct((SUBLANES, t), F32),
        ),
        grid=(t // tm,),
        in_specs=[
            pl.BlockSpec((tm, d), lambda i: (i, 0)),
            pl.BlockSpec((tm, da), lambda i: (i, 0)),
            pl.BlockSpec((tm, dr), lambda i: (i, 0)),
            pl.BlockSpec((tm, d), lambda i: (i, ga_blk)),
            pl.BlockSpec((tm, d), lambda i: (i, gr_blk)),
            _layer_spec(wao, layer), _layer_spec(wro, layer), _layer_spec(wo, layer),
            _layer_spec(g, layer), _layer_spec(wrh, layer), _layer_spec(wrl, layer),
            _layer_spec(br, layer),
        ],
        out_specs=(
            pl.BlockSpec((tm, d), lambda i: (i, 0)),
            pl.BlockSpec((tm, d + ROUTE_LANES), lambda i: (i, 0)),
            pl.BlockSpec((SUBLANES, tm), lambda i: (0, i)),
        ),
        compiler_params=_params("arbitrary"),
        name="merge_router",
    )(x, attn, rnn, proj, proj, wao, wro, wo, g, wrh, wrl, br)


def _rank_kernel(cls_ref, lt_ref, pos_ref, tinfo_ref, cnt_ref, off_ref, run_ref, *, tm):
    ph = pl.program_id(0)
    bi = pl.program_id(1)
    cls = cls_ref[0:1, :]
    crow = lax.broadcasted_iota(jnp.int32, (CLASS_ROWS, tm), 0).astype(F32)
    member = crow == cls
    onehot = jnp.where(member, 1.0, 0.0)
    block_count = jnp.sum(onehot, axis=1, keepdims=True)

    @pl.when(jnp.logical_and(ph == 0, bi == 0))
    def _():
        cnt_ref[...] = jnp.zeros_like(cnt_ref)

    @pl.when(ph == 0)
    def _():
        cnt_ref[...] += block_count

    @pl.when(jnp.logical_and(ph == 1, bi == 0))
    def _():
        ntile = jnp.floor((cnt_ref[...] + float(MOE_TILE - 1)) * (1.0 / MOE_TILE))
        rid = lax.broadcasted_iota(jnp.int32, (CLASS_ROWS, LANES), 0)
        toff = jnp.zeros((CLASS_ROWS, LANES), F32)
        for c in range(1, CLASS_ROWS):
            toff = toff + jnp.where(rid >= c, ntile[c - 1:c, :], 0.0)
        off_ref[...] = toff * float(MOE_TILE)
        run_ref[...] = jnp.zeros_like(run_ref)
        ti = lax.broadcasted_iota(jnp.int32, (CLASS_ROWS, LANES), 1).astype(F32)
        inside = jnp.where(ti >= toff, jnp.where(ti < toff + ntile, 1.0, 0.0), 0.0)
        c = rid.astype(F32)
        grp = (jnp.where(c >= 6.0, 1.0, 0.0) + jnp.where(c >= 12.0, 1.0, 0.0)
               + jnp.where(c >= 18.0, 1.0, 0.0))
        pair = c - float(PAIRS_PER_GROUP) * grp
        a = jnp.where(pair >= 3.0, 1.0, 0.0) + jnp.where(pair >= 5.0, 1.0, 0.0)
        b = jnp.where(pair < 3.0, pair + 1.0, jnp.where(pair < 5.0, pair - 1.0, 3.0))
        e_lo = jnp.sum(inside * (float(EXPERTS_PER_GROUP) * grp + a), axis=0, keepdims=True)
        e_hi = jnp.sum(inside * (float(EXPERTS_PER_GROUP) * grp + b), axis=0, keepdims=True)
        valid = jnp.sum(inside, axis=0, keepdims=True)
        info = jnp.concatenate([e_lo, e_hi, valid, jnp.zeros((SUBLANES - 3, LANES), F32)], axis=0)
        tinfo_ref[...] = info.astype(jnp.int32)

    @pl.when(ph == 1)
    def _():
        before = _dot(onehot.astype(BF16), lt_ref[...])
        val = before + run_ref[:, 0:1] + off_ref[:, 0:1]
        pos = jnp.sum(jnp.where(member, val, 0.0), axis=0, keepdims=True)
        pos_ref[...] = pos.astype(jnp.int32)
        run_ref[...] += block_count


def _rank(cls, strict_lt):
    t = cls.shape[1]
    tm = ROW_TILE
    return pl.pallas_call(
        functools.partial(_rank_kernel, tm=tm),
        out_shape=(
            jax.ShapeDtypeStruct((1, t), jnp.int32),
            jax.ShapeDtypeStruct((SUBLANES, LANES), jnp.int32),
        ),
        grid=(2, t // tm),
        in_specs=[
            pl.BlockSpec((SUBLANES, tm), lambda ph, bi: (0, bi)),
            pl.BlockSpec((tm, tm), lambda ph, bi: (0, 0)),
        ],
        out_specs=(
            pl.BlockSpec((1, tm), lambda ph, bi: (0, bi * ph)),
            pl.BlockSpec((SUBLANES, LANES), lambda ph, bi: (0, 0)),
        ),
        scratch_shapes=[pltpu.VMEM((CLASS_ROWS, LANES), F32)] * 3,
        compiler_params=_params("arbitrary", "arbitrary"),
        name="rank_tokens",
    )(cls, strict_lt)


def _scatter_kernel(pos_ref, h_ref, init_ref, o_ref, sem, *, tm):
    del init_ref

    def row_copy(r, p):
        return pltpu.make_async_copy(h_ref.at[pl.ds(r, 1)], o_ref.at[pl.ds(p, 1)], sem)

    def start(grp, c):
        for j in range(DMA_UNROLL):
            r = grp * DMA_UNROLL + j
            row_copy(r, pos_ref[0, 0, r]).start(priority=j % 2)
        return c

    def wait(r, c):
        row_copy(0, 0).wait()
        return c

    lax.fori_loop(0, tm // DMA_UNROLL, start, 0)
    lax.fori_loop(0, tm, wait, 0, unroll=DMA_UNROLL)


def _scatter_rows(pos, rows, init):
    t, w = rows.shape
    tm = SCATTER_TILE
    return pl.pallas_call(
        functools.partial(_scatter_kernel, tm=tm),
        out_shape=jax.ShapeDtypeStruct(init.shape, init.dtype),
        grid=(t // tm,),
        in_specs=[
            pl.BlockSpec((1, 1, tm), lambda i: (i, 0, 0), memory_space=pltpu.SMEM),
            pl.BlockSpec((tm, w), lambda i: (i, 0)),
            pl.BlockSpec(memory_space=pl.ANY),
        ],
        out_specs=pl.BlockSpec(memory_space=pl.ANY),
        scratch_shapes=[pltpu.SemaphoreType.DMA(())],
        input_output_aliases={2: 0},
        compiler_params=_params("arbitrary"),
        name="scatter_rows",
    )(pos.reshape(t // tm, 1, tm), rows, init)


def _ffn_kernel(info_ref, x_ref, wgl_ref, wul_ref, wdl_ref, wgh_ref, wuh_ref, wdh_ref, o_ref,
                wg_s, wu_s, wd_s):
    i = pl.program_id(0)
    d = o_ref.shape[1]
    prev = jnp.maximum(i - 1, 0)
    for k, (wg, wu, wd) in enumerate(((wgl_ref, wul_ref, wdl_ref), (wgh_ref, wuh_ref, wdh_ref))):
        @pl.when(jnp.logical_or(i == 0, info_ref[k, i] != info_ref[k, prev]))
        def _():
            wg_s[k] = wg[...].astype(BF16)
            wu_s[k] = wu[...].astype(BF16)
            wd_s[k] = wd[...].astype(BF16)

    @pl.when(info_ref[2, i] == 0)
    def _():
        o_ref[...] = jnp.zeros_like(o_ref)

    @pl.when(info_ref[2, i] != 0)
    def _():
        xt = x_ref[...]
        xb = xt[:, :d].astype(BF16)
        y = None
        for k in range(2):
            gte = _dot(xb, wg_s[k])
            up = _dot(xb, wu_s[k])
            he = (gte * _sigmoid(gte)) * up
            part = xt[:, d + k:d + k + 1] * _dot(he.astype(BF16), wd_s[k])
            y = part if y is None else y + part
        o_ref[...] = y


def _expert_ffn(tinfo, xs, wg, wu, wd, layer):
    npad, w = xs.shape
    _, _, d, de = wg.shape
    tm = MOE_TILE
    w_in = lambda row: pl.BlockSpec((None, None, d, de), lambda i, info: (layer, info[row, i], 0, 0))
    w_out = lambda row: pl.BlockSpec((None, None, de, d), lambda i, info: (layer, info[row, i], 0, 0))
    return pl.pallas_call(
        _ffn_kernel,
        out_shape=jax.ShapeDtypeStruct((npad, d), F32),
        grid_spec=pltpu.PrefetchScalarGridSpec(
            num_scalar_prefetch=1,
            grid=(npad // tm,),
            in_specs=[
                pl.BlockSpec((tm, w), lambda i, info: (i, 0)),
                w_in(0), w_in(0), w_out(0), w_in(1), w_in(1), w_out(1),
            ],
            out_specs=pl.BlockSpec((tm, d), lambda i, info: (i, 0)),
            scratch_shapes=[
                pltpu.VMEM((2, d, de), BF16),
                pltpu.VMEM((2, d, de), BF16),
                pltpu.VMEM((2, de, d), BF16),
            ],
        ),
        compiler_params=_params("arbitrary"),
        name="expert_ffn",
    )(tinfo, xs, wg, wu, wd, wg, wu, wd)


def _ple_kernel(pos_ref, posn_ref, x_ref, p_ref, ys_ref, g_ref, wg_ref, wp_ref, gn_ref, *rest,
                tm, nsteps, col_map):
    if col_map is None:
        o_ref, ybuf, sem = rest
    else:
        win_ref, o_ref, proj_ref, ybuf, sem = rest
    i = pl.program_id(0)
    slot = lax.rem(i, 2)

    def row_copy(r, p, s):
        return pltpu.make_async_copy(ys_ref.at[pl.ds(p, 1)], ybuf.at[s, pl.ds(r, 1)], sem.at[s])

    def gather(idx_ref, s):
        def start(r, c):
            row_copy(r, idx_ref[0, 0, r], s).start()
            return c
        lax.fori_loop(0, tm, start, 0, unroll=DMA_UNROLL)

    @pl.when(i == 0)
    def _():
        gather(pos_ref, 0)

    @pl.when(i + 1 < nsteps)
    def _():
        gather(posn_ref, 1 - slot)

    def wait(r, c):
        row_copy(0, 0, slot).wait()
        return c

    lax.fori_loop(0, tm, wait, 0, unroll=DMA_UNROLL)
    x2 = x_ref[...] + ybuf[slot]
    gate = _sigmoid(_dot(_rms(x2, g_ref[...]).astype(BF16), wg_ref[...]))
    x3 = x2 + gate * _dot(p_ref[...].astype(BF16), wp_ref[...])
    if col_map is None:
        o_ref[...] = _rms(x3, gn_ref[...])
    else:
        o_ref[...] = x3
        _in_projection(x3, gn_ref[...], win_ref, proj_ref, col_map)


def _ple(pos, x, p, ys, g, wg, wp, layer, *, g_next, next_layer, w_in=None, col_map=None):
    t, d = x.shape
    dp = p.shape[2]
    tm = PLE_TILE
    nsteps = t // tm
    pos3 = pos.reshape(nsteps, 1, tm)
    in_specs = [
        pl.BlockSpec((1, 1, tm), lambda i: (i, 0, 0), memory_space=pltpu.SMEM),
        pl.BlockSpec((1, 1, tm), lambda i: (jnp.minimum(i + 1, nsteps - 1), 0, 0),
                     memory_space=pltpu.SMEM),
        pl.BlockSpec((tm, d), lambda i: (i, 0)),
        pl.BlockSpec((None, tm, dp), lambda i: (layer, i, 0)),
        pl.BlockSpec(memory_space=pl.ANY),
        _layer_spec(g, layer), _layer_spec(wg, layer), _layer_spec(wp, layer),
        _layer_spec(g_next, next_layer),
    ]
    args = [pos3, pos3, x, p, ys, g, wg, wp, g_next]
    out_shape = jax.ShapeDtypeStruct((t, d), F32)
    out_specs = pl.BlockSpec((tm, d), lambda i: (i, 0))
    if col_map is not None:
        n = w_in.shape[2]
        in_specs.append(_layer_spec(w_in, next_layer))
        args.append(w_in)
        out_shape = (out_shape, jax.ShapeDtypeStruct((t, n), BF16))
        out_specs = (out_specs, pl.BlockSpec((tm, n), lambda i: (i, 0)))
    return pl.pallas_call(
        functools.partial(_ple_kernel, tm=tm, nsteps=nsteps, col_map=col_map),
        out_shape=out_shape,
        grid=(nsteps,),
        in_specs=in_specs,
        out_specs=out_specs,
        scratch_shapes=[pltpu.VMEM((2, tm, d), F32), pltpu.SemaphoreType.DMA((2,))],
        compiler_params=_params("arbitrary"),
        name="gather_ple",
    )(*args)


def _block_diag(w):
    l, g, n, _ = w.shape
    eye = jnp.eye(g, dtype=w.dtype)
    return (eye[None, :, None, :, None] * w[:, :, :, None, :]).reshape(l, g * n, g * n)


def kernel(x, p, norm_mix, w_in, conv_w, conv_b, w_rg_a, b_rg_a, w_rg_x, b_rg_x, rg_lambda,
           w_attn_o, w_rnn_o, w_out, norm_moe, w_router_group, b_router_group, w_router_expert,
           b_router_expert, w_exp_gate, w_exp_up, w_exp_down, norm_ple, w_ple_gate, w_ple_proj,
           norm_final):
    bsz, seq, d = x.shape
    depth = w_in.shape[0]
    t = bsz * seq
    d_attn = N_HEADS * HEAD_DIM
    d_rnn = conv_w.shape[2]
    assert d_attn == d_rnn and d == 2 * d_attn and d_attn == COL_CHUNK
    assert t % ROW_TILE == 0 and seq % ATT_TQ == 0 and seq % RNN_TS == 0
    assert t % SCATTER_TILE == 0 and t % PLE_TILE == 0

    n_in = w_in.shape[2]
    gates_at = 3 * d_attn + 2 * d_rnn
    col_map = tuple((src, (src - gates_at) % n_in) for src in range(0, n_in, COL_CHUNK))
    ga_blk, gr_blk = 0, 1
    q_col = 2 * d
    q_blk, k_blk, v_blk = (q_col // LANES, (q_col + d_attn) // LANES, (q_col + 2 * d_attn) // LANES)
    xr_blk, xg_blk = (q_col + 3 * d_attn) // d_rnn, (q_col + 3 * d_attn + d_rnn) // d_rnn

    tri_incl = jnp.tril(jnp.ones((ATT_TK, ATT_TK), F32)).astype(BF16)
    strict_lt = jnp.triu(jnp.ones((ROW_TILE, ROW_TILE), F32), k=1).astype(BF16)

    n_tiles = (t + N_CLASSES * (MOE_TILE - 1)) // MOE_TILE + 1
    assert n_tiles <= LANES
    sorted_init = jnp.zeros((n_tiles * MOE_TILE, d + ROUTE_LANES), F32)

    vec = lambda v: v.reshape(depth, 1, -1)
    w_in_b = w_in.astype(BF16)
    wa_bd = _block_diag(w_rg_a).astype(BF16)
    wx_bd = _block_diag(w_rg_x).astype(BF16)
    w_r = jnp.concatenate([w_router_group, w_router_expert], axis=2).transpose(0, 2, 1)
    w_r = jnp.pad(w_r, ((0, 0), (0, CLASS_ROWS - w_r.shape[1]), (0, 0)))
    w_r_hi = w_r.astype(BF16)
    w_r_lo = (w_r - w_r_hi.astype(F32)).astype(BF16)
    b_r = jnp.concatenate([b_router_group, b_router_expert], axis=1)
    b_r = jnp.pad(b_r, ((0, 0), (0, CLASS_ROWS - b_r.shape[1]))).reshape(depth, CLASS_ROWS, 1)
    wao_b, wro_b, wo_b = w_attn_o.astype(BF16), w_rnn_o.astype(BF16), w_out.astype(BF16)
    wpg_b, wpp_b = w_ple_gate.astype(BF16), w_ple_proj.astype(BF16)
    g_mix, g_moe, g_ple = vec(norm_mix), vec(norm_moe), vec(norm_ple)
    g_fin = norm_final.reshape(1, 1, -1)
    p3 = p.reshape(depth, t, -1)

    x2d = x.reshape(t, d)
    proj = _inproj(x2d, g_mix, w_in_b, 0, col_map)
    for i in range(depth):
        proj3 = proj.reshape(bsz, seq, -1)
        attn = _attention(proj3, tri_incl, q_blk=q_blk, k_blk=k_blk, v_blk=v_blk)
        rnn = _rglru(proj3, conv_w, vec(conv_b), wa_bd, vec(b_rg_a), wx_bd, vec(b_rg_x),
                     vec(rg_lambda), i, xr_blk=xr_blk, xg_blk=xg_blk)
        x1, routed, cls = _merge(
            x2d, attn.reshape(t, d_attn), rnn.reshape(t, d_rnn), proj, wao_b, wro_b, wo_b,
            g_moe, w_r_hi, w_r_lo, b_r, i, ga_blk=ga_blk, gr_blk=gr_blk)
        pos, tinfo = _rank(cls, strict_lt)
        pos = pos.reshape(t)
        xs = _scatter_rows(pos, routed, sorted_init)
        ys = _expert_ffn(tinfo, xs, w_exp_gate, w_exp_up, w_exp_down, i)
        if i + 1 < depth:
            x2d, proj = _ple(pos, x1, p3, ys, g_ple, wpg_b, wpp_b, i, g_next=g_mix,
                             next_layer=i + 1, w_in=w_in_b, col_map=col_map)
        else:
            x2d = _ple(pos, x1, p3, ys, g_ple, wpg_b, wpp_b, i, g_next=g_fin, next_layer=0)
    return x2d.reshape(bsz, seq, d)
```

```python
import functools

import jax
import jax.numpy as jnp
from jax import lax
from jax.experimental import pallas as pl
from jax.experimental.pallas import tpu as pltpu

F32 = jnp.float32
BF16 = jnp.bfloat16
EPS = 1e-6

N_HEADS = 8
HEAD_DIM = 64
N_GROUPS = 4
EXPERTS_PER_GROUP = 4
N_EXPERTS = N_GROUPS * EXPERTS_PER_GROUP
PAIRS_PER_GROUP = 6
N_CLASSES = N_GROUPS * PAIRS_PER_GROUP
CLASS_ROWS = 32
CONV_WIDTH = 4
RG_C = 8.0

LANES = 128
SUBLANES = 8
VMEM_LIMIT = 56 * 1024 * 1024

ROW_TILE = 512
ATT_TQ = 512
ATT_TK = 256
RNN_TS = 256
MOE_TILE = 512
SCATTER_TILE = 1024
PLE_TILE = 512
ROUTE_LANES = 128
DMA_UNROLL = 8
COL_CHUNK = 512

_NT = (((1,), (1,)), ((), ()))
LOG2E = 1.4426950408889634
UNDERFLOW_EXPONENT = 151.0 / LOG2E


def _params(*sem):
    return pltpu.CompilerParams(dimension_semantics=sem, vmem_limit_bytes=VMEM_LIMIT)


def _layer_spec(arr, layer):
    nd = arr.ndim
    return pl.BlockSpec((None,) + arr.shape[1:], lambda *_: (layer,) + (0,) * (nd - 1))


def _rms(x, g):
    ms = jnp.mean(x * x, axis=-1, keepdims=True)
    return x * lax.rsqrt(ms + EPS) * g


def _sigmoid(x):
    return 1.0 / (1.0 + jnp.exp(-x))


def _dot(a, b):
    return jnp.dot(a, b, preferred_element_type=F32)


def _in_projection(x, g, w_ref, o_ref, col_map):
    h = _rms(x, g).astype(BF16)
    for src, dst in col_map:
        o_ref[:, dst:dst + COL_CHUNK] = _dot(h, w_ref[:, src:src + COL_CHUNK]).astype(BF16)


def _inproj_kernel(x_ref, g_ref, w_ref, o_ref, *, col_map):
    _in_projection(x_ref[...], g_ref[...], w_ref, o_ref, col_map)


def _inproj(x, g, w, layer, col_map):
    t, d = x.shape
    n = w.shape[2]
    return pl.pallas_call(
        functools.partial(_inproj_kernel, col_map=col_map),
        out_shape=jax.ShapeDtypeStruct((t, n), BF16),
        grid=(t // ROW_TILE,),
        in_specs=[
            pl.BlockSpec((ROW_TILE, d), lambda i: (i, 0)),
            _layer_spec(g, layer),
            _layer_spec(w, layer),
        ],
        out_specs=pl.BlockSpec((ROW_TILE, n), lambda i: (i, 0)),
        compiler_params=_params("arbitrary"),
        name="inproj",
    )(x, g, w)


def _attn_tile(q_heads, kb, vb, u, acc_ref, carry_ref, rows, causal):
    tk = kb.shape[0]
    lane = lax.broadcasted_iota(jnp.int32, (1, LANES), 1)
    first_head = lane < HEAD_DIM
    vz = jnp.zeros_like(vb)
    v_heads = (jnp.where(first_head, vb, vz), jnp.where(first_head, vz, vb))
    pv = None
    for h in range(2):
        z = lax.dot_general(q_heads[h], kb, _NT, preferred_element_type=F32)
        sp = jnp.maximum(z, 0.0) + jnp.log(1.0 + jnp.exp2(jnp.abs(z) * (-LOG2E)))
        if causal is not None:
            sp = jnp.where(causal, sp, 0.0)
        suffix = _dot(sp.astype(BF16), u)
        carry = carry_ref[h, rows, :]
        w = jnp.exp2((z - suffix - jnp.tile(carry, (1, tk // LANES))) * LOG2E)
        if causal is not None:
            w = jnp.where(causal, w, 0.0)
        d = _dot(w.astype(BF16), v_heads[h])
        pv = d if pv is None else pv + d
        carry_ref[h, rows, :] = carry + suffix[:, 0:1]
    acc_ref[rows, :] += pv


def _attn_kernel(q_ref, k_ref, v_ref, u_ref, o_ref, acc_ref, carry_ref, *, tq, tk):
    qi = pl.program_id(2)
    lane = lax.broadcasted_iota(jnp.int32, (1, LANES), 1)
    first_head = lane < HEAD_DIM
    q = q_ref[...] * jnp.asarray(HEAD_DIM ** -0.5, BF16)
    qz = jnp.zeros_like(q)
    q_heads = (jnp.where(first_head, q, qz), jnp.where(first_head, qz, q))
    acc_ref[...] = jnp.zeros_like(acc_ref)
    carry_ref[...] = jnp.zeros_like(carry_ref)
    u = u_ref[...]
    q0 = qi * tq

    for d in range(tq // tk - 1, -1, -1):
        r0 = d * tk
        rows = slice(r0, tq)
        ks = pl.multiple_of(q0 + r0, tk)
        row = lax.broadcasted_iota(jnp.int32, (tq - r0, tk), 0)
        col = lax.broadcasted_iota(jnp.int32, (tq - r0, tk), 1)
        _attn_tile([qh[rows] for qh in q_heads], k_ref[pl.ds(ks, tk), :], v_ref[pl.ds(ks, tk), :],
                   u, acc_ref, carry_ref, rows, col < row)

    nfull = qi * (tq // tk)

    def more(state):
        step, min_carry = state
        return jnp.logical_and(step < nfull, min_carry < UNDERFLOW_EXPONENT)

    def body(state):
        step, _ = state
        ks = pl.multiple_of((nfull - 1 - step) * tk, tk)
        _attn_tile(q_heads, k_ref[pl.ds(ks, tk), :], v_ref[pl.ds(ks, tk), :],
                   u, acc_ref, carry_ref, slice(0, tq), None)
        return step + 1, jnp.min(carry_ref[...])

    lax.while_loop(more, body, (jnp.int32(0), jnp.min(carry_ref[...])))
    o_ref[...] = acc_ref[...].astype(BF16)


def _attention(proj, u, *, q_blk, k_blk, v_blk):
    b, s, _ = proj.shape
    hp = N_HEADS * HEAD_DIM // LANES
    return pl.pallas_call(
        functools.partial(_attn_kernel, tq=ATT_TQ, tk=ATT_TK),
        out_shape=jax.ShapeDtypeStruct((b, s, N_HEADS * HEAD_DIM), BF16),
        grid=(b, hp, s // ATT_TQ),
        in_specs=[
            pl.BlockSpec((None, ATT_TQ, LANES), lambda bi, hi, qi: (bi, qi, q_blk + hi)),
            pl.BlockSpec((None, s, LANES), lambda bi, hi, qi: (bi, 0, k_blk + hi)),
            pl.BlockSpec((None, s, LANES), lambda bi, hi, qi: (bi, 0, v_blk + hi)),
            pl.BlockSpec((ATT_TK, ATT_TK), lambda bi, hi, qi: (0, 0)),
        ],
        out_specs=pl.BlockSpec((None, ATT_TQ, LANES), lambda bi, hi, qi: (bi, qi, hi)),
        scratch_shapes=[
            pltpu.VMEM((ATT_TQ, LANES), F32),
            pltpu.VMEM((2, ATT_TQ, LANES), F32),
        ],
        compiler_params=_params("arbitrary", "arbitrary", "arbitrary"),
        name="sb_attention",
    )(proj, proj, proj, u)


def _rglru_kernel(xr_ref, xg_ref, cw_ref, cb_ref, wa_ref, ba_ref, wx_ref, bx_ref, lam_ref,
                  o_ref, tail_ref, h_ref, a_scr, b_scr, *, ts):
    @pl.when(pl.program_id(1) == 0)
    def _():
        tail_ref[...] = jnp.zeros_like(tail_ref)
        h_ref[...] = jnp.zeros_like(h_ref)

    xr = xr_ref[...].astype(F32)
    ext = jnp.concatenate([tail_ref[...], xr], axis=0)
    tail_ref[...] = xr[ts - SUBLANES:, :]
    xc = cb_ref[...]
    for j in range(CONV_WIDTH):
        lo = SUBLANES - (CONV_WIDTH - 1) + j
        xc = xc + cw_ref[j:j + 1, :] * ext[lo:lo + ts, :]
    xcb = xc.astype(BF16)
    r = 0.5 + 0.5 * jnp.tanh(0.5 * (_dot(xcb, wa_ref[...]) + ba_ref[...]))
    ig = 0.5 + 0.5 * jnp.tanh(0.5 * (_dot(xcb, wx_ref[...]) + bx_ref[...]))
    lam = lam_ref[...]
    softplus_neg_lam = jnp.maximum(-lam, 0.0) + jnp.log(1.0 + jnp.exp(-jnp.abs(lam)))
    log_a = (-RG_C * softplus_neg_lam) * r
    a = jnp.exp(log_a)
    a_scr[...] = a
    b_scr[...] = jnp.sqrt(1.0 - a * a) * (ig * xc)

    def step(t, h):
        h = a_scr[pl.ds(t, 1), :] * h + b_scr[pl.ds(t, 1), :]
        b_scr[pl.ds(t, 1), :] = h
        return h

    h_ref[...] = lax.fori_loop(0, ts, step, h_ref[...], unroll=8)
    xg = xg_ref[...].astype(F32)
    gelu = 0.5 * xg * (1.0 + jnp.tanh(0.7978845608028654 * (xg + 0.044715 * (xg * xg * xg))))
    o_ref[...] = (b_scr[...] * gelu).astype(BF16)


def _rglru(proj, cw, cb, wa, ba, wx, bx, lam, layer, *, xr_blk, xg_blk):
    b, s, _ = proj.shape
    dr = cw.shape[2]
    return pl.pallas_call(
        functools.partial(_rglru_kernel, ts=RNN_TS),
        out_shape=jax.ShapeDtypeStruct((b, s, dr), BF16),
        grid=(b, s // RNN_TS),
        in_specs=[
            pl.BlockSpec((None, RNN_TS, dr), lambda bi, si: (bi, si, xr_blk)),
            pl.BlockSpec((None, RNN_TS, dr), lambda bi, si: (bi, si, xg_blk)),
            _layer_spec(cw, layer), _layer_spec(cb, layer),
            _layer_spec(wa, layer), _layer_spec(ba, layer),
            _layer_spec(wx, layer), _layer_spec(bx, layer),
            _layer_spec(lam, layer),
        ],
        out_specs=pl.BlockSpec((None, RNN_TS, dr), lambda bi, si: (bi, si, 0)),
        scratch_shapes=[
            pltpu.VMEM((SUBLANES, dr), F32),
            pltpu.VMEM((1, dr), F32),
            pltpu.VMEM((RNN_TS, dr), F32),
            pltpu.VMEM((RNN_TS, dr), F32),
        ],
        compiler_params=_params("arbitrary", "arbitrary"),
        name="rglru",
    )(proj, proj, cw, cb, wa, ba, wx, bx, lam)


def _first_argmax(vals):
    m = vals[0]
    for v in vals[1:]:
        m = jnp.maximum(m, v)
    idx = jnp.full_like(m, float(len(vals) - 1))
    for k in range(len(vals) - 2, -1, -1):
        idx = jnp.where(vals[k] == m, float(k), idx)
    return m, idx


def _merge_kernel(x_ref, at_ref, rn_ref, ga_ref, gr_ref, wao_ref, wro_ref, wo_ref, g_ref,
                  wrh_ref, wrl_ref, br_ref, x1_ref, hr_ref, cls_ref):
    d = x_ref.shape[1]
    ya = _dot(at_ref[...], wao_ref[...])
    yr = _dot(rn_ref[...], wro_ref[...])
    merged = _sigmoid(ga_ref[...].astype(F32)) * ya + _sigmoid(gr_ref[...].astype(F32)) * yr
    x1 = x_ref[...] + _dot(merged.astype(BF16), wo_ref[...])
    x1_ref[...] = x1
    h = _rms(x1, g_ref[...])
    hr_ref[:, :d] = h

    h_hi = h.astype(BF16)
    h_lo = (h - h_hi.astype(F32)).astype(BF16)
    wrh = wrh_ref[...]
    lt = (lax.dot_general(wrh, h_hi, _NT, preferred_element_type=F32)
          + lax.dot_general(wrh, h_lo, _NT, preferred_element_type=F32)
          + lax.dot_general(wrl_ref[...], h_hi, _NT, preferred_element_type=F32))
    lt = lt + br_ref[...]
    g = [lt[k:k + 1, :] for k in range(N_GROUPS)]
    e = [lt[N_GROUPS + k:N_GROUPS + k + 1, :] for k in range(N_EXPERTS)]

    gmax, gi = _first_argmax(g)
    den = jnp.exp(g[0] - gmax)
    for k in range(1, N_GROUPS):
        den = den + jnp.exp(g[k] - gmax)
    gate = 1.0 / den
    sel = []
    for j in range(EXPERTS_PER_GROUP):
        s = e[(N_GROUPS - 1) * EXPERTS_PER_GROUP + j]
        for k in range(N_GROUPS - 2, -1, -1):
            s = jnp.where(gi == float(k), e[k * EXPERTS_PER_GROUP + j], s)
        sel.append(s)
    m1, i1 = _first_argmax(sel)
    sel2 = [jnp.where(i1 == float(j), -jnp.inf, sel[j]) for j in range(EXPERTS_PER_GROUP)]
    m2, i2 = _first_argmax(sel2)
    t = jnp.exp(m2 - m1)
    w1 = gate * (1.0 / (1.0 + t))
    w2 = gate * (t / (1.0 + t))
    first_is_lo = i1 < i2
    a = jnp.minimum(i1, i2)
    b = jnp.maximum(i1, i2)
    w_lo = jnp.where(first_is_lo, w1, w2)
    w_hi = jnp.where(first_is_lo, w2, w1)
    pair = jnp.where(a == 0.0, b - 1.0, jnp.where(a == 1.0, b + 1.0, 5.0))
    cls = gi * float(PAIRS_PER_GROUP) + pair

    n = cls.shape[1]
    cls_ref[...] = jnp.concatenate([cls, jnp.zeros((SUBLANES - 1, n), F32)], axis=0)
    rows = jnp.concatenate([w_lo, w_hi, jnp.zeros((ROUTE_LANES - 2, n), F32)], axis=0)
    hr_ref[:, d:] = rows.T


def _merge(x, attn, rnn, proj, wao, wro, wo, g, wrh, wrl, br, layer, *, ga_blk, gr_blk):
    t, d = x.shape
    da = attn.shape[1]
    dr = rnn.shape[1]
    tm = ROW_TILE
    return pl.pallas_call(
        _merge_kernel,
        out_shape=(
            jax.ShapeDtypeStruct((t, d), F32),
            jax.ShapeDtypeStruct((t, d + ROUTE_LANES), F32),
            jax.ShapeDtypeStruct((SUBLANES, t), F32),
        ),
        grid=(t // tm,),
        in_specs=[
            pl.BlockSpec((tm, d), lambda i: (i, 0)),
            pl.BlockSpec((tm, da), lambda i: (i, 0)),
            pl.BlockSpec((tm, dr), lambda i: (i, 0)),
            pl.BlockSpec((tm, d), lambda i: (i, ga_blk)),
            pl.BlockSpec((tm, d), lambda i: (i, gr_blk)),
            _layer_spec(wao, layer), _layer_spec(wro, layer), _layer_spec(wo, layer),
            _layer_spec(g, layer), _layer_spec(wrh, layer), _layer_spec(wrl, layer),
            _layer_spec(br, layer),
        ],
        out_specs=(
            pl.BlockSpec((tm, d), lambda i: (i, 0)),
            pl.BlockSpec((tm, d + ROUTE_LANES), lambda i: (i, 0)),
            pl.BlockSpec((SUBLANES, tm), lambda i: (0, i)),
        ),
        compiler_params=_params("arbitrary"),
        name="merge_router",
    )(x, attn, rnn, proj, proj, wao, wro, wo, g, wrh, wrl, br)


def _rank_kernel(cls_ref, lt_ref, pos_ref, tinfo_ref, cnt_ref, off_ref, run_ref, *, tm):
    ph = pl.program_id(0)
    bi = pl.program_id(1)
    cls = cls_ref[0:1, :]
    crow = lax.broadcasted_iota(jnp.int32, (CLASS_ROWS, tm), 0).astype(F32)
    member = crow == cls
    onehot = jnp.where(member, 1.0, 0.0)
    block_count = jnp.sum(onehot, axis=1, keepdims=True)

    @pl.when(jnp.logical_and(ph == 0, bi == 0))
    def _():
        cnt_ref[...] = jnp.zeros_like(cnt_ref)

    @pl.when(ph == 0)
    def _():
        cnt_ref[...] += block_count

    @pl.when(jnp.logical_and(ph == 1, bi == 0))
    def _():
        ntile = jnp.floor((cnt_ref[...] + float(MOE_TILE - 1)) * (1.0 / MOE_TILE))
        rid = lax.broadcasted_iota(jnp.int32, (CLASS_ROWS, LANES), 0)
        toff = jnp.zeros((CLASS_ROWS, LANES), F32)
        for c in range(1, CLASS_ROWS):
            toff = toff + jnp.where(rid >= c, ntile[c - 1:c, :], 0.0)
        off_ref[...] = toff * float(MOE_TILE)
        run_ref[...] = jnp.zeros_like(run_ref)
        ti = lax.broadcasted_iota(jnp.int32, (CLASS_ROWS, LANES), 1).astype(F32)
        inside = jnp.where(ti >= toff, jnp.where(ti < toff + ntile, 1.0, 0.0), 0.0)
        c = rid.astype(F32)
        grp = (jnp.where(c >= 6.0, 1.0, 0.0) + jnp.where(c >= 12.0, 1.0, 0.0)
               + jnp.where(c >= 18.0, 1.0, 0.0))
        pair = c - float(PAIRS_PER_GROUP) * grp
        a = jnp.where(pair >= 3.0, 1.0, 0.0) + jnp.where(pair >= 5.0, 1.0, 0.0)
        b = jnp.where(pair < 3.0, pair + 1.0, jnp.where(pair < 5.0, pair - 1.0, 3.0))
        e_lo = jnp.sum(inside * (float(EXPERTS_PER_GROUP) * grp + a), axis=0, keepdims=True)
        e_hi = jnp.sum(inside * (float(EXPERTS_PER_GROUP) * grp + b), axis=0, keepdims=True)
        left = cnt_ref[...] - (ti - toff) * float(MOE_TILE)
        rows = jnp.sum(inside * jnp.minimum(left, float(MOE_TILE)), axis=0, keepdims=True)
        used = jnp.sum(ntile, axis=0, keepdims=True)
        info = jnp.concatenate([e_lo, e_hi, rows, used, jnp.zeros((SUBLANES - 4, LANES), F32)],
                               axis=0)
        tinfo_ref[...] = info.astype(jnp.int32)

    @pl.when(ph == 1)
    def _():
        before = _dot(onehot.astype(BF16), lt_ref[...])
        val = before + run_ref[:, 0:1] + off_ref[:, 0:1]
        pos = jnp.sum(jnp.where(member, val, 0.0), axis=0, keepdims=True)
        pos_ref[...] = pos.astype(jnp.int32)
        run_ref[...] += block_count


def _rank(cls, strict_lt):
    t = cls.shape[1]
    tm = ROW_TILE
    return pl.pallas_call(
        functools.partial(_rank_kernel, tm=tm),
        out_shape=(
            jax.ShapeDtypeStruct((1, t), jnp.int32),
            jax.ShapeDtypeStruct((SUBLANES, LANES), jnp.int32),
        ),
        grid=(2, t // tm),
        in_specs=[
            pl.BlockSpec((SUBLANES, tm), lambda ph, bi: (0, bi)),
            pl.BlockSpec((tm, tm), lambda ph, bi: (0, 0)),
        ],
        out_specs=(
            pl.BlockSpec((1, tm), lambda ph, bi: (0, bi * ph)),
            pl.BlockSpec((SUBLANES, LANES), lambda ph, bi: (0, 0)),
        ),
        scratch_shapes=[pltpu.VMEM((CLASS_ROWS, LANES), F32)] * 3,
        compiler_params=_params("arbitrary", "arbitrary"),
        name="rank_tokens",
    )(cls, strict_lt)


def _scatter_kernel(info_ref, pos_ref, h_ref, o_ref, zero_ref, sem, zsem, *, tm):
    tile_groups = MOE_TILE // SUBLANES

    @pl.when(pl.program_id(0) == 0)
    def _():
        zero_ref[...] = jnp.zeros_like(zero_ref)

        def tile_clear(ti):
            dst = o_ref.at[pl.ds(ti * tile_groups, tile_groups)]
            return pltpu.make_async_copy(zero_ref, dst, zsem)

        n_tiles = o_ref.shape[0] // tile_groups
        for ti in range(n_tiles):
            @pl.when(info_ref[2, ti] < MOE_TILE)
            def _():
                tile_clear(ti).start()
        for ti in range(n_tiles):
            @pl.when(info_ref[2, ti] < MOE_TILE)
            def _():
                tile_clear(ti).wait()

    def row_copy(grp, j, p):
        dst = o_ref.at[lax.shift_right_logical(p, 3), pl.ds(jnp.bitwise_and(p, SUBLANES - 1), 1)]
        return pltpu.make_async_copy(h_ref.at[grp, pl.ds(j, 1)], dst, sem)

    def start(grp, c):
        for j in range(SUBLANES):
            row_copy(grp, j, pos_ref[0, 0, grp * SUBLANES + j]).start(priority=j % 2)
        return c

    def wait(r, c):
        row_copy(0, 0, 0).wait()
        return c

    lax.fori_loop(0, tm // SUBLANES, start, 0)
    lax.fori_loop(0, tm, wait, 0, unroll=DMA_UNROLL)


def _scatter_rows(tinfo, pos, rows, n_sorted):
    t, w = rows.shape
    tm = SCATTER_TILE
    return pl.pallas_call(
        functools.partial(_scatter_kernel, tm=tm),
        out_shape=jax.ShapeDtypeStruct((n_sorted // SUBLANES, SUBLANES, w), rows.dtype),
        grid_spec=pltpu.PrefetchScalarGridSpec(
            num_scalar_prefetch=1,
            grid=(t // tm,),
            in_specs=[
                pl.BlockSpec((1, 1, tm), lambda i, info: (i, 0, 0), memory_space=pltpu.SMEM),
                pl.BlockSpec((tm // SUBLANES, SUBLANES, w), lambda i, info: (i, 0, 0)),
            ],
            out_specs=pl.BlockSpec(memory_space=pl.ANY),
            scratch_shapes=[
                pltpu.VMEM((MOE_TILE // SUBLANES, SUBLANES, w), rows.dtype),
                pltpu.SemaphoreType.DMA(()),
                pltpu.SemaphoreType.DMA(()),
            ],
        ),
        compiler_params=_params("arbitrary"),
        name="scatter_rows",
    )(tinfo, pos.reshape(t // tm, 1, tm), rows.reshape(t // SUBLANES, SUBLANES, w))


def _ffn_kernel(info_ref, x_ref, wgl_ref, wul_ref, wdl_ref, wgh_ref, wuh_ref, wdh_ref, o_ref,
                wg_s, wu_s, wd_s):
    i = pl.program_id(0)
    tm, d = o_ref.shape
    half = tm // 2
    live = i < info_ref[3, 0]
    prev = jnp.maximum(i - 1, 0)
    for k, (wg, wu, wd) in enumerate(((wgl_ref, wul_ref, wdl_ref), (wgh_ref, wuh_ref, wdh_ref))):
        changed = jnp.logical_or(i == 0, info_ref[k, i] != info_ref[k, prev])

        @pl.when(jnp.logical_and(live, changed))
        def _():
            wg_s[k] = wg[...].astype(BF16)
            wu_s[k] = wu[...].astype(BF16)
            wd_s[k] = wd[...].astype(BF16)

    def ffn(rows):
        xt = x_ref[rows, :]
        xb = xt[:, :d].astype(BF16)
        y = None
        for k in range(2):
            gte = _dot(xb, wg_s[k])
            up = _dot(xb, wu_s[k])
            he = (gte * _sigmoid(gte)) * up
            part = xt[:, d + k:d + k + 1] * _dot(he.astype(BF16), wd_s[k])
            y = part if y is None else y + part
        o_ref[rows, :] = y

    @pl.when(jnp.logical_not(live))
    def _():
        o_ref[...] = jnp.zeros_like(o_ref)

    @pl.when(jnp.logical_and(live, info_ref[2, i] > half))
    def _():
        ffn(slice(0, tm))

    @pl.when(jnp.logical_and(live, info_ref[2, i] <= half))
    def _():
        ffn(slice(0, half))
        o_ref[half:, :] = jnp.zeros((tm - half, d), F32)


def _expert_ffn(tinfo, xs, wg, wu, wd, layer):
    npad, w = xs.shape
    _, _, d, de = wg.shape
    tm = MOE_TILE

    def tile(i, info):
        return jnp.minimum(i, info[3, 0] - 1)

    w_in = lambda row: pl.BlockSpec((None, None, d, de),
                                    lambda i, info: (layer, info[row, tile(i, info)], 0, 0))
    w_out = lambda row: pl.BlockSpec((None, None, de, d),
                                     lambda i, info: (layer, info[row, tile(i, info)], 0, 0))
    return pl.pallas_call(
        _ffn_kernel,
        out_shape=jax.ShapeDtypeStruct((npad, d), F32),
        grid_spec=pltpu.PrefetchScalarGridSpec(
            num_scalar_prefetch=1,
            grid=(npad // tm,),
            in_specs=[
                pl.BlockSpec((tm, w), lambda i, info: (tile(i, info), 0)),
                w_in(0), w_in(0), w_out(0), w_in(1), w_in(1), w_out(1),
            ],
            out_specs=pl.BlockSpec((tm, d), lambda i, info: (i, 0)),
            scratch_shapes=[
                pltpu.VMEM((2, d, de), BF16),
                pltpu.VMEM((2, d, de), BF16),
                pltpu.VMEM((2, de, d), BF16),
            ],
        ),
        compiler_params=_params("arbitrary"),
        name="expert_ffn",
    )(tinfo, xs, wg, wu, wd, wg, wu, wd)


def _ple_kernel(pos_ref, posn_ref, x_ref, p_ref, ys_ref, g_ref, wg_ref, wp_ref, gn_ref, *rest,
                tm, nsteps, col_map):
    if col_map is None:
        o_ref, ybuf, sem = rest
    else:
        win_ref, o_ref, proj_ref, ybuf, sem = rest
    i = pl.program_id(0)
    slot = lax.rem(i, 2)
    d = x_ref.shape[1]

    def row_copy(grp, j, p, s):
        src = ys_ref.at[lax.shift_right_logical(p, 3), pl.ds(jnp.bitwise_and(p, SUBLANES - 1), 1)]
        return pltpu.make_async_copy(src, ybuf.at[s, grp, pl.ds(j, 1)], sem.at[s])

    def wait_slot(s):
        def wait(r, c):
            row_copy(0, 0, 0, s).wait()
            return c
        lax.fori_loop(0, tm, wait, 0, unroll=DMA_UNROLL)

    @pl.when(i == 0)
    def _():
        def start(grp, c):
            for j in range(SUBLANES):
                row_copy(grp, j, pos_ref[0, 0, grp * SUBLANES + j], 0).start()
            return c
        lax.fori_loop(0, tm // SUBLANES, start, 0)

    wait_slot(slot)

    n_stage = 1 if col_map is None else len(col_map)
    per_stage = -(-(tm // SUBLANES) // n_stage)

    def prefetch(stage):
        for grp in range(stage * per_stage, min((stage + 1) * per_stage, tm // SUBLANES)):
            for j in range(SUBLANES):
                row_copy(grp, j, posn_ref[0, 0, grp * SUBLANES + j], 1 - slot).start()

    x2 = x_ref[...] + ybuf[slot].reshape(tm, d)
    gate = _sigmoid(_dot(_rms(x2, g_ref[...]).astype(BF16), wg_ref[...]))
    x3 = x2 + gate * _dot(p_ref[...].astype(BF16), wp_ref[...])
    if col_map is None:
        o_ref[...] = _rms(x3, gn_ref[...])
        prefetch(0)
    else:
        o_ref[...] = x3
        h = _rms(x3, gn_ref[...]).astype(BF16)
        for stage, (src, dst) in enumerate(col_map):
            prefetch(stage)
            proj_ref[:, dst:dst + COL_CHUNK] = _dot(h, win_ref[:, src:src + COL_CHUNK]).astype(BF16)

    @pl.when(i == nsteps - 1)
    def _():
        wait_slot(1 - slot)


def _ple(pos, x, p, ys, g, wg, wp, layer, *, g_next, next_layer, w_in=None, col_map=None):
    t, d = x.shape
    dp = p.shape[2]
    tm = PLE_TILE
    nsteps = t // tm
    pos3 = pos.reshape(nsteps, 1, tm)
    in_specs = [
        pl.BlockSpec((1, 1, tm), lambda i: (i, 0, 0), memory_space=pltpu.SMEM),
        pl.BlockSpec((1, 1, tm), lambda i: (jnp.minimum(i + 1, nsteps - 1), 0, 0),
                     memory_space=pltpu.SMEM),
        pl.BlockSpec((tm, d), lambda i: (i, 0)),
        pl.BlockSpec((None, tm, dp), lambda i: (layer, i, 0)),
        pl.BlockSpec(memory_space=pl.ANY),
        _layer_spec(g, layer), _layer_spec(wg, layer), _layer_spec(wp, layer),
        _layer_spec(g_next, next_layer),
    ]
    args = [pos3, pos3, x, p, ys.reshape(-1, SUBLANES, d), g, wg, wp, g_next]
    out_shape = jax.ShapeDtypeStruct((t, d), F32)
    out_specs = pl.BlockSpec((tm, d), lambda i: (i, 0))
    if col_map is not None:
        n = w_in.shape[2]
        in_specs.append(_layer_spec(w_in, next_layer))
        args.append(w_in)
        out_shape = (out_shape, jax.ShapeDtypeStruct((t, n), BF16))
        out_specs = (out_specs, pl.BlockSpec((tm, n), lambda i: (i, 0)))
    return pl.pallas_call(
        functools.partial(_ple_kernel, tm=tm, nsteps=nsteps, col_map=col_map),
        out_shape=out_shape,
        grid=(nsteps,),
        in_specs=in_specs,
        out_specs=out_specs,
        scratch_shapes=[pltpu.VMEM((2, tm // SUBLANES, SUBLANES, d), F32),
                        pltpu.SemaphoreType.DMA((2,))],
        compiler_params=_params("arbitrary"),
        name="gather_ple",
    )(*args)


def _block_diag(w):
    l, g, n, _ = w.shape
    eye = jnp.eye(g, dtype=w.dtype)
    return (eye[None, :, None, :, None] * w[:, :, :, None, :]).reshape(l, g * n, g * n)


def kernel(x, p, norm_mix, w_in, conv_w, conv_b, w_rg_a, b_rg_a, w_rg_x, b_rg_x, rg_lambda,
           w_attn_o, w_rnn_o, w_out, norm_moe, w_router_group, b_router_group, w_router_expert,
           b_router_expert, w_exp_gate, w_exp_up, w_exp_down, norm_ple, w_ple_gate, w_ple_proj,
           norm_final):
    bsz, seq, d = x.shape
    depth = w_in.shape[0]
    t = bsz * seq
    d_attn = N_HEADS * HEAD_DIM
    d_rnn = conv_w.shape[2]
    assert d_attn == d_rnn and d == 2 * d_attn and d_attn == COL_CHUNK
    assert t % ROW_TILE == 0 and seq % ATT_TQ == 0 and seq % RNN_TS == 0
    assert t % SCATTER_TILE == 0 and t % PLE_TILE == 0

    n_in = w_in.shape[2]
    gates_at = 3 * d_attn + 2 * d_rnn
    col_map = tuple((src, (src - gates_at) % n_in) for src in range(0, n_in, COL_CHUNK))
    ga_blk, gr_blk = 0, 1
    q_col = 2 * d
    q_blk, k_blk, v_blk = (q_col // LANES, (q_col + d_attn) // LANES, (q_col + 2 * d_attn) // LANES)
    xr_blk, xg_blk = (q_col + 3 * d_attn) // d_rnn, (q_col + 3 * d_attn + d_rnn) // d_rnn

    tri_incl = jnp.tril(jnp.ones((ATT_TK, ATT_TK), F32)).astype(BF16)
    strict_lt = jnp.triu(jnp.ones((ROW_TILE, ROW_TILE), F32), k=1).astype(BF16)

    n_tiles = (t + N_CLASSES * (MOE_TILE - 1)) // MOE_TILE
    assert n_tiles <= LANES
    n_sorted = n_tiles * MOE_TILE

    vec = lambda v: v.reshape(depth, 1, -1)
    w_in_b = w_in.astype(BF16)
    wa_bd = _block_diag(w_rg_a).astype(BF16)
    wx_bd = _block_diag(w_rg_x).astype(BF16)
    w_r = jnp.concatenate([w_router_group, w_router_expert], axis=2).transpose(0, 2, 1)
    w_r = jnp.pad(w_r, ((0, 0), (0, CLASS_ROWS - w_r.shape[1]), (0, 0)))
    w_r_hi = w_r.astype(BF16)
    w_r_lo = (w_r - w_r_hi.astype(F32)).astype(BF16)
    b_r = jnp.concatenate([b_router_group, b_router_expert], axis=1)
    b_r = jnp.pad(b_r, ((0, 0), (0, CLASS_ROWS - b_r.shape[1]))).reshape(depth, CLASS_ROWS, 1)
    wao_b, wro_b, wo_b = w_attn_o.astype(BF16), w_rnn_o.astype(BF16), w_out.astype(BF16)
    wpg_b, wpp_b = w_ple_gate.astype(BF16), w_ple_proj.astype(BF16)
    g_mix, g_moe, g_ple = vec(norm_mix), vec(norm_moe), vec(norm_ple)
    g_fin = norm_final.reshape(1, 1, -1)
    p3 = p.reshape(depth, t, -1)

    x2d = x.reshape(t, d)
    proj = _inproj(x2d, g_mix, w_in_b, 0, col_map)
    for i in range(depth):
        proj3 = proj.reshape(bsz, seq, -1)
        attn = _attention(proj3, tri_incl, q_blk=q_blk, k_blk=k_blk, v_blk=v_blk)
        rnn = _rglru(proj3, conv_w, vec(conv_b), wa_bd, vec(b_rg_a), wx_bd, vec(b_rg_x),
                     vec(rg_lambda), i, xr_blk=xr_blk, xg_blk=xg_blk)
        x1, routed, cls = _merge(
            x2d, attn.reshape(t, d_attn), rnn.reshape(t, d_rnn), proj, wao_b, wro_b, wo_b,
            g_moe, w_r_hi, w_r_lo, b_r, i, ga_blk=ga_blk, gr_blk=gr_blk)
        pos, tinfo = _rank(cls, strict_lt)
        pos = pos.reshape(t)
        xs = _scatter_rows(tinfo, pos, routed, n_sorted)
        ys = _expert_ffn(tinfo, xs.reshape(n_sorted, -1), w_exp_gate, w_exp_up, w_exp_down, i)
        if i + 1 < depth:
            x2d, proj = _ple(pos, x1, p3, ys, g_ple, wpg_b, wpp_b, i, g_next=g_mix,
                             next_layer=i + 1, w_in=w_in_b, col_map=col_map)
        else:
            x2d = _ple(pos, x1, p3, ys, g_ple, wpg_b, wpp_b, i, g_next=g_fin, next_layer=0)
    return x2d.reshape(bsz, seq, d)
```

```python
import functools

import jax
import jax.numpy as jnp
from jax import lax
from jax.experimental import pallas as pl
from jax.experimental.pallas import tpu as pltpu

F32 = jnp.float32
BF16 = jnp.bfloat16
EPS = 1e-6

N_HEADS = 8
HEAD_DIM = 64
N_GROUPS = 4
EXPERTS_PER_GROUP = 4
N_EXPERTS = N_GROUPS * EXPERTS_PER_GROUP
PAIRS_PER_GROUP = 6
N_CLASSES = N_GROUPS * PAIRS_PER_GROUP
CLASS_ROWS = 32
CONV_WIDTH = 4
RG_C = 8.0

LANES = 128
SUBLANES = 8
VMEM_LIMIT = 56 * 1024 * 1024

ROW_TILE = 512
ATT_TILE = 256
RNN_TS = 256
MOE_TILE = 512
SCATTER_TILE = 1024
PLE_TILE = 512
ROUTE_LANES = 128
DMA_UNROLL = 8
COL_CHUNK = 512

_NT = (((1,), (1,)), ((), ()))
LOG2E = 1.4426950408889634
UNDERFLOW_EXPONENT = 151.0 / LOG2E


def _params(*sem):
    return pltpu.CompilerParams(dimension_semantics=sem, vmem_limit_bytes=VMEM_LIMIT)


def _layer_spec(arr, layer):
    nd = arr.ndim
    return pl.BlockSpec((None,) + arr.shape[1:], lambda *_: (layer,) + (0,) * (nd - 1))


def _rms(x, g):
    ms = jnp.mean(x * x, axis=-1, keepdims=True)
    return x * lax.rsqrt(ms + EPS) * g


def _sigmoid(x):
    return 1.0 / (1.0 + jnp.exp(-x))


def _dot(a, b):
    return jnp.dot(a, b, preferred_element_type=F32)


def _in_projection(x, g, w_ref, o_ref, col_map):
    h = _rms(x, g).astype(BF16)
    for src, dst in col_map:
        o_ref[:, dst:dst + COL_CHUNK] = _dot(h, w_ref[:, src:src + COL_CHUNK]).astype(BF16)


def _inproj_kernel(x_ref, g_ref, w_ref, o_ref, *, col_map):
    _in_projection(x_ref[...], g_ref[...], w_ref, o_ref, col_map)


def _inproj(x, g, w, layer, col_map):
    t, d = x.shape
    n = w.shape[2]
    return pl.pallas_call(
        functools.partial(_inproj_kernel, col_map=col_map),
        out_shape=jax.ShapeDtypeStruct((t, n), BF16),
        grid=(t // ROW_TILE,),
        in_specs=[
            pl.BlockSpec((ROW_TILE, d), lambda i: (i, 0)),
            _layer_spec(g, layer),
            _layer_spec(w, layer),
        ],
        out_specs=pl.BlockSpec((ROW_TILE, n), lambda i: (i, 0)),
        compiler_params=_params("arbitrary"),
        name="inproj",
    )(x, g, w)


def _attn_tile(q_heads, kb, vb, u, acc_ref, carry_ref, rows, causal):
    tk = kb.shape[0]
    lane = lax.broadcasted_iota(jnp.int32, (1, LANES), 1)
    first_head = lane < HEAD_DIM
    vz = jnp.zeros_like(vb)
    v_heads = (jnp.where(first_head, vb, vz), jnp.where(first_head, vz, vb))
    pv = None
    for h in range(2):
        z = lax.dot_general(q_heads[h], kb, _NT, preferred_element_type=F32)
        sp = jnp.maximum(z, 0.0) + jnp.log(1.0 + jnp.exp2(jnp.abs(z) * (-LOG2E)))
        if causal is not None:
            sp = jnp.where(causal, sp, 0.0)
        suffix = _dot(sp.astype(BF16), u)
        carry = carry_ref[h, rows, :]
        w = jnp.exp2((z - suffix - jnp.tile(carry, (1, tk // LANES))) * LOG2E)
        if causal is not None:
            w = jnp.where(causal, w, 0.0)
        d = _dot(w.astype(BF16), v_heads[h])
        pv = d if pv is None else pv + d
        carry_ref[h, rows, :] = carry + suffix[:, 0:1]
    acc_ref[rows, :] += pv


def _attn_kernel(q_ref, k_ref, v_ref, u_ref, o_ref, acc_ref, carry_ref, *, tk):
    s = q_ref.shape[0]
    nb = s // tk
    lane = lax.broadcasted_iota(jnp.int32, (1, LANES), 1)
    first_head = lane < HEAD_DIM
    u = u_ref[...]
    causal = (lax.broadcasted_iota(jnp.int32, (2 * tk, tk), 1)
              < lax.broadcasted_iota(jnp.int32, (2 * tk, tk), 0))
    acc_ref[...] = jnp.zeros_like(acc_ref)
    carry_ref[...] = jnp.zeros_like(carry_ref)

    def tile(r0, nrows, kb, mask):
        rows = pl.ds(pl.multiple_of(r0, tk), nrows)
        ks = pl.multiple_of(kb * tk, tk)
        q = q_ref[rows, :] * jnp.asarray(HEAD_DIM ** -0.5, BF16)
        qz = jnp.zeros_like(q)
        q_heads = (jnp.where(first_head, q, qz), jnp.where(first_head, qz, q))
        _attn_tile(q_heads, k_ref[pl.ds(ks, tk), :], v_ref[pl.ds(ks, tk), :],
                   u, acc_ref, carry_ref, rows, mask)

    tile((nb - 1) * tk, tk, nb - 1, causal[:tk])

    def key_block(i, c):
        kb = nb - 2 - i
        tile(kb * tk, 2 * tk, kb, causal)
        return c

    lax.fori_loop(0, nb - 1, key_block, 0)

    def query_block(j, c):
        block_rows = pl.ds(pl.multiple_of(j * tk, tk), tk)

        def more(state):
            kb, min_carry = state
            return jnp.logical_and(kb >= 0, min_carry < UNDERFLOW_EXPONENT)

        def body(state):
            kb, _ = state
            tile(j * tk, tk, kb, None)
            return kb - 1, jnp.min(carry_ref[:, block_rows, :])

        lax.while_loop(more, body, (j - 2, jnp.min(carry_ref[:, block_rows, :])))
        return c

    lax.fori_loop(2, nb, query_block, 0)
    o_ref[...] = acc_ref[...].astype(BF16)


def _attention(proj, u, *, q_blk, k_blk, v_blk):
    b, s, _ = proj.shape
    hp = N_HEADS * HEAD_DIM // LANES
    cols = lambda blk: pl.BlockSpec((None, s, LANES), lambda bi, hi: (bi, 0, blk + hi))
    return pl.pallas_call(
        functools.partial(_attn_kernel, tk=ATT_TILE),
        out_shape=jax.ShapeDtypeStruct((b, s, N_HEADS * HEAD_DIM), BF16),
        grid=(b, hp),
        in_specs=[cols(q_blk), cols(k_blk), cols(v_blk),
                  pl.BlockSpec((ATT_TILE, ATT_TILE), lambda bi, hi: (0, 0))],
        out_specs=pl.BlockSpec((None, s, LANES), lambda bi, hi: (bi, 0, hi)),
        scratch_shapes=[
            pltpu.VMEM((s, LANES), F32),
            pltpu.VMEM((2, s, LANES), F32),
        ],
        compiler_params=_params("arbitrary", "arbitrary"),
        name="sb_attention",
    )(proj, proj, proj, u)


def _rglru_kernel(xr_ref, xg_ref, cw_ref, cb_ref, wa_ref, ba_ref, wx_ref, bx_ref, lam_ref,
                  o_ref, tail_ref, h_ref, a_scr, b_scr, *, ts):
    @pl.when(pl.program_id(1) == 0)
    def _():
        tail_ref[...] = jnp.zeros_like(tail_ref)
        h_ref[...] = jnp.zeros_like(h_ref)

    xr = xr_ref[...].astype(F32)
    ext = jnp.concatenate([tail_ref[...], xr], axis=0)
    tail_ref[...] = xr[ts - SUBLANES:, :]
    xc = cb_ref[...]
    for j in range(CONV_WIDTH):
        lo = SUBLANES - (CONV_WIDTH - 1) + j
        xc = xc + cw_ref[j:j + 1, :] * ext[lo:lo + ts, :]
    xcb = xc.astype(BF16)
    r = 0.5 + 0.5 * jnp.tanh(0.5 * (_dot(xcb, wa_ref[...]) + ba_ref[...]))
    ig = 0.5 + 0.5 * jnp.tanh(0.5 * (_dot(xcb, wx_ref[...]) + bx_ref[...]))
    lam = lam_ref[...]
    softplus_neg_lam = jnp.maximum(-lam, 0.0) + jnp.log(1.0 + jnp.exp(-jnp.abs(lam)))
    log_a = (-RG_C * softplus_neg_lam) * r
    a = jnp.exp(log_a)
    b = jnp.sqrt(1.0 - a * a) * (ig * xc)

    in_group = jnp.bitwise_and(lax.broadcasted_iota(jnp.int32, a.shape, 0), SUBLANES - 1)
    for s in (1, 2, 4):
        has_prev = in_group >= s
        b = jnp.where(has_prev, a * pltpu.roll(b, s, 0) + b, b)
        a = jnp.where(has_prev, a * pltpu.roll(a, s, 0), a)
    a_scr[...] = a
    b_scr[...] = b

    def group(gidx, h):
        r0 = pl.multiple_of(gidx * SUBLANES, SUBLANES)
        hg = a_scr[pl.ds(r0, SUBLANES), :] * h + b_scr[pl.ds(r0, SUBLANES), :]
        b_scr[pl.ds(r0, SUBLANES), :] = hg
        return hg[SUBLANES - 1:, :]

    h_ref[...] = lax.fori_loop(0, ts // SUBLANES, group, h_ref[...], unroll=4)
    xg = xg_ref[...].astype(F32)
    gelu = 0.5 * xg * (1.0 + jnp.tanh(0.7978845608028654 * (xg + 0.044715 * (xg * xg * xg))))
    o_ref[...] = (b_scr[...] * gelu).astype(BF16)


def _rglru(proj, cw, cb, wa, ba, wx, bx, lam, layer, *, xr_blk, xg_blk):
    b, s, _ = proj.shape
    dr = cw.shape[2]
    return pl.pallas_call(
        functools.partial(_rglru_kernel, ts=RNN_TS),
        out_shape=jax.ShapeDtypeStruct((b, s, dr), BF16),
        grid=(b, s // RNN_TS),
        in_specs=[
            pl.BlockSpec((None, RNN_TS, dr), lambda bi, si: (bi, si, xr_blk)),
            pl.BlockSpec((None, RNN_TS, dr), lambda bi, si: (bi, si, xg_blk)),
            _layer_spec(cw, layer), _layer_spec(cb, layer),
            _layer_spec(wa, layer), _layer_spec(ba, layer),
            _layer_spec(wx, layer), _layer_spec(bx, layer),
            _layer_spec(lam, layer),
        ],
        out_specs=pl.BlockSpec((None, RNN_TS, dr), lambda bi, si: (bi, si, 0)),
        scratch_shapes=[
            pltpu.VMEM((SUBLANES, dr), F32),
            pltpu.VMEM((1, dr), F32),
            pltpu.VMEM((RNN_TS, dr), F32),
            pltpu.VMEM((RNN_TS, dr), F32),
        ],
        compiler_params=_params("arbitrary", "arbitrary"),
        name="rglru",
    )(proj, proj, cw, cb, wa, ba, wx, bx, lam)


def _first_argmax(vals):
    m = vals[0]
    for v in vals[1:]:
        m = jnp.maximum(m, v)
    idx = jnp.full_like(m, float(len(vals) - 1))
    for k in range(len(vals) - 2, -1, -1):
        idx = jnp.where(vals[k] == m, float(k), idx)
    return m, idx


def _merge_kernel(x_ref, at_ref, rn_ref, ga_ref, gr_ref, wao_ref, wro_ref, wo_ref, g_ref,
                  wrh_ref, wrl_ref, br_ref, x1_ref, hr_ref, cls_ref):
    d = x_ref.shape[1]
    ya = _dot(at_ref[...], wao_ref[...])
    yr = _dot(rn_ref[...], wro_ref[...])
    merged = _sigmoid(ga_ref[...].astype(F32)) * ya + _sigmoid(gr_ref[...].astype(F32)) * yr
    x1 = x_ref[...] + _dot(merged.astype(BF16), wo_ref[...])
    x1_ref[...] = x1
    h = _rms(x1, g_ref[...])
    hr_ref[:, :d] = h

    h_hi = h.astype(BF16)
    h_lo = (h - h_hi.astype(F32)).astype(BF16)
    wrh = wrh_ref[...]
    lt = (lax.dot_general(wrh, h_hi, _NT, preferred_element_type=F32)
          + lax.dot_general(wrh, h_lo, _NT, preferred_element_type=F32)
          + lax.dot_general(wrl_ref[...], h_hi, _NT, preferred_element_type=F32))
    lt = lt + br_ref[...]
    g = [lt[k:k + 1, :] for k in range(N_GROUPS)]
    e = [lt[N_GROUPS + k:N_GROUPS + k + 1, :] for k in range(N_EXPERTS)]

    gmax, gi = _first_argmax(g)
    den = jnp.exp(g[0] - gmax)
    for k in range(1, N_GROUPS):
        den = den + jnp.exp(g[k] - gmax)
    gate = 1.0 / den
    sel = []
    for j in range(EXPERTS_PER_GROUP):
        s = e[(N_GROUPS - 1) * EXPERTS_PER_GROUP + j]
        for k in range(N_GROUPS - 2, -1, -1):
            s = jnp.where(gi == float(k), e[k * EXPERTS_PER_GROUP + j], s)
        sel.append(s)
    m1, i1 = _first_argmax(sel)
    sel2 = [jnp.where(i1 == float(j), -jnp.inf, sel[j]) for j in range(EXPERTS_PER_GROUP)]
    m2, i2 = _first_argmax(sel2)
    t = jnp.exp(m2 - m1)
    w1 = gate * (1.0 / (1.0 + t))
    w2 = gate * (t / (1.0 + t))
    first_is_lo = i1 < i2
    a = jnp.minimum(i1, i2)
    b = jnp.maximum(i1, i2)
    w_lo = jnp.where(first_is_lo, w1, w2)
    w_hi = jnp.where(first_is_lo, w2, w1)
    pair = jnp.where(a == 0.0, b - 1.0, jnp.where(a == 1.0, b + 1.0, 5.0))
    cls = gi * float(PAIRS_PER_GROUP) + pair

    n = cls.shape[1]
    cls_ref[...] = jnp.concatenate([cls, jnp.zeros((SUBLANES - 1, n), F32)], axis=0)
    rows = jnp.concatenate([w_lo, w_hi, jnp.zeros((ROUTE_LANES - 2, n), F32)], axis=0)
    hr_ref[:, d:] = rows.T


def _merge(x, attn, rnn, proj, wao, wro, wo, g, wrh, wrl, br, layer, *, ga_blk, gr_blk):
    t, d = x.shape
    da = attn.shape[1]
    dr = rnn.shape[1]
    tm = ROW_TILE
    return pl.pallas_call(
        _merge_kernel,
        out_shape=(
            jax.ShapeDtypeStruct((t, d), F32),
            jax.ShapeDtypeStruct((t, d + ROUTE_LANES), F32),
            jax.ShapeDtypeStruct((SUBLANES, t), F32),
        ),
        grid=(t // tm,),
        in_specs=[
            pl.BlockSpec((tm, d), lambda i: (i, 0)),
            pl.BlockSpec((tm, da), lambda i: (i, 0)),
            pl.BlockSpec((tm, dr), lambda i: (i, 0)),
            pl.BlockSpec((tm, d), lambda i: (i, ga_blk)),
            pl.BlockSpec((tm, d), lambda i: (i, gr_blk)),
            _layer_spec(wao, layer), _layer_spec(wro, layer), _layer_spec(wo, layer),
            _layer_spec(g, layer), _layer_spec(wrh, layer), _layer_spec(wrl, layer),
            _layer_spec(br, layer),
        ],
        out_specs=(
            pl.BlockSpec((tm, d), lambda i: (i, 0)),
            pl.BlockSpec((tm, d + ROUTE_LANES), lambda i: (i, 0)),
            pl.BlockSpec((SUBLANES, tm), lambda i: (0, i)),
        ),
        compiler_params=_params("arbitrary"),
        name="merge_router",
    )(x, attn, rnn, proj, proj, wao, wro, wo, g, wrh, wrl, br)


def _rank_kernel(cls_ref, lt_ref, pos_ref, tinfo_ref, cnt_ref, off_ref, run_ref, *, tm):
    ph = pl.program_id(0)
    bi = pl.program_id(1)
    cls = cls_ref[0:1, :]
    crow = lax.broadcasted_iota(jnp.int32, (CLASS_ROWS, tm), 0).astype(F32)
    member = crow == cls
    onehot = jnp.where(member, 1.0, 0.0)
    block_count = jnp.sum(onehot, axis=1, keepdims=True)

    @pl.when(jnp.logical_and(ph == 0, bi == 0))
    def _():
        cnt_ref[...] = jnp.zeros_like(cnt_ref)

    @pl.when(ph == 0)
    def _():
        cnt_ref[...] += block_count

    @pl.when(jnp.logical_and(ph == 1, bi == 0))
    def _():
        ntile = jnp.floor((cnt_ref[...] + float(MOE_TILE - 1)) * (1.0 / MOE_TILE))
        rid = lax.broadcasted_iota(jnp.int32, (CLASS_ROWS, LANES), 0)
        toff = jnp.zeros((CLASS_ROWS, LANES), F32)
        for c in range(1, CLASS_ROWS):
            toff = toff + jnp.where(rid >= c, ntile[c - 1:c, :], 0.0)
        off_ref[...] = toff * float(MOE_TILE)
        run_ref[...] = jnp.zeros_like(run_ref)
        ti = lax.broadcasted_iota(jnp.int32, (CLASS_ROWS, LANES), 1).astype(F32)
        inside = jnp.where(ti >= toff, jnp.where(ti < toff + ntile, 1.0, 0.0), 0.0)
        c = rid.astype(F32)
        grp = (jnp.where(c >= 6.0, 1.0, 0.0) + jnp.where(c >= 12.0, 1.0, 0.0)
               + jnp.where(c >= 18.0, 1.0, 0.0))
        pair = c - float(PAIRS_PER_GROUP) * grp
        a = jnp.where(pair >= 3.0, 1.0, 0.0) + jnp.where(pair >= 5.0, 1.0, 0.0)
        b = jnp.where(pair < 3.0, pair + 1.0, jnp.where(pair < 5.0, pair - 1.0, 3.0))
        e_lo = jnp.sum(inside * (float(EXPERTS_PER_GROUP) * grp + a), axis=0, keepdims=True)
        e_hi = jnp.sum(inside * (float(EXPERTS_PER_GROUP) * grp + b), axis=0, keepdims=True)
        first = (ti - toff) == 0.0
        tile_c = toff + jnp.where(first, ntile - 1.0, ti - toff - 1.0)
        rows_c = jnp.where(first, cnt_ref[...] - (ntile - 1.0) * float(MOE_TILE), float(MOE_TILE))
        used = jnp.sum(ntile, axis=0, keepdims=True)
        tile = jnp.sum(inside * tile_c, axis=0, keepdims=True)
        tile = tile + jnp.where(ti[0:1, :] >= used, ti[0:1, :], 0.0)
        rows = jnp.sum(inside * rows_c, axis=0, keepdims=True)
        left = cnt_ref[...] - (ti - toff) * float(MOE_TILE)
        tile_rows = jnp.sum(inside * jnp.minimum(left, float(MOE_TILE)), axis=0, keepdims=True)
        info = jnp.concatenate([e_lo, e_hi, rows, used, tile, tile_rows,
                                jnp.zeros((SUBLANES - 6, LANES), F32)], axis=0)
        tinfo_ref[...] = info.astype(jnp.int32)

    @pl.when(ph == 1)
    def _():
        before = _dot(onehot.astype(BF16), lt_ref[...])
        val = before + run_ref[:, 0:1] + off_ref[:, 0:1]
        pos = jnp.sum(jnp.where(member, val, 0.0), axis=0, keepdims=True)
        pos_ref[...] = pos.astype(jnp.int32)
        run_ref[...] += block_count


def _rank(cls, strict_lt):
    t = cls.shape[1]
    tm = ROW_TILE
    return pl.pallas_call(
        functools.partial(_rank_kernel, tm=tm),
        out_shape=(
            jax.ShapeDtypeStruct((1, t), jnp.int32),
            jax.ShapeDtypeStruct((SUBLANES, LANES), jnp.int32),
        ),
        grid=(2, t // tm),
        in_specs=[
            pl.BlockSpec((SUBLANES, tm), lambda ph, bi: (0, bi)),
            pl.BlockSpec((tm, tm), lambda ph, bi: (0, 0)),
        ],
        out_specs=(
            pl.BlockSpec((1, tm), lambda ph, bi: (0, bi * ph)),
            pl.BlockSpec((SUBLANES, LANES), lambda ph, bi: (0, 0)),
        ),
        scratch_shapes=[pltpu.VMEM((CLASS_ROWS, LANES), F32)] * 3,
        compiler_params=_params("arbitrary", "arbitrary"),
        name="rank_tokens",
    )(cls, strict_lt)


def _scatter_kernel(info_ref, pos_ref, h_ref, o_ref, zero_ref, sem, zsem, *, tm):
    tile_groups = MOE_TILE // SUBLANES

    @pl.when(pl.program_id(0) == 0)
    def _():
        zero_ref[...] = jnp.zeros_like(zero_ref)

        def tile_clear(ti):
            dst = o_ref.at[pl.ds(ti * tile_groups, tile_groups)]
            return pltpu.make_async_copy(zero_ref, dst, zsem)

        n_tiles = o_ref.shape[0] // tile_groups
        for ti in range(n_tiles):
            @pl.when(info_ref[5, ti] < MOE_TILE)
            def _():
                tile_clear(ti).start()
        for ti in range(n_tiles):
            @pl.when(info_ref[5, ti] < MOE_TILE)
            def _():
                tile_clear(ti).wait()

    def row_copy(grp, j, p):
        dst = o_ref.at[lax.shift_right_logical(p, 3), pl.ds(jnp.bitwise_and(p, SUBLANES - 1), 1)]
        return pltpu.make_async_copy(h_ref.at[grp, pl.ds(j, 1)], dst, sem)

    def start(grp, c):
        for j in range(SUBLANES):
            row_copy(grp, j, pos_ref[0, 0, grp * SUBLANES + j]).start(priority=j % 2)
        return c

    def wait(r, c):
        row_copy(0, 0, 0).wait()
        return c

    lax.fori_loop(0, tm // SUBLANES, start, 0)
    lax.fori_loop(0, tm, wait, 0, unroll=DMA_UNROLL)


def _scatter_rows(tinfo, pos, rows, n_sorted):
    t, w = rows.shape
    tm = SCATTER_TILE
    return pl.pallas_call(
        functools.partial(_scatter_kernel, tm=tm),
        out_shape=jax.ShapeDtypeStruct((n_sorted // SUBLANES, SUBLANES, w), rows.dtype),
        grid_spec=pltpu.PrefetchScalarGridSpec(
            num_scalar_prefetch=1,
            grid=(t // tm,),
            in_specs=[
                pl.BlockSpec((1, 1, tm), lambda i, info: (i, 0, 0), memory_space=pltpu.SMEM),
                pl.BlockSpec((tm // SUBLANES, SUBLANES, w), lambda i, info: (i, 0, 0)),
            ],
            out_specs=pl.BlockSpec(memory_space=pl.ANY),
            scratch_shapes=[
                pltpu.VMEM((MOE_TILE // SUBLANES, SUBLANES, w), rows.dtype),
                pltpu.SemaphoreType.DMA(()),
                pltpu.SemaphoreType.DMA(()),
            ],
        ),
        compiler_params=_params("arbitrary"),
        name="scatter_rows",
    )(tinfo, pos.reshape(t // tm, 1, tm), rows.reshape(t // SUBLANES, SUBLANES, w))


def _ffn_kernel(info_ref, x_ref, wgl_ref, wul_ref, wdl_ref, wgh_ref, wuh_ref, wdh_ref, o_ref,
                wg_s, wu_s, wd_s):
    i = pl.program_id(0)
    tm, d = o_ref.shape
    half = tm // 2
    live = i < info_ref[3, 0]
    prev = jnp.maximum(i - 1, 0)
    for k, (wg, wu, wd) in enumerate(((wgl_ref, wul_ref, wdl_ref), (wgh_ref, wuh_ref, wdh_ref))):
        changed = jnp.logical_or(i == 0, info_ref[k, i] != info_ref[k, prev])

        @pl.when(jnp.logical_and(live, changed))
        def _():
            wg_s[k] = wg[...].astype(BF16)
            wu_s[k] = wu[...].astype(BF16)
            wd_s[k] = wd[...].astype(BF16)

    def ffn(rows):
        xt = x_ref[rows, :]
        xb = xt[:, :d].astype(BF16)
        y = None
        for k in range(2):
            gte = _dot(xb, wg_s[k])
            up = _dot(xb, wu_s[k])
            he = (gte * _sigmoid(gte)) * up
            part = xt[:, d + k:d + k + 1] * _dot(he.astype(BF16), wd_s[k])
            y = part if y is None else y + part
        o_ref[rows, :] = y

    @pl.when(jnp.logical_not(live))
    def _():
        o_ref[...] = jnp.zeros_like(o_ref)

    @pl.when(jnp.logical_and(live, info_ref[2, i] > half))
    def _():
        ffn(slice(0, tm))

    @pl.when(jnp.logical_and(live, info_ref[2, i] <= half))
    def _():
        ffn(slice(0, half))
        o_ref[half:, :] = jnp.zeros((tm - half, d), F32)


def _expert_ffn(tinfo, xs, wg, wu, wd, layer):
    npad, w = xs.shape
    _, _, d, de = wg.shape
    tm = MOE_TILE

    def step(i, info):
        return jnp.minimum(i, info[3, 0] - 1)

    w_in = lambda row: pl.BlockSpec((None, None, d, de),
                                    lambda i, info: (layer, info[row, step(i, info)], 0, 0))
    w_out = lambda row: pl.BlockSpec((None, None, de, d),
                                     lambda i, info: (layer, info[row, step(i, info)], 0, 0))
    return pl.pallas_call(
        _ffn_kernel,
        out_shape=jax.ShapeDtypeStruct((npad, d), F32),
        grid_spec=pltpu.PrefetchScalarGridSpec(
            num_scalar_prefetch=1,
            grid=(npad // tm,),
            in_specs=[
                pl.BlockSpec((tm, w), lambda i, info: (info[4, step(i, info)], 0)),
                w_in(0), w_in(0), w_out(0), w_in(1), w_in(1), w_out(1),
            ],
            out_specs=pl.BlockSpec((tm, d), lambda i, info: (info[4, i], 0)),
            scratch_shapes=[
                pltpu.VMEM((2, d, de), BF16),
                pltpu.VMEM((2, d, de), BF16),
                pltpu.VMEM((2, de, d), BF16),
            ],
        ),
        compiler_params=_params("arbitrary"),
        name="expert_ffn",
    )(tinfo, xs, wg, wu, wd, wg, wu, wd)


def _ple_kernel(pos_ref, posn_ref, x_ref, p_ref, ys_ref, g_ref, wg_ref, wp_ref, gn_ref, *rest,
                tm, nsteps, col_map):
    if col_map is None:
        o_ref, ybuf, sem = rest
    else:
        win_ref, o_ref, proj_ref, ybuf, sem = rest
    i = pl.program_id(0)
    slot = lax.rem(i, 2)
    d = x_ref.shape[1]

    def row_copy(grp, j, p, s):
        src = ys_ref.at[lax.shift_right_logical(p, 3), pl.ds(jnp.bitwise_and(p, SUBLANES - 1), 1)]
        return pltpu.make_async_copy(src, ybuf.at[s, grp, pl.ds(j, 1)], sem.at[s])

    def wait_slot(s):
        def wait(r, c):
            row_copy(0, 0, 0, s).wait()
            return c
        lax.fori_loop(0, tm, wait, 0, unroll=DMA_UNROLL)

    @pl.when(i == 0)
    def _():
        def start(grp, c):
            for j in range(SUBLANES):
                row_copy(grp, j, pos_ref[0, 0, grp * SUBLANES + j], 0).start()
            return c
        lax.fori_loop(0, tm // SUBLANES, start, 0)

    wait_slot(slot)

    n_stage = 1 if col_map is None else len(col_map)
    per_stage = -(-(tm // SUBLANES) // n_stage)

    def prefetch(stage):
        for grp in range(stage * per_stage, min((stage + 1) * per_stage, tm // SUBLANES)):
            for j in range(SUBLANES):
                row_copy(grp, j, posn_ref[0, 0, grp * SUBLANES + j], 1 - slot).start()

    x2 = x_ref[...] + ybuf[slot].reshape(tm, d)
    gate = _sigmoid(_dot(_rms(x2, g_ref[...]).astype(BF16), wg_ref[...]))
    x3 = x2 + gate * _dot(p_ref[...].astype(BF16), wp_ref[...])
    if col_map is None:
        o_ref[...] = _rms(x3, gn_ref[...])
        prefetch(0)
    else:
        o_ref[...] = x3
        h = _rms(x3, gn_ref[...]).astype(BF16)
        for stage, (src, dst) in enumerate(col_map):
            prefetch(stage)
            proj_ref[:, dst:dst + COL_CHUNK] = _dot(h, win_ref[:, src:src + COL_CHUNK]).astype(BF16)

    @pl.when(i == nsteps - 1)
    def _():
        wait_slot(1 - slot)


def _ple(pos, x, p, ys, g, wg, wp, layer, *, g_next, next_layer, w_in=None, col_map=None):
    t, d = x.shape
    dp = p.shape[2]
    tm = PLE_TILE
    nsteps = t // tm
    pos3 = pos.reshape(nsteps, 1, tm)
    in_specs = [
        pl.BlockSpec((1, 1, tm), lambda i: (i, 0, 0), memory_space=pltpu.SMEM),
        pl.BlockSpec((1, 1, tm), lambda i: (jnp.minimum(i + 1, nsteps - 1), 0, 0),
                     memory_space=pltpu.SMEM),
        pl.BlockSpec((tm, d), lambda i: (i, 0)),
        pl.BlockSpec((None, tm, dp), lambda i: (layer, i, 0)),
        pl.BlockSpec(memory_space=pl.ANY),
        _layer_spec(g, layer), _layer_spec(wg, layer), _layer_spec(wp, layer),
        _layer_spec(g_next, next_layer),
    ]
    args = [pos3, pos3, x, p, ys.reshape(-1, SUBLANES, d), g, wg, wp, g_next]
    out_shape = jax.ShapeDtypeStruct((t, d), F32)
    out_specs = pl.BlockSpec((tm, d), lambda i: (i, 0))
    if col_map is not None:
        n = w_in.shape[2]
        in_specs.append(_layer_spec(w_in, next_layer))
        args.append(w_in)
        out_shape = (out_shape, jax.ShapeDtypeStruct((t, n), BF16))
        out_specs = (out_specs, pl.BlockSpec((tm, n), lambda i: (i, 0)))
    return pl.pallas_call(
        functools.partial(_ple_kernel, tm=tm, nsteps=nsteps, col_map=col_map),
        out_shape=out_shape,
        grid=(nsteps,),
        in_specs=in_specs,
        out_specs=out_specs,
        scratch_shapes=[pltpu.VMEM((2, tm // SUBLANES, SUBLANES, d), F32),
                        pltpu.SemaphoreType.DMA((2,))],
        compiler_params=_params("arbitrary"),
        name="gather_ple",
    )(*args)


def _block_diag(w):
    l, g, n, _ = w.shape
    eye = jnp.eye(g, dtype=w.dtype)
    return (eye[None, :, None, :, None] * w[:, :, :, None, :]).reshape(l, g * n, g * n)


def kernel(x, p, norm_mix, w_in, conv_w, conv_b, w_rg_a, b_rg_a, w_rg_x, b_rg_x, rg_lambda,
           w_attn_o, w_rnn_o, w_out, norm_moe, w_router_group, b_router_group, w_router_expert,
           b_router_expert, w_exp_gate, w_exp_up, w_exp_down, norm_ple, w_ple_gate, w_ple_proj,
           norm_final):
    bsz, seq, d = x.shape
    depth = w_in.shape[0]
    t = bsz * seq
    d_attn = N_HEADS * HEAD_DIM
    d_rnn = conv_w.shape[2]
    assert d_attn == d_rnn and d == 2 * d_attn and d_attn == COL_CHUNK
    assert t % ROW_TILE == 0 and seq % (2 * ATT_TILE) == 0 and seq % RNN_TS == 0
    assert t % SCATTER_TILE == 0 and t % PLE_TILE == 0

    n_in = w_in.shape[2]
    gates_at = 3 * d_attn + 2 * d_rnn
    col_map = tuple((src, (src - gates_at) % n_in) for src in range(0, n_in, COL_CHUNK))
    ga_blk, gr_blk = 0, 1
    q_col = 2 * d
    q_blk, k_blk, v_blk = (q_col // LANES, (q_col + d_attn) // LANES, (q_col + 2 * d_attn) // LANES)
    xr_blk, xg_blk = (q_col + 3 * d_attn) // d_rnn, (q_col + 3 * d_attn + d_rnn) // d_rnn

    tri_incl = jnp.tril(jnp.ones((ATT_TILE, ATT_TILE), F32)).astype(BF16)
    strict_lt = jnp.triu(jnp.ones((ROW_TILE, ROW_TILE), F32), k=1).astype(BF16)

    n_tiles = (t + N_CLASSES * (MOE_TILE - 1)) // MOE_TILE
    assert n_tiles <= LANES
    n_sorted = n_tiles * MOE_TILE

    vec = lambda v: v.reshape(depth, 1, -1)
    w_in_b = w_in.astype(BF16)
    wa_bd = _block_diag(w_rg_a).astype(BF16)
    wx_bd = _block_diag(w_rg_x).astype(BF16)
    w_r = jnp.concatenate([w_router_group, w_router_expert], axis=2).transpose(0, 2, 1)
    w_r = jnp.pad(w_r, ((0, 0), (0, CLASS_ROWS - w_r.shape[1]), (0, 0)))
    w_r_hi = w_r.astype(BF16)
    w_r_lo = (w_r - w_r_hi.astype(F32)).astype(BF16)
    b_r = jnp.concatenate([b_router_group, b_router_expert], axis=1)
    b_r = jnp.pad(b_r, ((0, 0), (0, CLASS_ROWS - b_r.shape[1]))).reshape(depth, CLASS_ROWS, 1)
    wao_b, wro_b, wo_b = w_attn_o.astype(BF16), w_rnn_o.astype(BF16), w_out.astype(BF16)
    wpg_b, wpp_b = w_ple_gate.astype(BF16), w_ple_proj.astype(BF16)
    g_mix, g_moe, g_ple = vec(norm_mix), vec(norm_moe), vec(norm_ple)
    g_fin = norm_final.reshape(1, 1, -1)
    p3 = p.reshape(depth, t, -1)

    x2d = x.reshape(t, d)
    proj = _inproj(x2d, g_mix, w_in_b, 0, col_map)
    for i in range(depth):
        proj3 = proj.reshape(bsz, seq, -1)
        attn = _attention(proj3, tri_incl, q_blk=q_blk, k_blk=k_blk, v_blk=v_blk)
        rnn = _rglru(proj3, conv_w, vec(conv_b), wa_bd, vec(b_rg_a), wx_bd, vec(b_rg_x),
                     vec(rg_lambda), i, xr_blk=xr_blk, xg_blk=xg_blk)
        x1, routed, cls = _merge(
            x2d, attn.reshape(t, d_attn), rnn.reshape(t, d_rnn), proj, wao_b, wro_b, wo_b,
            g_moe, w_r_hi, w_r_lo, b_r, i, ga_blk=ga_blk, gr_blk=gr_blk)
        pos, tinfo = _rank(cls, strict_lt)
        pos = pos.reshape(t)
        xs = _scatter_rows(tinfo, pos, routed, n_sorted)
        ys = _expert_ffn(tinfo, xs.reshape(n_sorted, -1), w_exp_gate, w_exp_up, w_exp_down, i)
        if i + 1 < depth:
            x2d, proj = _ple(pos, x1, p3, ys, g_ple, wpg_b, wpp_b, i, g_next=g_mix,
                             next_layer=i + 1, w_in=w_in_b, col_map=col_map)
        else:
            x2d = _ple(pos, x1, p3, ys, g_ple, wpg_b, wpp_b, i, g_next=g_fin, next_layer=0)
    return x2d.reshape(bsz, seq, d)
```

```python
import functools

import jax
import jax.numpy as jnp
from jax import lax
from jax.experimental import pallas as pl
from jax.experimental.pallas import tpu as pltpu

F32 = jnp.float32
BF16 = jnp.bfloat16
EPS = 1e-6

N_HEADS = 8
HEAD_DIM = 64
N_GROUPS = 4
EXPERTS_PER_GROUP = 4
N_EXPERTS = N_GROUPS * EXPERTS_PER_GROUP
PAIRS_PER_GROUP = 6
N_CLASSES = N_GROUPS * PAIRS_PER_GROUP
CLASS_ROWS = 32
CONV_WIDTH = 4
RG_C = 8.0

LANES = 128
SUBLANES = 8
VMEM_LIMIT = 56 * 1024 * 1024

ROW_TILE = 512
RANK_TILE = 1024
ATT_TILE = 256
RNN_TS = 256
MOE_TILE = 512
SCATTER_TILE = 1024
PLE_TILE = 512
ROUTE_LANES = 128
DMA_UNROLL = 8
COL_CHUNK = 512

_NT = (((1,), (1,)), ((), ()))
LOG2E = 1.4426950408889634
UNDERFLOW_EXPONENT = 151.0 / LOG2E


def _params(*sem):
    return pltpu.CompilerParams(dimension_semantics=sem, vmem_limit_bytes=VMEM_LIMIT)


def _layer_spec(arr, layer):
    nd = arr.ndim
    return pl.BlockSpec((None,) + arr.shape[1:], lambda *_: (layer,) + (0,) * (nd - 1))


def _rms(x, g):
    ms = jnp.mean(x * x, axis=-1, keepdims=True)
    return x * lax.rsqrt(ms + EPS) * g


def _sigmoid(x):
    return 1.0 / (1.0 + jnp.exp(-x))


def _dot(a, b):
    return jnp.dot(a, b, preferred_element_type=F32)


def _in_projection(x, g, w_ref, o_ref, col_map):
    h = _rms(x, g).astype(BF16)
    for src, dst in col_map:
        o_ref[:, dst:dst + COL_CHUNK] = _dot(h, w_ref[:, src:src + COL_CHUNK]).astype(BF16)


def _inproj_kernel(x_ref, g_ref, w_ref, o_ref, *, col_map):
    _in_projection(x_ref[...], g_ref[...], w_ref, o_ref, col_map)


def _inproj(x, g, w, layer, col_map):
    t, d = x.shape
    n = w.shape[2]
    return pl.pallas_call(
        functools.partial(_inproj_kernel, col_map=col_map),
        out_shape=jax.ShapeDtypeStruct((t, n), BF16),
        grid=(t // ROW_TILE,),
        in_specs=[
            pl.BlockSpec((ROW_TILE, d), lambda i: (i, 0)),
            _layer_spec(g, layer),
            _layer_spec(w, layer),
        ],
        out_specs=pl.BlockSpec((ROW_TILE, n), lambda i: (i, 0)),
        compiler_params=_params("arbitrary"),
        name="inproj",
    )(x, g, w)


def _attn_tile(q_heads, kb, vb, u, acc_ref, carry_ref, rows, causal):
    tk = kb.shape[0]
    lane = lax.broadcasted_iota(jnp.int32, (1, LANES), 1)
    first_head = lane < HEAD_DIM
    vz = jnp.zeros_like(vb)
    v_heads = (jnp.where(first_head, vb, vz), jnp.where(first_head, vz, vb))
    pv = None
    for h in range(2):
        z = lax.dot_general(q_heads[h], kb, _NT, preferred_element_type=F32)
        sp = jnp.maximum(z, 0.0) + jnp.log(1.0 + jnp.exp2(jnp.abs(z) * (-LOG2E)))
        if causal is not None:
            sp = jnp.where(causal, sp, 0.0)
        suffix = _dot(sp.astype(BF16), u)
        carry = carry_ref[h, rows, :]
        w = jnp.exp2((z - suffix - jnp.tile(carry, (1, tk // LANES))) * LOG2E)
        if causal is not None:
            w = jnp.where(causal, w, 0.0)
        d = _dot(w.astype(BF16), v_heads[h])
        pv = d if pv is None else pv + d
        carry_ref[h, rows, :] = carry + suffix[:, 0:1]
    acc_ref[rows, :] += pv


def _attn_kernel(q_ref, k_ref, v_ref, u_ref, o_ref, acc_ref, carry_ref, *, tk):
    s = q_ref.shape[0]
    nb = s // tk
    lane = lax.broadcasted_iota(jnp.int32, (1, LANES), 1)
    first_head = lane < HEAD_DIM
    u = u_ref[...]
    causal = (lax.broadcasted_iota(jnp.int32, (2 * tk, tk), 1)
              < lax.broadcasted_iota(jnp.int32, (2 * tk, tk), 0))
    acc_ref[...] = jnp.zeros_like(acc_ref)
    carry_ref[...] = jnp.zeros_like(carry_ref)

    def tile(r0, nrows, kb, mask):
        rows = pl.ds(pl.multiple_of(r0, tk), nrows)
        ks = pl.multiple_of(kb * tk, tk)
        q = q_ref[rows, :] * jnp.asarray(HEAD_DIM ** -0.5, BF16)
        qz = jnp.zeros_like(q)
        q_heads = (jnp.where(first_head, q, qz), jnp.where(first_head, qz, q))
        _attn_tile(q_heads, k_ref[pl.ds(ks, tk), :], v_ref[pl.ds(ks, tk), :],
                   u, acc_ref, carry_ref, rows, mask)

    tile((nb - 1) * tk, tk, nb - 1, causal[:tk])

    def key_block(i, c):
        kb = nb - 2 - i
        tile(kb * tk, 2 * tk, kb, causal)
        return c

    lax.fori_loop(0, nb - 1, key_block, 0)

    def query_block(j, c):
        block_rows = pl.ds(pl.multiple_of(j * tk, tk), tk)

        def more(state):
            kb, min_carry = state
            return jnp.logical_and(kb >= 0, min_carry < UNDERFLOW_EXPONENT)

        def body(state):
            kb, _ = state
            tile(j * tk, tk, kb, None)
            return kb - 1, jnp.min(carry_ref[:, block_rows, :])

        lax.while_loop(more, body, (j - 2, jnp.min(carry_ref[:, block_rows, :])))
        return c

    lax.fori_loop(2, nb, query_block, 0)
    o_ref[...] = acc_ref[...].astype(BF16)


def _attention(proj, u, *, q_blk, k_blk, v_blk):
    b, s, _ = proj.shape
    hp = N_HEADS * HEAD_DIM // LANES
    cols = lambda blk: pl.BlockSpec((None, s, LANES), lambda bi, hi: (bi, 0, blk + hi))
    return pl.pallas_call(
        functools.partial(_attn_kernel, tk=ATT_TILE),
        out_shape=jax.ShapeDtypeStruct((b, s, N_HEADS * HEAD_DIM), BF16),
        grid=(b, hp),
        in_specs=[cols(q_blk), cols(k_blk), cols(v_blk),
                  pl.BlockSpec((ATT_TILE, ATT_TILE), lambda bi, hi: (0, 0))],
        out_specs=pl.BlockSpec((None, s, LANES), lambda bi, hi: (bi, 0, hi)),
        scratch_shapes=[
            pltpu.VMEM((s, LANES), F32),
            pltpu.VMEM((2, s, LANES), F32),
        ],
        compiler_params=_params("arbitrary", "arbitrary"),
        name="sb_attention",
    )(proj, proj, proj, u)


def _rglru_kernel(xr_ref, xg_ref, cw_ref, cb_ref, wa_ref, ba_ref, wx_ref, bx_ref, lam_ref,
                  o_ref, tail_ref, h_ref, a_scr, b_scr, *, ts):
    @pl.when(pl.program_id(1) == 0)
    def _():
        tail_ref[...] = jnp.zeros_like(tail_ref)
        h_ref[...] = jnp.zeros_like(h_ref)

    xr = xr_ref[...].astype(F32)
    ext = jnp.concatenate([tail_ref[...], xr], axis=0)
    tail_ref[...] = xr[ts - SUBLANES:, :]
    xc = cb_ref[...]
    for j in range(CONV_WIDTH):
        lo = SUBLANES - (CONV_WIDTH - 1) + j
        xc = xc + cw_ref[j:j + 1, :] * ext[lo:lo + ts, :]
    xcb = xc.astype(BF16)
    r = 0.5 + 0.5 * jnp.tanh(0.5 * (_dot(xcb, wa_ref[...]) + ba_ref[...]))
    ig = 0.5 + 0.5 * jnp.tanh(0.5 * (_dot(xcb, wx_ref[...]) + bx_ref[...]))
    lam = lam_ref[...]
    softplus_neg_lam = jnp.maximum(-lam, 0.0) + jnp.log(1.0 + jnp.exp(-jnp.abs(lam)))
    log_a = (-RG_C * softplus_neg_lam) * r
    a = jnp.exp(log_a)
    b = jnp.sqrt(1.0 - a * a) * (ig * xc)

    in_group = jnp.bitwise_and(lax.broadcasted_iota(jnp.int32, a.shape, 0), SUBLANES - 1)
    for s in (1, 2, 4):
        has_prev = in_group >= s
        b = jnp.where(has_prev, a * pltpu.roll(b, s, 0) + b, b)
        a = jnp.where(has_prev, a * pltpu.roll(a, s, 0), a)
    a_scr[...] = a
    b_scr[...] = b

    def group(gidx, h):
        r0 = pl.multiple_of(gidx * SUBLANES, SUBLANES)
        hg = a_scr[pl.ds(r0, SUBLANES), :] * h + b_scr[pl.ds(r0, SUBLANES), :]
        b_scr[pl.ds(r0, SUBLANES), :] = hg
        return hg[SUBLANES - 1:, :]

    h_ref[...] = lax.fori_loop(0, ts // SUBLANES, group, h_ref[...], unroll=4)
    xg = xg_ref[...].astype(F32)
    gelu = 0.5 * xg * (1.0 + jnp.tanh(0.7978845608028654 * (xg + 0.044715 * (xg * xg * xg))))
    o_ref[...] = (b_scr[...] * gelu).astype(BF16)


def _rglru(proj, cw, cb, wa, ba, wx, bx, lam, layer, *, xr_blk, xg_blk):
    b, s, _ = proj.shape
    dr = cw.shape[2]
    return pl.pallas_call(
        functools.partial(_rglru_kernel, ts=RNN_TS),
        out_shape=jax.ShapeDtypeStruct((b, s, dr), BF16),
        grid=(b, s // RNN_TS),
        in_specs=[
            pl.BlockSpec((None, RNN_TS, dr), lambda bi, si: (bi, si, xr_blk)),
            pl.BlockSpec((None, RNN_TS, dr), lambda bi, si: (bi, si, xg_blk)),
            _layer_spec(cw, layer), _layer_spec(cb, layer),
            _layer_spec(wa, layer), _layer_spec(ba, layer),
            _layer_spec(wx, layer), _layer_spec(bx, layer),
            _layer_spec(lam, layer),
        ],
        out_specs=pl.BlockSpec((None, RNN_TS, dr), lambda bi, si: (bi, si, 0)),
        scratch_shapes=[
            pltpu.VMEM((SUBLANES, dr), F32),
            pltpu.VMEM((1, dr), F32),
            pltpu.VMEM((RNN_TS, dr), F32),
            pltpu.VMEM((RNN_TS, dr), F32),
        ],
        compiler_params=_params("arbitrary", "arbitrary"),
        name="rglru",
    )(proj, proj, cw, cb, wa, ba, wx, bx, lam)


def _first_argmax(vals):
    m = vals[0]
    for v in vals[1:]:
        m = jnp.maximum(m, v)
    idx = jnp.full_like(m, float(len(vals) - 1))
    for k in range(len(vals) - 2, -1, -1):
        idx = jnp.where(vals[k] == m, float(k), idx)
    return m, idx


def _merge_kernel(x_ref, at_ref, rn_ref, ga_ref, gr_ref, wao_ref, wro_ref, wo_ref, g_ref,
                  wr_ref, br_ref, x1_ref, hr_ref, cls_ref):
    d = x_ref.shape[1]
    ya = _dot(at_ref[...], wao_ref[...])
    yr = _dot(rn_ref[...], wro_ref[...])
    merged = _sigmoid(ga_ref[...].astype(F32)) * ya + _sigmoid(gr_ref[...].astype(F32)) * yr
    x1 = x_ref[...] + _dot(merged.astype(BF16), wo_ref[...])
    x1_ref[...] = x1
    h = _rms(x1, g_ref[...])
    hr_ref[:, :d] = h

    h_hi = h.astype(BF16)
    h_lo = (h - h_hi.astype(F32)).astype(BF16)
    wr = wr_ref[...]
    la = _dot(h_hi, wr)
    lb = _dot(h_lo, wr)
    logits = la + pltpu.roll(la, LANES - CLASS_ROWS, 1) + lb
    lt = logits.T[:CLASS_ROWS, :] + br_ref[...]
    g = [lt[k:k + 1, :] for k in range(N_GROUPS)]
    e = [lt[N_GROUPS + k:N_GROUPS + k + 1, :] for k in range(N_EXPERTS)]

    gmax, gi = _first_argmax(g)
    den = jnp.exp(g[0] - gmax)
    for k in range(1, N_GROUPS):
        den = den + jnp.exp(g[k] - gmax)
    gate = 1.0 / den
    sel = []
    for j in range(EXPERTS_PER_GROUP):
        s = e[(N_GROUPS - 1) * EXPERTS_PER_GROUP + j]
        for k in range(N_GROUPS - 2, -1, -1):
            s = jnp.where(gi == float(k), e[k * EXPERTS_PER_GROUP + j], s)
        sel.append(s)
    m1, i1 = _first_argmax(sel)
    sel2 = [jnp.where(i1 == float(j), -jnp.inf, sel[j]) for j in range(EXPERTS_PER_GROUP)]
    m2, i2 = _first_argmax(sel2)
    t = jnp.exp(m2 - m1)
    w1 = gate * (1.0 / (1.0 + t))
    w2 = gate * (t / (1.0 + t))
    first_is_lo = i1 < i2
    a = jnp.minimum(i1, i2)
    b = jnp.maximum(i1, i2)
    w_lo = jnp.where(first_is_lo, w1, w2)
    w_hi = jnp.where(first_is_lo, w2, w1)
    pair = jnp.where(a == 0.0, b - 1.0, jnp.where(a == 1.0, b + 1.0, 5.0))
    cls = gi * float(PAIRS_PER_GROUP) + pair

    n = cls.shape[1]
    cls_ref[...] = jnp.concatenate([cls, jnp.zeros((SUBLANES - 1, n), F32)], axis=0)
    rows = jnp.concatenate([w_lo, w_hi, jnp.zeros((ROUTE_LANES - 2, n), F32)], axis=0)
    hr_ref[:, d:] = rows.T


def _merge(x, attn, rnn, proj, wao, wro, wo, g, wr, br, layer, *, ga_blk, gr_blk):
    t, d = x.shape
    da = attn.shape[1]
    dr = rnn.shape[1]
    tm = ROW_TILE
    return pl.pallas_call(
        _merge_kernel,
        out_shape=(
            jax.ShapeDtypeStruct((t, d), F32),
            jax.ShapeDtypeStruct((t, d + ROUTE_LANES), F32),
            jax.ShapeDtypeStruct((SUBLANES, t), F32),
        ),
        grid=(t // tm,),
        in_specs=[
            pl.BlockSpec((tm, d), lambda i: (i, 0)),
            pl.BlockSpec((tm, da), lambda i: (i, 0)),
            pl.BlockSpec((tm, dr), lambda i: (i, 0)),
            pl.BlockSpec((tm, d), lambda i: (i, ga_blk)),
            pl.BlockSpec((tm, d), lambda i: (i, gr_blk)),
            _layer_spec(wao, layer), _layer_spec(wro, layer), _layer_spec(wo, layer),
            _layer_spec(g, layer), _layer_spec(wr, layer), _layer_spec(br, layer),
        ],
        out_specs=(
            pl.BlockSpec((tm, d), lambda i: (i, 0)),
            pl.BlockSpec((tm, d + ROUTE_LANES), lambda i: (i, 0)),
            pl.BlockSpec((SUBLANES, tm), lambda i: (0, i)),
        ),
        compiler_params=_params("arbitrary"),
        name="merge_router",
    )(x, attn, rnn, proj, proj, wao, wro, wo, g, wr, br)


def _rank_kernel(cls_ref, lt_ref, pos_ref, tinfo_ref, cnt_ref, off_ref, run_ref, *, tm):
    ph = pl.program_id(0)
    bi = pl.program_id(1)
    cls = cls_ref[0:1, :]
    crow = lax.broadcasted_iota(jnp.int32, (CLASS_ROWS, tm), 0).astype(F32)
    member = crow == cls
    onehot = jnp.where(member, 1.0, 0.0)
    block_count = jnp.sum(onehot, axis=1, keepdims=True)

    @pl.when(jnp.logical_and(ph == 0, bi == 0))
    def _():
        cnt_ref[...] = jnp.zeros_like(cnt_ref)

    @pl.when(ph == 0)
    def _():
        cnt_ref[...] += block_count

    @pl.when(jnp.logical_and(ph == 1, bi == 0))
    def _():
        ntile = jnp.floor((cnt_ref[...] + float(MOE_TILE - 1)) * (1.0 / MOE_TILE))
        rid = lax.broadcasted_iota(jnp.int32, (CLASS_ROWS, LANES), 0)
        toff = jnp.zeros((CLASS_ROWS, LANES), F32)
        for c in range(1, CLASS_ROWS):
            toff = toff + jnp.where(rid >= c, ntile[c - 1:c, :], 0.0)
        off_ref[...] = toff * float(MOE_TILE)
        run_ref[...] = jnp.zeros_like(run_ref)
        ti = lax.broadcasted_iota(jnp.int32, (CLASS_ROWS, LANES), 1).astype(F32)
        inside = jnp.where(ti >= toff, jnp.where(ti < toff + ntile, 1.0, 0.0), 0.0)
        c = rid.astype(F32)
        grp = (jnp.where(c >= 6.0, 1.0, 0.0) + jnp.where(c >= 12.0, 1.0, 0.0)
               + jnp.where(c >= 18.0, 1.0, 0.0))
        pair = c - float(PAIRS_PER_GROUP) * grp
        a = jnp.where(pair >= 3.0, 1.0, 0.0) + jnp.where(pair >= 5.0, 1.0, 0.0)
        b = jnp.where(pair < 3.0, pair + 1.0, jnp.where(pair < 5.0, pair - 1.0, 3.0))
        e_lo = jnp.sum(inside * (float(EXPERTS_PER_GROUP) * grp + a), axis=0, keepdims=True)
        e_hi = jnp.sum(inside * (float(EXPERTS_PER_GROUP) * grp + b), axis=0, keepdims=True)
        first = (ti - toff) == 0.0
        tile_c = toff + jnp.where(first, ntile - 1.0, ti - toff - 1.0)
        rows_c = jnp.where(first, cnt_ref[...] - (ntile - 1.0) * float(MOE_TILE), float(MOE_TILE))
        used = jnp.sum(ntile, axis=0, keepdims=True)
        tile = jnp.sum(inside * tile_c, axis=0, keepdims=True)
        tile = tile + jnp.where(ti[0:1, :] >= used, ti[0:1, :], 0.0)
        rows = jnp.sum(inside * rows_c, axis=0, keepdims=True)
        left = cnt_ref[...] - (ti - toff) * float(MOE_TILE)
        tile_rows = jnp.sum(inside * jnp.minimum(left, float(MOE_TILE)), axis=0, keepdims=True)
        info = jnp.concatenate([e_lo, e_hi, rows, used, tile, tile_rows,
                                jnp.zeros((SUBLANES - 6, LANES), F32)], axis=0)
        tinfo_ref[...] = info.astype(jnp.int32)

    @pl.when(ph == 1)
    def _():
        before = _dot(onehot.astype(BF16), lt_ref[...])
        val = before + run_ref[:, 0:1] + off_ref[:, 0:1]
        pos = jnp.sum(jnp.where(member, val, 0.0), axis=0, keepdims=True)
        pos_ref[...] = pos.astype(jnp.int32)
        run_ref[...] += block_count


def _rank(cls, strict_lt):
    t = cls.shape[1]
    tm = RANK_TILE
    return pl.pallas_call(
        functools.partial(_rank_kernel, tm=tm),
        out_shape=(
            jax.ShapeDtypeStruct((1, t), jnp.int32),
            jax.ShapeDtypeStruct((SUBLANES, LANES), jnp.int32),
        ),
        grid=(2, t // tm),
        in_specs=[
            pl.BlockSpec((SUBLANES, tm), lambda ph, bi: (0, bi)),
            pl.BlockSpec((tm, tm), lambda ph, bi: (0, 0)),
        ],
        out_specs=(
            pl.BlockSpec((1, tm), lambda ph, bi: (0, bi * ph)),
            pl.BlockSpec((SUBLANES, LANES), lambda ph, bi: (0, 0)),
        ),
        scratch_shapes=[pltpu.VMEM((CLASS_ROWS, LANES), F32)] * 3,
        compiler_params=_params("arbitrary", "arbitrary"),
        name="rank_tokens",
    )(cls, strict_lt)


def _scatter_kernel(info_ref, pos_ref, h_ref, o_ref, zero_ref, sem, zsem, *, tm):
    tile_groups = MOE_TILE // SUBLANES

    @pl.when(pl.program_id(0) == 0)
    def _():
        zero_ref[...] = jnp.zeros_like(zero_ref)

        def tile_clear(ti):
            dst = o_ref.at[pl.ds(ti * tile_groups, tile_groups)]
            return pltpu.make_async_copy(zero_ref, dst, zsem)

        n_tiles = o_ref.shape[0] // tile_groups
        for ti in range(n_tiles):
            @pl.when(info_ref[5, ti] < MOE_TILE)
            def _():
                tile_clear(ti).start()
        for ti in range(n_tiles):
            @pl.when(info_ref[5, ti] < MOE_TILE)
            def _():
                tile_clear(ti).wait()

    def row_copy(grp, j, p):
        dst = o_ref.at[lax.shift_right_logical(p, 3), pl.ds(jnp.bitwise_and(p, SUBLANES - 1), 1)]
        return pltpu.make_async_copy(h_ref.at[grp, pl.ds(j, 1)], dst, sem)

    def start(grp, c):
        for j in range(SUBLANES):
            row_copy(grp, j, pos_ref[0, 0, grp * SUBLANES + j]).start(priority=j % 2)
        return c

    def wait(r, c):
        row_copy(0, 0, 0).wait()
        return c

    lax.fori_loop(0, tm // SUBLANES, start, 0)
    lax.fori_loop(0, tm, wait, 0, unroll=DMA_UNROLL)


def _scatter_rows(tinfo, pos, rows, n_sorted):
    t, w = rows.shape
    tm = SCATTER_TILE
    return pl.pallas_call(
        functools.partial(_scatter_kernel, tm=tm),
        out_shape=jax.ShapeDtypeStruct((n_sorted // SUBLANES, SUBLANES, w), rows.dtype),
        grid_spec=pltpu.PrefetchScalarGridSpec(
            num_scalar_prefetch=1,
            grid=(t // tm,),
            in_specs=[
                pl.BlockSpec((1, 1, tm), lambda i, info: (i, 0, 0), memory_space=pltpu.SMEM),
                pl.BlockSpec((tm // SUBLANES, SUBLANES, w), lambda i, info: (i, 0, 0)),
            ],
            out_specs=pl.BlockSpec(memory_space=pl.ANY),
            scratch_shapes=[
                pltpu.VMEM((MOE_TILE // SUBLANES, SUBLANES, w), rows.dtype),
                pltpu.SemaphoreType.DMA(()),
                pltpu.SemaphoreType.DMA(()),
            ],
        ),
        compiler_params=_params("arbitrary"),
        name="scatter_rows",
    )(tinfo, pos.reshape(t // tm, 1, tm), rows.reshape(t // SUBLANES, SUBLANES, w))


def _ffn_kernel(info_ref, x_ref, wgl_ref, wul_ref, wdl_ref, wgh_ref, wuh_ref, wdh_ref, o_ref,
                wg_s, wu_s, wd_s):
    i = pl.program_id(0)
    tm, d = o_ref.shape
    half = tm // 2
    live = i < info_ref[3, 0]
    prev = jnp.maximum(i - 1, 0)
    for k, (wg, wu, wd) in enumerate(((wgl_ref, wul_ref, wdl_ref), (wgh_ref, wuh_ref, wdh_ref))):
        changed = jnp.logical_or(i == 0, info_ref[k, i] != info_ref[k, prev])

        @pl.when(jnp.logical_and(live, changed))
        def _():
            wg_s[k] = wg[...].astype(BF16)
            wu_s[k] = wu[...].astype(BF16)
            wd_s[k] = wd[...].astype(BF16)

    def ffn(rows):
        xt = x_ref[rows, :]
        xb = xt[:, :d].astype(BF16)
        y = None
        for k in range(2):
            gte = _dot(xb, wg_s[k])
            up = _dot(xb, wu_s[k])
            he = (gte * _sigmoid(gte)) * up
            part = xt[:, d + k:d + k + 1] * _dot(he.astype(BF16), wd_s[k])
            y = part if y is None else y + part
        o_ref[rows, :] = y

    @pl.when(jnp.logical_not(live))
    def _():
        o_ref[...] = jnp.zeros_like(o_ref)

    @pl.when(jnp.logical_and(live, info_ref[2, i] > half))
    def _():
        ffn(slice(0, tm))

    @pl.when(jnp.logical_and(live, info_ref[2, i] <= half))
    def _():
        ffn(slice(0, half))
        o_ref[half:, :] = jnp.zeros((tm - half, d), F32)


def _expert_ffn(tinfo, xs, wg, wu, wd, layer):
    npad, w = xs.shape
    _, _, d, de = wg.shape
    tm = MOE_TILE

    def step(i, info):
        return jnp.minimum(i, info[3, 0] - 1)

    w_in = lambda row: pl.BlockSpec((None, None, d, de),
                                    lambda i, info: (layer, info[row, step(i, info)], 0, 0))
    w_out = lambda row: pl.BlockSpec((None, None, de, d),
                                     lambda i, info: (layer, info[row, step(i, info)], 0, 0))
    return pl.pallas_call(
        _ffn_kernel,
        out_shape=jax.ShapeDtypeStruct((npad, d), F32),
        grid_spec=pltpu.PrefetchScalarGridSpec(
            num_scalar_prefetch=1,
            grid=(npad // tm,),
            in_specs=[
                pl.BlockSpec((tm, w), lambda i, info: (info[4, step(i, info)], 0)),
                w_in(0), w_in(0), w_out(0), w_in(1), w_in(1), w_out(1),
            ],
            out_specs=pl.BlockSpec((tm, d), lambda i, info: (info[4, i], 0)),
            scratch_shapes=[
                pltpu.VMEM((2, d, de), BF16),
                pltpu.VMEM((2, d, de), BF16),
                pltpu.VMEM((2, de, d), BF16),
            ],
        ),
        compiler_params=_params("arbitrary"),
        name="expert_ffn",
    )(tinfo, xs, wg, wu, wd, wg, wu, wd)


def _ple_kernel(pos_ref, posn_ref, x_ref, p_ref, ys_ref, g_ref, wg_ref, wp_ref, gn_ref, *rest,
                tm, nsteps, col_map):
    if col_map is None:
        o_ref, ybuf, sem = rest
    else:
        win_ref, o_ref, proj_ref, ybuf, sem = rest
    i = pl.program_id(0)
    slot = lax.rem(i, 2)
    d = x_ref.shape[1]

    def row_copy(grp, j, p, s):
        src = ys_ref.at[lax.shift_right_logical(p, 3), pl.ds(jnp.bitwise_and(p, SUBLANES - 1), 1)]
        return pltpu.make_async_copy(src, ybuf.at[s, grp, pl.ds(j, 1)], sem.at[s])

    def wait_slot(s):
        def wait(r, c):
            row_copy(0, 0, 0, s).wait()
            return c
        lax.fori_loop(0, tm, wait, 0, unroll=DMA_UNROLL)

    @pl.when(i == 0)
    def _():
        def start(grp, c):
            for j in range(SUBLANES):
                row_copy(grp, j, pos_ref[0, 0, grp * SUBLANES + j], 0).start()
            return c
        lax.fori_loop(0, tm // SUBLANES, start, 0)

    wait_slot(slot)

    n_stage = 1 if col_map is None else (len(col_map) + 1) // 2
    per_stage = -(-(tm // SUBLANES) // n_stage)

    def prefetch(stage):
        for grp in range(stage * per_stage, min((stage + 1) * per_stage, tm // SUBLANES)):
            for j in range(SUBLANES):
                row_copy(grp, j, posn_ref[0, 0, grp * SUBLANES + j], 1 - slot).start(priority=j % 2)

    x2 = x_ref[...] + ybuf[slot].reshape(tm, d)
    if col_map is None:
        prefetch(0)
    gate = _sigmoid(_dot(_rms(x2, g_ref[...]).astype(BF16), wg_ref[...]))
    x3 = x2 + gate * _dot(p_ref[...].astype(BF16), wp_ref[...])
    if col_map is None:
        o_ref[...] = _rms(x3, gn_ref[...])
    else:
        o_ref[...] = x3
        h = _rms(x3, gn_ref[...]).astype(BF16)
        for stage, (src, dst) in enumerate(col_map):
            prefetch(stage)
            proj_ref[:, dst:dst + COL_CHUNK] = _dot(h, win_ref[:, src:src + COL_CHUNK]).astype(BF16)

    @pl.when(i == nsteps - 1)
    def _():
        wait_slot(1 - slot)


def _ple(pos, x, p, ys, g, wg, wp, layer, *, g_next, next_layer, w_in=None, col_map=None):
    t, d = x.shape
    dp = p.shape[2]
    tm = PLE_TILE
    nsteps = t // tm
    pos3 = pos.reshape(nsteps, 1, tm)
    in_specs = [
        pl.BlockSpec((1, 1, tm), lambda i: (i, 0, 0), memory_space=pltpu.SMEM),
        pl.BlockSpec((1, 1, tm), lambda i: (jnp.minimum(i + 1, nsteps - 1), 0, 0),
                     memory_space=pltpu.SMEM),
        pl.BlockSpec((tm, d), lambda i: (i, 0)),
        pl.BlockSpec((None, tm, dp), lambda i: (layer, i, 0)),
        pl.BlockSpec(memory_space=pl.ANY),
        _layer_spec(g, layer), _layer_spec(wg, layer), _layer_spec(wp, layer),
        _layer_spec(g_next, next_layer),
    ]
    args = [pos3, pos3, x, p, ys.reshape(-1, SUBLANES, d), g, wg, wp, g_next]
    out_shape = jax.ShapeDtypeStruct((t, d), F32)
    out_specs = pl.BlockSpec((tm, d), lambda i: (i, 0))
    if col_map is not None:
        n = w_in.shape[2]
        in_specs.append(_layer_spec(w_in, next_layer))
        args.append(w_in)
        out_shape = (out_shape, jax.ShapeDtypeStruct((t, n), BF16))
        out_specs = (out_specs, pl.BlockSpec((tm, n), lambda i: (i, 0)))
    return pl.pallas_call(
        functools.partial(_ple_kernel, tm=tm, nsteps=nsteps, col_map=col_map),
        out_shape=out_shape,
        grid=(nsteps,),
        in_specs=in_specs,
        out_specs=out_specs,
        scratch_shapes=[pltpu.VMEM((2, tm // SUBLANES, SUBLANES, d), F32),
                        pltpu.SemaphoreType.DMA((2,))],
        compiler_params=_params("arbitrary"),
        name="gather_ple",
    )(*args)


def _block_diag(w):
    l, g, n, _ = w.shape
    eye = jnp.eye(g, dtype=w.dtype)
    return (eye[None, :, None, :, None] * w[:, :, :, None, :]).reshape(l, g * n, g * n)


def kernel(x, p, norm_mix, w_in, conv_w, conv_b, w_rg_a, b_rg_a, w_rg_x, b_rg_x, rg_lambda,
           w_attn_o, w_rnn_o, w_out, norm_moe, w_router_group, b_router_group, w_router_expert,
           b_router_expert, w_exp_gate, w_exp_up, w_exp_down, norm_ple, w_ple_gate, w_ple_proj,
           norm_final):
    bsz, seq, d = x.shape
    depth = w_in.shape[0]
    t = bsz * seq
    d_attn = N_HEADS * HEAD_DIM
    d_rnn = conv_w.shape[2]
    assert d_attn == d_rnn and d == 2 * d_attn and d_attn == COL_CHUNK
    assert t % ROW_TILE == 0 and seq % (2 * ATT_TILE) == 0 and seq % RNN_TS == 0
    assert t % RANK_TILE == 0
    assert t % SCATTER_TILE == 0 and t % PLE_TILE == 0

    n_in = w_in.shape[2]
    gates_at = 3 * d_attn + 2 * d_rnn
    col_map = tuple((src, (src - gates_at) % n_in) for src in range(0, n_in, COL_CHUNK))
    ga_blk, gr_blk = 0, 1
    q_col = 2 * d
    q_blk, k_blk, v_blk = (q_col // LANES, (q_col + d_attn) // LANES, (q_col + 2 * d_attn) // LANES)
    xr_blk, xg_blk = (q_col + 3 * d_attn) // d_rnn, (q_col + 3 * d_attn + d_rnn) // d_rnn

    tri_incl = jnp.tril(jnp.ones((ATT_TILE, ATT_TILE), F32)).astype(BF16)
    strict_lt = jnp.triu(jnp.ones((RANK_TILE, RANK_TILE), F32), k=1).astype(BF16)

    n_tiles = (t + N_CLASSES * (MOE_TILE - 1)) // MOE_TILE
    assert n_tiles <= LANES
    n_sorted = n_tiles * MOE_TILE

    vec = lambda v: v.reshape(depth, 1, -1)
    w_in_b = w_in.astype(BF16)
    wa_bd = _block_diag(w_rg_a).astype(BF16)
    wx_bd = _block_diag(w_rg_x).astype(BF16)
    w_r = jnp.concatenate([w_router_group, w_router_expert], axis=2)
    w_r = jnp.pad(w_r, ((0, 0), (0, 0), (0, CLASS_ROWS - w_r.shape[2])))
    w_r_hi = w_r.astype(BF16)
    w_r_lo = (w_r - w_r_hi.astype(F32)).astype(BF16)
    w_r_cat = jnp.pad(jnp.concatenate([w_r_hi, w_r_lo], axis=2),
                      ((0, 0), (0, 0), (0, LANES - 2 * CLASS_ROWS)))
    b_r = jnp.concatenate([b_router_group, b_router_expert], axis=1)
    b_r = jnp.pad(b_r, ((0, 0), (0, CLASS_ROWS - b_r.shape[1]))).reshape(depth, CLASS_ROWS, 1)
    wao_b, wro_b, wo_b = w_attn_o.astype(BF16), w_rnn_o.astype(BF16), w_out.astype(BF16)
    wpg_b, wpp_b = w_ple_gate.astype(BF16), w_ple_proj.astype(BF16)
    g_mix, g_moe, g_ple = vec(norm_mix), vec(norm_moe), vec(norm_ple)
    g_fin = norm_final.reshape(1, 1, -1)
    p3 = p.reshape(depth, t, -1)

    x2d = x.reshape(t, d)
    proj = _inproj(x2d, g_mix, w_in_b, 0, col_map)
    for i in range(depth):
        proj3 = proj.reshape(bsz, seq, -1)
        attn = _attention(proj3, tri_incl, q_blk=q_blk, k_blk=k_blk, v_blk=v_blk)
        rnn = _rglru(proj3, conv_w, vec(conv_b), wa_bd, vec(b_rg_a), wx_bd, vec(b_rg_x),
                     vec(rg_lambda), i, xr_blk=xr_blk, xg_blk=xg_blk)
        x1, routed, cls = _merge(
            x2d, attn.reshape(t, d_attn), rnn.reshape(t, d_rnn), proj, wao_b, wro_b, wo_b,
            g_moe, w_r_cat, b_r, i, ga_blk=ga_blk, gr_blk=gr_blk)
        pos, tinfo = _rank(cls, strict_lt)
        pos = pos.reshape(t)
        xs = _scatter_rows(tinfo, pos, routed, n_sorted)
        ys = _expert_ffn(tinfo, xs.reshape(n_sorted, -1), w_exp_gate, w_exp_up, w_exp_down, i)
        if i + 1 < depth:
            x2d, proj = _ple(pos, x1, p3, ys, g_ple, wpg_b, wpp_b, i, g_next=g_mix,
                             next_layer=i + 1, w_in=w_in_b, col_map=col_map)
        else:
            x2d = _ple(pos, x1, p3, ys, g_ple, wpg_b, wpp_b, i, g_next=g_fin, next_layer=0)
    return x2d.reshape(bsz, seq, d)
```

```python
import functools

import jax
import jax.numpy as jnp
from jax import lax
from jax.experimental import pallas as pl
from jax.experimental.pallas import tpu as pltpu

F32 = jnp.float32
BF16 = jnp.bfloat16
EPS = 1e-6

N_HEADS = 8
HEAD_DIM = 64
N_GROUPS = 4
EXPERTS_PER_GROUP = 4
N_EXPERTS = N_GROUPS * EXPERTS_PER_GROUP
PAIRS_PER_GROUP = 6
N_CLASSES = N_GROUPS * PAIRS_PER_GROUP
CLASS_ROWS = 32
CONV_WIDTH = 4
RG_C = 8.0

LANES = 128
SUBLANES = 8
VMEM_LIMIT = 56 * 1024 * 1024

ROW_TILE = 512
RANK_TILE = 1024
ATT_TILE = 256
RNN_TS = 256
MOE_TILE = 512
SCATTER_TILE = 1024
PLE_TILE = 512
ROUTE_LANES = 128
DMA_UNROLL = 8
COL_CHUNK = 512

_NT = (((1,), (1,)), ((), ()))
LOG2E = 1.4426950408889634
UNDERFLOW_EXPONENT = 151.0 / LOG2E


def _params(*sem):
    return pltpu.CompilerParams(dimension_semantics=sem, vmem_limit_bytes=VMEM_LIMIT)


def _layer_spec(arr, layer):
    nd = arr.ndim
    return pl.BlockSpec((None,) + arr.shape[1:], lambda *_: (layer,) + (0,) * (nd - 1))


def _rms(x, g):
    ms = jnp.mean(x * x, axis=-1, keepdims=True)
    return x * lax.rsqrt(ms + EPS) * g


def _sigmoid(x):
    return 1.0 / (1.0 + jnp.exp(-x))


def _dot(a, b):
    return jnp.dot(a, b, preferred_element_type=F32)


def _in_projection(x, g, w_ref, o_ref, col_map):
    h = _rms(x, g).astype(BF16)
    for src, dst in col_map:
        o_ref[:, dst:dst + COL_CHUNK] = _dot(h, w_ref[:, src:src + COL_CHUNK]).astype(BF16)


def _inproj_kernel(x_ref, g_ref, w_ref, o_ref, *, col_map):
    _in_projection(x_ref[...], g_ref[...], w_ref, o_ref, col_map)


def _inproj(x, g, w, layer, col_map):
    t, d = x.shape
    n = w.shape[2]
    return pl.pallas_call(
        functools.partial(_inproj_kernel, col_map=col_map),
        out_shape=jax.ShapeDtypeStruct((t, n), BF16),
        grid=(t // ROW_TILE,),
        in_specs=[
            pl.BlockSpec((ROW_TILE, d), lambda i: (i, 0)),
            _layer_spec(g, layer),
            _layer_spec(w, layer),
        ],
        out_specs=pl.BlockSpec((ROW_TILE, n), lambda i: (i, 0)),
        compiler_params=_params("arbitrary"),
        name="inproj",
    )(x, g, w)


def _attn_tile(q_heads, kb, vb, u, acc_ref, carry_ref, rows, causal):
    tk = kb.shape[0]
    lane = lax.broadcasted_iota(jnp.int32, (1, LANES), 1)
    first_head = lane < HEAD_DIM
    vz = jnp.zeros_like(vb)
    v_heads = (jnp.where(first_head, vb, vz), jnp.where(first_head, vz, vb))
    pv = None
    for h in range(2):
        z = lax.dot_general(q_heads[h], kb, _NT, preferred_element_type=F32)
        sp = jnp.maximum(z, 0.0) + jnp.log(1.0 + jnp.exp2(jnp.abs(z) * (-LOG2E)))
        if causal is not None:
            sp = jnp.where(causal, sp, 0.0)
        suffix = _dot(sp.astype(BF16), u)
        carry = carry_ref[h, rows, :]
        w = jnp.exp2((z - suffix - jnp.tile(carry, (1, tk // LANES))) * LOG2E)
        if causal is not None:
            w = jnp.where(causal, w, 0.0)
        d = _dot(w.astype(BF16), v_heads[h])
        pv = d if pv is None else pv + d
        carry_ref[h, rows, :] = carry + suffix[:, 0:1]
    acc_ref[rows, :] += pv


def _attn_kernel(q_ref, k_ref, v_ref, u_ref, o_ref, acc_ref, carry_ref, *, tk):
    s = q_ref.shape[0]
    nb = s // tk
    lane = lax.broadcasted_iota(jnp.int32, (1, LANES), 1)
    first_head = lane < HEAD_DIM
    u = u_ref[...]
    causal = (lax.broadcasted_iota(jnp.int32, (2 * tk, tk), 1)
              < lax.broadcasted_iota(jnp.int32, (2 * tk, tk), 0))
    acc_ref[...] = jnp.zeros_like(acc_ref)
    carry_ref[...] = jnp.zeros_like(carry_ref)

    def tile(r0, nrows, kb, mask):
        rows = pl.ds(pl.multiple_of(r0, tk), nrows)
        ks = pl.multiple_of(kb * tk, tk)
        q = q_ref[rows, :] * jnp.asarray(HEAD_DIM ** -0.5, BF16)
        qz = jnp.zeros_like(q)
        q_heads = (jnp.where(first_head, q, qz), jnp.where(first_head, qz, q))
        _attn_tile(q_heads, k_ref[pl.ds(ks, tk), :], v_ref[pl.ds(ks, tk), :],
                   u, acc_ref, carry_ref, rows, mask)

    tile((nb - 1) * tk, tk, nb - 1, causal[:tk])

    def key_block(i, c):
        kb = nb - 2 - i
        tile(kb * tk, 2 * tk, kb, causal)
        return c

    lax.fori_loop(0, nb - 1, key_block, 0)

    def query_block(j, c):
        block_rows = pl.ds(pl.multiple_of(j * tk, tk), tk)

        def more(state):
            kb, min_carry = state
            return jnp.logical_and(kb >= 0, min_carry < UNDERFLOW_EXPONENT)

        def body(state):
            kb, _ = state
            tile(j * tk, tk, kb, None)
            return kb - 1, jnp.min(carry_ref[:, block_rows, :])

        lax.while_loop(more, body, (j - 2, jnp.min(carry_ref[:, block_rows, :])))
        return c

    lax.fori_loop(2, nb, query_block, 0)
    o_ref[...] = acc_ref[...].astype(BF16)


def _attention(proj, u, *, q_blk, k_blk, v_blk):
    b, s, _ = proj.shape
    hp = N_HEADS * HEAD_DIM // LANES
    cols = lambda blk: pl.BlockSpec((None, s, LANES), lambda bi, hi: (bi, 0, blk + hi))
    return pl.pallas_call(
        functools.partial(_attn_kernel, tk=ATT_TILE),
        out_shape=jax.ShapeDtypeStruct((b, s, N_HEADS * HEAD_DIM), BF16),
        grid=(b, hp),
        in_specs=[cols(q_blk), cols(k_blk), cols(v_blk),
                  pl.BlockSpec((ATT_TILE, ATT_TILE), lambda bi, hi: (0, 0))],
        out_specs=pl.BlockSpec((None, s, LANES), lambda bi, hi: (bi, 0, hi)),
        scratch_shapes=[
            pltpu.VMEM((s, LANES), F32),
            pltpu.VMEM((2, s, LANES), F32),
        ],
        compiler_params=_params("arbitrary", "arbitrary"),
        name="sb_attention",
    )(proj, proj, proj, u)


def _rglru_kernel(xr_ref, xg_ref, cw_ref, cb_ref, wa_ref, ba_ref, wx_ref, bx_ref, lam_ref,
                  o_ref, tail_ref, h_ref, a_scr, b_scr, *, ts):
    @pl.when(pl.program_id(1) == 0)
    def _():
        tail_ref[...] = jnp.zeros_like(tail_ref)
        h_ref[...] = jnp.zeros_like(h_ref)

    xr = xr_ref[...].astype(F32)
    ext = jnp.concatenate([tail_ref[...], xr], axis=0)
    tail_ref[...] = xr[ts - SUBLANES:, :]
    xc = cb_ref[...]
    for j in range(CONV_WIDTH):
        lo = SUBLANES - (CONV_WIDTH - 1) + j
        xc = xc + cw_ref[j:j + 1, :] * ext[lo:lo + ts, :]
    xcb = xc.astype(BF16)
    r = 0.5 + 0.5 * jnp.tanh(0.5 * (_dot(xcb, wa_ref[...]) + ba_ref[...]))
    ig = 0.5 + 0.5 * jnp.tanh(0.5 * (_dot(xcb, wx_ref[...]) + bx_ref[...]))
    lam = lam_ref[...]
    softplus_neg_lam = jnp.maximum(-lam, 0.0) + jnp.log(1.0 + jnp.exp(-jnp.abs(lam)))
    log_a = (-RG_C * softplus_neg_lam) * r
    a = jnp.exp(log_a)
    b = jnp.sqrt(1.0 - a * a) * (ig * xc)

    in_group = jnp.bitwise_and(lax.broadcasted_iota(jnp.int32, a.shape, 0), SUBLANES - 1)
    for s in (1, 2, 4):
        has_prev = in_group >= s
        b = jnp.where(has_prev, a * pltpu.roll(b, s, 0) + b, b)
        a = jnp.where(has_prev, a * pltpu.roll(a, s, 0), a)
    a_scr[...] = a
    b_scr[...] = b

    def group(gidx, h):
        r0 = pl.multiple_of(gidx * SUBLANES, SUBLANES)
        hg = a_scr[pl.ds(r0, SUBLANES), :] * h + b_scr[pl.ds(r0, SUBLANES), :]
        b_scr[pl.ds(r0, SUBLANES), :] = hg
        return hg[SUBLANES - 1:, :]

    h_ref[...] = lax.fori_loop(0, ts // SUBLANES, group, h_ref[...], unroll=4)
    xg = xg_ref[...].astype(F32)
    gelu = 0.5 * xg * (1.0 + jnp.tanh(0.7978845608028654 * (xg + 0.044715 * (xg * xg * xg))))
    o_ref[...] = (b_scr[...] * gelu).astype(BF16)


def _rglru(proj, cw, cb, wa, ba, wx, bx, lam, layer, *, xr_blk, xg_blk):
    b, s, _ = proj.shape
    dr = cw.shape[2]
    return pl.pallas_call(
        functools.partial(_rglru_kernel, ts=RNN_TS),
        out_shape=jax.ShapeDtypeStruct((b, s, dr), BF16),
        grid=(b, s // RNN_TS),
        in_specs=[
            pl.BlockSpec((None, RNN_TS, dr), lambda bi, si: (bi, si, xr_blk)),
            pl.BlockSpec((None, RNN_TS, dr), lambda bi, si: (bi, si, xg_blk)),
            _layer_spec(cw, layer), _layer_spec(cb, layer),
            _layer_spec(wa, layer), _layer_spec(ba, layer),
            _layer_spec(wx, layer), _layer_spec(bx, layer),
            _layer_spec(lam, layer),
        ],
        out_specs=pl.BlockSpec((None, RNN_TS, dr), lambda bi, si: (bi, si, 0)),
        scratch_shapes=[
            pltpu.VMEM((SUBLANES, dr), F32),
            pltpu.VMEM((1, dr), F32),
            pltpu.VMEM((RNN_TS, dr), F32),
            pltpu.VMEM((RNN_TS, dr), F32),
        ],
        compiler_params=_params("arbitrary", "arbitrary"),
        name="rglru",
    )(proj, proj, cw, cb, wa, ba, wx, bx, lam)


def _first_argmax(vals):
    m = vals[0]
    for v in vals[1:]:
        m = jnp.maximum(m, v)
    idx = jnp.full_like(m, float(len(vals) - 1))
    for k in range(len(vals) - 2, -1, -1):
        idx = jnp.where(vals[k] == m, float(k), idx)
    return m, idx


def _merge_kernel(x_ref, at_ref, rn_ref, ga_ref, gr_ref, wao_ref, wro_ref, wo_ref, g_ref,
                  wr_ref, br_ref, x1_ref, hr_ref, cls_ref):
    d = x_ref.shape[1]
    ya = _dot(at_ref[...], wao_ref[...])
    yr = _dot(rn_ref[...], wro_ref[...])
    merged = _sigmoid(ga_ref[...].astype(F32)) * ya + _sigmoid(gr_ref[...].astype(F32)) * yr
    x1 = x_ref[...] + _dot(merged.astype(BF16), wo_ref[...])
    x1_ref[...] = x1
    h = _rms(x1, g_ref[...])
    hr_ref[:, :d] = h

    h_hi = h.astype(BF16)
    h_lo = (h - h_hi.astype(F32)).astype(BF16)
    wr = wr_ref[...]
    la = _dot(h_hi, wr)
    lb = _dot(h_lo, wr)
    logits = la + pltpu.roll(la, LANES - CLASS_ROWS, 1) + lb
    lt = logits.T[:CLASS_ROWS, :] + br_ref[...]
    g = [lt[k:k + 1, :] for k in range(N_GROUPS)]
    e = [lt[N_GROUPS + k:N_GROUPS + k + 1, :] for k in range(N_EXPERTS)]

    gmax, gi = _first_argmax(g)
    den = jnp.exp(g[0] - gmax)
    for k in range(1, N_GROUPS):
        den = den + jnp.exp(g[k] - gmax)
    gate = 1.0 / den
    sel = []
    for j in range(EXPERTS_PER_GROUP):
        s = e[(N_GROUPS - 1) * EXPERTS_PER_GROUP + j]
        for k in range(N_GROUPS - 2, -1, -1):
            s = jnp.where(gi == float(k), e[k * EXPERTS_PER_GROUP + j], s)
        sel.append(s)
    m1, i1 = _first_argmax(sel)
    sel2 = [jnp.where(i1 == float(j), -jnp.inf, sel[j]) for j in range(EXPERTS_PER_GROUP)]
    m2, i2 = _first_argmax(sel2)
    t = jnp.exp(m2 - m1)
    w1 = gate * (1.0 / (1.0 + t))
    w2 = gate * (t / (1.0 + t))
    first_is_lo = i1 < i2
    a = jnp.minimum(i1, i2)
    b = jnp.maximum(i1, i2)
    w_lo = jnp.where(first_is_lo, w1, w2)
    w_hi = jnp.where(first_is_lo, w2, w1)
    pair = jnp.where(a == 0.0, b - 1.0, jnp.where(a == 1.0, b + 1.0, 5.0))
    cls = gi * float(PAIRS_PER_GROUP) + pair

    n = cls.shape[1]
    cls_ref[...] = jnp.concatenate([cls, jnp.zeros((SUBLANES - 1, n), F32)], axis=0)
    rows = jnp.concatenate([w_lo, w_hi, jnp.zeros((ROUTE_LANES - 2, n), F32)], axis=0)
    hr_ref[:, d:] = rows.T


def _merge(x, attn, rnn, proj, wao, wro, wo, g, wr, br, layer, *, ga_blk, gr_blk):
    t, d = x.shape
    da = attn.shape[1]
    dr = rnn.shape[1]
    tm = ROW_TILE
    return pl.pallas_call(
        _merge_kernel,
        out_shape=(
            jax.ShapeDtypeStruct((t, d), F32),
            jax.ShapeDtypeStruct((t, d + ROUTE_LANES), F32),
            jax.ShapeDtypeStruct((SUBLANES, t), F32),
        ),
        grid=(t // tm,),
        in_specs=[
            pl.BlockSpec((tm, d), lambda i: (i, 0)),
            pl.BlockSpec((tm, da), lambda i: (i, 0)),
            pl.BlockSpec((tm, dr), lambda i: (i, 0)),
            pl.BlockSpec((tm, d), lambda i: (i, ga_blk)),
            pl.BlockSpec((tm, d), lambda i: (i, gr_blk)),
            _layer_spec(wao, layer), _layer_spec(wro, layer), _layer_spec(wo, layer),
            _layer_spec(g, layer), _layer_spec(wr, layer), _layer_spec(br, layer),
        ],
        out_specs=(
            pl.BlockSpec((tm, d), lambda i: (i, 0)),
            pl.BlockSpec((tm, d + ROUTE_LANES), lambda i: (i, 0)),
            pl.BlockSpec((SUBLANES, tm), lambda i: (0, i)),
        ),
        compiler_params=_params("arbitrary"),
        name="merge_router",
    )(x, attn, rnn, proj, proj, wao, wro, wo, g, wr, br)


def _rank_kernel(cls_ref, lt_ref, pos_ref, tinfo_ref, cnt_ref, off_ref, run_ref, *, tm):
    ph = pl.program_id(0)
    bi = pl.program_id(1)
    cls = cls_ref[0:1, :]
    crow = lax.broadcasted_iota(jnp.int32, (CLASS_ROWS, tm), 0).astype(F32)
    member = crow == cls
    onehot = jnp.where(member, 1.0, 0.0)
    block_count = jnp.sum(onehot, axis=1, keepdims=True)

    @pl.when(jnp.logical_and(ph == 0, bi == 0))
    def _():
        cnt_ref[...] = jnp.zeros_like(cnt_ref)

    @pl.when(ph == 0)
    def _():
        cnt_ref[...] += block_count

    @pl.when(jnp.logical_and(ph == 1, bi == 0))
    def _():
        ntile = jnp.floor((cnt_ref[...] + float(MOE_TILE - 1)) * (1.0 / MOE_TILE))
        rid = lax.broadcasted_iota(jnp.int32, (CLASS_ROWS, LANES), 0)
        toff = jnp.zeros((CLASS_ROWS, LANES), F32)
        for c in range(1, CLASS_ROWS):
            toff = toff + jnp.where(rid >= c, ntile[c - 1:c, :], 0.0)
        off_ref[...] = toff * float(MOE_TILE)
        run_ref[...] = jnp.zeros_like(run_ref)
        ti = lax.broadcasted_iota(jnp.int32, (CLASS_ROWS, LANES), 1).astype(F32)
        inside = jnp.where(ti >= toff, jnp.where(ti < toff + ntile, 1.0, 0.0), 0.0)
        c = rid.astype(F32)
        grp = (jnp.where(c >= 6.0, 1.0, 0.0) + jnp.where(c >= 12.0, 1.0, 0.0)
               + jnp.where(c >= 18.0, 1.0, 0.0))
        pair = c - float(PAIRS_PER_GROUP) * grp
        a = jnp.where(pair >= 3.0, 1.0, 0.0) + jnp.where(pair >= 5.0, 1.0, 0.0)
        b = jnp.where(pair < 3.0, pair + 1.0, jnp.where(pair < 5.0, pair - 1.0, 3.0))
        e_lo = jnp.sum(inside * (float(EXPERTS_PER_GROUP) * grp + a), axis=0, keepdims=True)
        e_hi = jnp.sum(inside * (float(EXPERTS_PER_GROUP) * grp + b), axis=0, keepdims=True)
        first = (ti - toff) == 0.0
        tile_c = toff + jnp.where(first, ntile - 1.0, ti - toff - 1.0)
        rows_c = jnp.where(first, cnt_ref[...] - (ntile - 1.0) * float(MOE_TILE), float(MOE_TILE))
        used = jnp.sum(ntile, axis=0, keepdims=True)
        tile = jnp.sum(inside * tile_c, axis=0, keepdims=True)
        tile = tile + jnp.where(ti[0:1, :] >= used, ti[0:1, :], 0.0)
        rows = jnp.sum(inside * rows_c, axis=0, keepdims=True)
        left = cnt_ref[...] - (ti - toff) * float(MOE_TILE)
        tile_rows = jnp.sum(inside * jnp.minimum(left, float(MOE_TILE)), axis=0, keepdims=True)
        info = jnp.concatenate([e_lo, e_hi, rows, used, tile, tile_rows,
                                jnp.zeros((SUBLANES - 6, LANES), F32)], axis=0)
        tinfo_ref[...] = info.astype(jnp.int32)

    @pl.when(ph == 1)
    def _():
        before = _dot(onehot.astype(BF16), lt_ref[...])
        val = before + run_ref[:, 0:1] + off_ref[:, 0:1]
        pos = jnp.sum(jnp.where(member, val, 0.0), axis=0, keepdims=True)
        pos_ref[...] = pos.astype(jnp.int32)
        run_ref[...] += block_count


def _rank(cls, strict_lt):
    t = cls.shape[1]
    tm = RANK_TILE
    return pl.pallas_call(
        functools.partial(_rank_kernel, tm=tm),
        out_shape=(
            jax.ShapeDtypeStruct((1, t), jnp.int32),
            jax.ShapeDtypeStruct((SUBLANES, LANES), jnp.int32),
        ),
        grid=(2, t // tm),
        in_specs=[
            pl.BlockSpec((SUBLANES, tm), lambda ph, bi: (0, bi)),
            pl.BlockSpec((tm, tm), lambda ph, bi: (0, 0)),
        ],
        out_specs=(
            pl.BlockSpec((1, tm), lambda ph, bi: (0, bi * ph)),
            pl.BlockSpec((SUBLANES, LANES), lambda ph, bi: (0, 0)),
        ),
        scratch_shapes=[pltpu.VMEM((CLASS_ROWS, LANES), F32)] * 3,
        compiler_params=_params("arbitrary", "arbitrary"),
        name="rank_tokens",
    )(cls, strict_lt)


def _scatter_kernel(info_ref, pos_ref, h_ref, o_ref, zero_ref, sem, zsem, *, tm):
    tile_groups = MOE_TILE // SUBLANES

    @pl.when(pl.program_id(0) == 0)
    def _():
        zero_ref[...] = jnp.zeros_like(zero_ref)

        def tile_clear(ti):
            dst = o_ref.at[pl.ds(ti * tile_groups, tile_groups)]
            return pltpu.make_async_copy(zero_ref, dst, zsem)

        n_tiles = o_ref.shape[0] // tile_groups
        for ti in range(n_tiles):
            @pl.when(info_ref[5, ti] < MOE_TILE)
            def _():
                tile_clear(ti).start()
        for ti in range(n_tiles):
            @pl.when(info_ref[5, ti] < MOE_TILE)
            def _():
                tile_clear(ti).wait()

    def row_copy(grp, j, p):
        dst = o_ref.at[lax.shift_right_logical(p, 3), pl.ds(jnp.bitwise_and(p, SUBLANES - 1), 1)]
        return pltpu.make_async_copy(h_ref.at[grp, pl.ds(j, 1)], dst, sem)

    def start(grp, c):
        for j in range(SUBLANES):
            row_copy(grp, j, pos_ref[0, 0, grp * SUBLANES + j]).start(priority=j % 2)
        return c

    def wait(r, c):
        row_copy(0, 0, 0).wait()
        return c

    lax.fori_loop(0, tm // SUBLANES, start, 0)
    lax.fori_loop(0, tm, wait, 0, unroll=DMA_UNROLL)


def _scatter_rows(tinfo, pos, rows, n_sorted):
    t, w = rows.shape
    tm = SCATTER_TILE
    return pl.pallas_call(
        functools.partial(_scatter_kernel, tm=tm),
        out_shape=jax.ShapeDtypeStruct((n_sorted // SUBLANES, SUBLANES, w), rows.dtype),
        grid_spec=pltpu.PrefetchScalarGridSpec(
            num_scalar_prefetch=1,
            grid=(t // tm,),
            in_specs=[
                pl.BlockSpec((1, 1, tm), lambda i, info: (i, 0, 0), memory_space=pltpu.SMEM),
                pl.BlockSpec((tm // SUBLANES, SUBLANES, w), lambda i, info: (i, 0, 0)),
            ],
            out_specs=pl.BlockSpec(memory_space=pl.ANY),
            scratch_shapes=[
                pltpu.VMEM((MOE_TILE // SUBLANES, SUBLANES, w), rows.dtype),
                pltpu.SemaphoreType.DMA(()),
                pltpu.SemaphoreType.DMA(()),
            ],
        ),
        compiler_params=_params("arbitrary"),
        name="scatter_rows",
    )(tinfo, pos.reshape(t // tm, 1, tm), rows.reshape(t // SUBLANES, SUBLANES, w))


def _ffn_kernel(info_ref, x_ref, wg_hbm, wu_hbm, wd_hbm, o_ref,
                wg_s, wu_s, wd_s, wg_f, wu_f, wd_f, sem, *, layer):
    i = pl.program_id(0)
    tm, d = o_ref.shape
    half = tm // 2
    used = info_ref[3, 0]
    live = i < used
    prev = jnp.maximum(i - 1, 0)
    nxt = jnp.minimum(i + 1, info_ref.shape[1] - 1)

    def fetch(k, e):
        return (pltpu.make_async_copy(wg_hbm.at[layer, e], wg_f.at[k], sem.at[k, 0]),
                pltpu.make_async_copy(wu_hbm.at[layer, e], wu_f.at[k], sem.at[k, 1]),
                pltpu.make_async_copy(wd_hbm.at[layer, e], wd_f.at[k], sem.at[k, 2]))

    for k in range(2):
        @pl.when(i == 0)
        def _():
            for c in fetch(k, info_ref[k, 0]):
                c.start(priority=1)

        changed = jnp.logical_or(i == 0, info_ref[k, i] != info_ref[k, prev])

        @pl.when(jnp.logical_and(live, changed))
        def _():
            for c in fetch(k, info_ref[k, i]):
                c.wait()
            wg_s[k] = wg_f[k].astype(BF16)
            wu_s[k] = wu_f[k].astype(BF16)
            wd_s[k] = wd_f[k].astype(BF16)

    for k in range(2):
        @pl.when(jnp.logical_and(i + 1 < used, info_ref[k, nxt] != info_ref[k, i]))
        def _():
            for c in fetch(k, info_ref[k, nxt]):
                c.start(priority=1)

    def ffn(rows):
        xt = x_ref[rows, :]
        xb = xt[:, :d].astype(BF16)
        y = None
        for k in range(2):
            gte = _dot(xb, wg_s[k])
            up = _dot(xb, wu_s[k])
            he = (gte * _sigmoid(gte)) * up
            part = xt[:, d + k:d + k + 1] * _dot(he.astype(BF16), wd_s[k])
            y = part if y is None else y + part
        o_ref[rows, :] = y

    @pl.when(jnp.logical_not(live))
    def _():
        o_ref[...] = jnp.zeros_like(o_ref)

    @pl.when(jnp.logical_and(live, info_ref[2, i] > half))
    def _():
        ffn(slice(0, tm))

    @pl.when(jnp.logical_and(live, info_ref[2, i] <= half))
    def _():
        ffn(slice(0, half))
        o_ref[half:, :] = jnp.zeros((tm - half, d), F32)


def _expert_ffn(tinfo, xs, wg, wu, wd, layer):
    npad, w = xs.shape
    _, _, d, de = wg.shape
    tm = MOE_TILE

    def step(i, info):
        return jnp.minimum(i, info[3, 0] - 1)

    hbm = pl.BlockSpec(memory_space=pl.ANY)
    return pl.pallas_call(
        functools.partial(_ffn_kernel, layer=layer),
        out_shape=jax.ShapeDtypeStruct((npad, d), F32),
        grid_spec=pltpu.PrefetchScalarGridSpec(
            num_scalar_prefetch=1,
            grid=(npad // tm,),
            in_specs=[
                pl.BlockSpec((tm, w), lambda i, info: (info[4, step(i, info)], 0)),
                hbm, hbm, hbm,
            ],
            out_specs=pl.BlockSpec((tm, d), lambda i, info: (info[4, i], 0)),
            scratch_shapes=[
                pltpu.VMEM((2, d, de), BF16),
                pltpu.VMEM((2, d, de), BF16),
                pltpu.VMEM((2, de, d), BF16),
                pltpu.VMEM((2, d, de), F32),
                pltpu.VMEM((2, d, de), F32),
                pltpu.VMEM((2, de, d), F32),
                pltpu.SemaphoreType.DMA((2, 3)),
            ],
        ),
        compiler_params=_params("arbitrary"),
        name="expert_ffn",
    )(tinfo, xs, wg, wu, wd)


def _ple_kernel(pos_ref, posn_ref, x_ref, p_ref, ys_ref, g_ref, wg_ref, wp_ref, gn_ref, *rest,
                tm, nsteps, col_map):
    if col_map is None:
        o_ref, ybuf, sem = rest
    else:
        win_ref, o_ref, proj_ref, ybuf, sem = rest
    i = pl.program_id(0)
    slot = lax.rem(i, 2)
    d = x_ref.shape[1]

    def row_copy(grp, j, p, s):
        src = ys_ref.at[lax.shift_right_logical(p, 3), pl.ds(jnp.bitwise_and(p, SUBLANES - 1), 1)]
        return pltpu.make_async_copy(src, ybuf.at[s, grp, pl.ds(j, 1)], sem.at[s])

    def wait_slot(s):
        def wait(r, c):
            row_copy(0, 0, 0, s).wait()
            return c
        lax.fori_loop(0, tm, wait, 0, unroll=DMA_UNROLL)

    @pl.when(i == 0)
    def _():
        def start(grp, c):
            for j in range(SUBLANES):
                row_copy(grp, j, pos_ref[0, 0, grp * SUBLANES + j], 0).start()
            return c
        lax.fori_loop(0, tm // SUBLANES, start, 0)

    wait_slot(slot)

    n_stage = 1 if col_map is None else (len(col_map) + 1) // 2
    per_stage = -(-(tm // SUBLANES) // n_stage)

    def prefetch(stage):
        for grp in range(stage * per_stage, min((stage + 1) * per_stage, tm // SUBLANES)):
            for j in range(SUBLANES):
                row_copy(grp, j, posn_ref[0, 0, grp * SUBLANES + j], 1 - slot).start(priority=1)

    x2 = x_ref[...] + ybuf[slot].reshape(tm, d)
    if col_map is None:
        prefetch(0)
    gate = _sigmoid(_dot(_rms(x2, g_ref[...]).astype(BF16), wg_ref[...]))
    x3 = x2 + gate * _dot(p_ref[...].astype(BF16), wp_ref[...])
    if col_map is None:
        o_ref[...] = _rms(x3, gn_ref[...])
    else:
        o_ref[...] = x3
        h = _rms(x3, gn_ref[...]).astype(BF16)
        for stage, (src, dst) in enumerate(col_map):
            prefetch(stage)
            proj_ref[:, dst:dst + COL_CHUNK] = _dot(h, win_ref[:, src:src + COL_CHUNK]).astype(BF16)

    @pl.when(i == nsteps - 1)
    def _():
        wait_slot(1 - slot)


def _ple(pos, x, p, ys, g, wg, wp, layer, *, g_next, next_layer, w_in=None, col_map=None):
    t, d = x.shape
    dp = p.shape[2]
    tm = PLE_TILE
    nsteps = t // tm
    pos3 = pos.reshape(nsteps, 1, tm)
    in_specs = [
        pl.BlockSpec((1, 1, tm), lambda i: (i, 0, 0), memory_space=pltpu.SMEM),
        pl.BlockSpec((1, 1, tm), lambda i: (jnp.minimum(i + 1, nsteps - 1), 0, 0),
                     memory_space=pltpu.SMEM),
        pl.BlockSpec((tm, d), lambda i: (i, 0)),
        pl.BlockSpec((None, tm, dp), lambda i: (layer, i, 0)),
        pl.BlockSpec(memory_space=pl.ANY),
        _layer_spec(g, layer), _layer_spec(wg, layer), _layer_spec(wp, layer),
        _layer_spec(g_next, next_layer),
    ]
    args = [pos3, pos3, x, p, ys.reshape(-1, SUBLANES, d), g, wg, wp, g_next]
    out_shape = jax.ShapeDtypeStruct((t, d), F32)
    out_specs = pl.BlockSpec((tm, d), lambda i: (i, 0))
    if col_map is not None:
        n = w_in.shape[2]
        in_specs.append(_layer_spec(w_in, next_layer))
        args.append(w_in)
        out_shape = (out_shape, jax.ShapeDtypeStruct((t, n), BF16))
        out_specs = (out_specs, pl.BlockSpec((tm, n), lambda i: (i, 0)))
    return pl.pallas_call(
        functools.partial(_ple_kernel, tm=tm, nsteps=nsteps, col_map=col_map),
        out_shape=out_shape,
        grid=(nsteps,),
        in_specs=in_specs,
        out_specs=out_specs,
        scratch_shapes=[pltpu.VMEM((2, tm // SUBLANES, SUBLANES, d), F32),
                        pltpu.SemaphoreType.DMA((2,))],
        compiler_params=_params("arbitrary"),
        name="gather_ple",
    )(*args)


def _block_diag(w):
    l, g, n, _ = w.shape
    eye = jnp.eye(g, dtype=w.dtype)
    return (eye[None, :, None, :, None] * w[:, :, :, None, :]).reshape(l, g * n, g * n)


def kernel(x, p, norm_mix, w_in, conv_w, conv_b, w_rg_a, b_rg_a, w_rg_x, b_rg_x, rg_lambda,
           w_attn_o, w_rnn_o, w_out, norm_moe, w_router_group, b_router_group, w_router_expert,
           b_router_expert, w_exp_gate, w_exp_up, w_exp_down, norm_ple, w_ple_gate, w_ple_proj,
           norm_final):
    bsz, seq, d = x.shape
    depth = w_in.shape[0]
    t = bsz * seq
    d_attn = N_HEADS * HEAD_DIM
    d_rnn = conv_w.shape[2]
    assert d_attn == d_rnn and d == 2 * d_attn and d_attn == COL_CHUNK
    assert t % ROW_TILE == 0 and seq % (2 * ATT_TILE) == 0 and seq % RNN_TS == 0
    assert t % RANK_TILE == 0
    assert t % SCATTER_TILE == 0 and t % PLE_TILE == 0

    n_in = w_in.shape[2]
    gates_at = 3 * d_attn + 2 * d_rnn
    col_map = tuple((src, (src - gates_at) % n_in) for src in range(0, n_in, COL_CHUNK))
    ga_blk, gr_blk = 0, 1
    q_col = 2 * d
    q_blk, k_blk, v_blk = (q_col // LANES, (q_col + d_attn) // LANES, (q_col + 2 * d_attn) // LANES)
    xr_blk, xg_blk = (q_col + 3 * d_attn) // d_rnn, (q_col + 3 * d_attn + d_rnn) // d_rnn

    tri_incl = jnp.tril(jnp.ones((ATT_TILE, ATT_TILE), F32)).astype(BF16)
    strict_lt = jnp.triu(jnp.ones((RANK_TILE, RANK_TILE), F32), k=1).astype(BF16)

    n_tiles = (t + N_CLASSES * (MOE_TILE - 1)) // MOE_TILE
    assert n_tiles <= LANES
    n_sorted = n_tiles * MOE_TILE

    vec = lambda v: v.reshape(depth, 1, -1)
    w_in_b = w_in.astype(BF16)
    wa_bd = _block_diag(w_rg_a).astype(BF16)
    wx_bd = _block_diag(w_rg_x).astype(BF16)
    w_r = jnp.concatenate([w_router_group, w_router_expert], axis=2)
    w_r = jnp.pad(w_r, ((0, 0), (0, 0), (0, CLASS_ROWS - w_r.shape[2])))
    w_r_hi = w_r.astype(BF16)
    w_r_lo = (w_r - w_r_hi.astype(F32)).astype(BF16)
    w_r_cat = jnp.pad(jnp.concatenate([w_r_hi, w_r_lo], axis=2),
                      ((0, 0), (0, 0), (0, LANES - 2 * CLASS_ROWS)))
    b_r = jnp.concatenate([b_router_group, b_router_expert], axis=1)
    b_r = jnp.pad(b_r, ((0, 0), (0, CLASS_ROWS - b_r.shape[1]))).reshape(depth, CLASS_ROWS, 1)
    wao_b, wro_b, wo_b = w_attn_o.astype(BF16), w_rnn_o.astype(BF16), w_out.astype(BF16)
    wpg_b, wpp_b = w_ple_gate.astype(BF16), w_ple_proj.astype(BF16)
    g_mix, g_moe, g_ple = vec(norm_mix), vec(norm_moe), vec(norm_ple)
    g_fin = norm_final.reshape(1, 1, -1)
    p3 = p.reshape(depth, t, -1)

    x2d = x.reshape(t, d)
    proj = _inproj(x2d, g_mix, w_in_b, 0, col_map)
    for i in range(depth):
        proj3 = proj.reshape(bsz, seq, -1)
        attn = _attention(proj3, tri_incl, q_blk=q_blk, k_blk=k_blk, v_blk=v_blk)
        rnn = _rglru(proj3, conv_w, vec(conv_b), wa_bd, vec(b_rg_a), wx_bd, vec(b_rg_x),
                     vec(rg_lambda), i, xr_blk=xr_blk, xg_blk=xg_blk)
        x1, routed, cls = _merge(
            x2d, attn.reshape(t, d_attn), rnn.reshape(t, d_rnn), proj, wao_b, wro_b, wo_b,
            g_moe, w_r_cat, b_r, i, ga_blk=ga_blk, gr_blk=gr_blk)
        pos, tinfo = _rank(cls, strict_lt)
        pos = pos.reshape(t)
        xs = _scatter_rows(tinfo, pos, routed, n_sorted)
        ys = _expert_ffn(tinfo, xs.reshape(n_sorted, -1), w_exp_gate, w_exp_up, w_exp_down, i)
        if i + 1 < depth:
            x2d, proj = _ple(pos, x1, p3, ys, g_ple, wpg_b, wpp_b, i, g_next=g_mix,
                             next_layer=i + 1, w_in=w_in_b, col_map=col_map)
        else:
            x2d = _ple(pos, x1, p3, ys, g_ple, wpg_b, wpp_b, i, g_next=g_fin, next_layer=0)
    return x2d.reshape(bsz, seq, d)
```

```python
import functools

import jax
import jax.numpy as jnp
from jax import lax
from jax.experimental import pallas as pl
from jax.experimental.pallas import tpu as pltpu

F32 = jnp.float32
BF16 = jnp.bfloat16
EPS = 1e-6

N_HEADS = 8
HEAD_DIM = 64
N_GROUPS = 4
EXPERTS_PER_GROUP = 4
N_EXPERTS = N_GROUPS * EXPERTS_PER_GROUP
PAIRS_PER_GROUP = 6
N_CLASSES = N_GROUPS * PAIRS_PER_GROUP
CLASS_ROWS = 32
CONV_WIDTH = 4
RG_C = 8.0

LANES = 128
SUBLANES = 8
VMEM_LIMIT = 56 * 1024 * 1024

ROW_TILE = 512
RANK_TILE = 1024
ATT_TILE = 256
RNN_TS = 512
MOE_TILE = 512
SCATTER_TILE = 1024
PLE_TILE = 512
ROUTE_LANES = 128
DMA_UNROLL = 8
COL_CHUNK = 512

_NT = (((1,), (1,)), ((), ()))
LOG2E = 1.4426950408889634
UNDERFLOW_EXPONENT = 151.0 / LOG2E


def _params(*sem):
    return pltpu.CompilerParams(dimension_semantics=sem, vmem_limit_bytes=VMEM_LIMIT)


def _layer_spec(arr, layer):
    nd = arr.ndim
    return pl.BlockSpec((None,) + arr.shape[1:], lambda *_: (layer,) + (0,) * (nd - 1))


def _rms(x, g):
    ms = jnp.mean(x * x, axis=-1, keepdims=True)
    return x * lax.rsqrt(ms + EPS) * g


def _sigmoid(x):
    return 1.0 / (1.0 + jnp.exp(-x))


def _dot(a, b):
    return jnp.dot(a, b, preferred_element_type=F32)


def _in_projection(x, g, w_ref, o_ref, col_map):
    h = _rms(x, g).astype(BF16)
    for src, dst in col_map:
        o_ref[:, dst:dst + COL_CHUNK] = _dot(h, w_ref[:, src:src + COL_CHUNK]).astype(BF16)


def _inproj_kernel(x_ref, g_ref, w_ref, o_ref, *, col_map):
    _in_projection(x_ref[...], g_ref[...], w_ref, o_ref, col_map)


def _inproj(x, g, w, layer, col_map):
    t, d = x.shape
    n = w.shape[2]
    return pl.pallas_call(
        functools.partial(_inproj_kernel, col_map=col_map),
        out_shape=jax.ShapeDtypeStruct((t, n), BF16),
        grid=(t // ROW_TILE,),
        in_specs=[
            pl.BlockSpec((ROW_TILE, d), lambda i: (i, 0)),
            _layer_spec(g, layer),
            _layer_spec(w, layer),
        ],
        out_specs=pl.BlockSpec((ROW_TILE, n), lambda i: (i, 0)),
        compiler_params=_params("arbitrary"),
        name="inproj",
    )(x, g, w)


def _attn_tile(q_heads, kb, vb, u, acc_ref, carry_ref, rows, causal):
    tk = kb.shape[0]
    lane = lax.broadcasted_iota(jnp.int32, (1, LANES), 1)
    first_head = lane < HEAD_DIM
    vz = jnp.zeros_like(vb)
    v_heads = (jnp.where(first_head, vb, vz), jnp.where(first_head, vz, vb))
    pv = None
    for h in range(2):
        z = lax.dot_general(q_heads[h], kb, _NT, preferred_element_type=F32)
        sp = jnp.maximum(z, 0.0) + jnp.log(1.0 + jnp.exp2(jnp.abs(z) * (-LOG2E)))
        if causal is not None:
            sp = jnp.where(causal, sp, 0.0)
        suffix = _dot(sp.astype(BF16), u)
        carry = carry_ref[h, rows, :]
        w = jnp.exp2((z - suffix - jnp.tile(carry, (1, tk // LANES))) * LOG2E)
        if causal is not None:
            w = jnp.where(causal, w, 0.0)
        d = _dot(w.astype(BF16), v_heads[h])
        pv = d if pv is None else pv + d
        carry_ref[h, rows, :] = carry + suffix[:, 0:1]
    acc_ref[rows, :] += pv


def _attn_kernel(q_ref, k_ref, v_ref, u_ref, o_ref, acc_ref, carry_ref, *, tk):
    s = q_ref.shape[0]
    nb = s // tk
    lane = lax.broadcasted_iota(jnp.int32, (1, LANES), 1)
    first_head = lane < HEAD_DIM
    u = u_ref[...]
    causal = (lax.broadcasted_iota(jnp.int32, (2 * tk, tk), 1)
              < lax.broadcasted_iota(jnp.int32, (2 * tk, tk), 0))
    acc_ref[...] = jnp.zeros_like(acc_ref)
    carry_ref[...] = jnp.zeros_like(carry_ref)

    def tile(r0, nrows, kb, mask):
        rows = pl.ds(pl.multiple_of(r0, tk), nrows)
        ks = pl.multiple_of(kb * tk, tk)
        q = q_ref[rows, :] * jnp.asarray(HEAD_DIM ** -0.5, BF16)
        qz = jnp.zeros_like(q)
        q_heads = (jnp.where(first_head, q, qz), jnp.where(first_head, qz, q))
        _attn_tile(q_heads, k_ref[pl.ds(ks, tk), :], v_ref[pl.ds(ks, tk), :],
                   u, acc_ref, carry_ref, rows, mask)

    tile((nb - 1) * tk, tk, nb - 1, causal[:tk])

    def key_block(i, c):
        kb = nb - 2 - i
        tile(kb * tk, 2 * tk, kb, causal)
        return c

    lax.fori_loop(0, nb - 1, key_block, 0)

    def query_block(j, c):
        block_rows = pl.ds(pl.multiple_of(j * tk, tk), tk)

        def more(state):
            kb, min_carry = state
            return jnp.logical_and(kb >= 0, min_carry < UNDERFLOW_EXPONENT)

        def body(state):
            kb, _ = state
            tile(j * tk, tk, kb, None)
            return kb - 1, jnp.min(carry_ref[:, block_rows, :])

        lax.while_loop(more, body, (j - 2, jnp.min(carry_ref[:, block_rows, :])))
        return c

    lax.fori_loop(2, nb, query_block, 0)
    o_ref[...] = acc_ref[...].astype(BF16)


def _attention(proj, u, *, q_blk, k_blk, v_blk):
    b, s, _ = proj.shape
    hp = N_HEADS * HEAD_DIM // LANES
    cols = lambda blk: pl.BlockSpec((None, s, LANES), lambda bi, hi: (bi, 0, blk + hi))
    return pl.pallas_call(
        functools.partial(_attn_kernel, tk=ATT_TILE),
        out_shape=jax.ShapeDtypeStruct((b, s, N_HEADS * HEAD_DIM), BF16),
        grid=(b, hp),
        in_specs=[cols(q_blk), cols(k_blk), cols(v_blk),
                  pl.BlockSpec((ATT_TILE, ATT_TILE), lambda bi, hi: (0, 0))],
        out_specs=pl.BlockSpec((None, s, LANES), lambda bi, hi: (bi, 0, hi)),
        scratch_shapes=[
            pltpu.VMEM((s, LANES), F32),
            pltpu.VMEM((2, s, LANES), F32),
        ],
        compiler_params=_params("arbitrary", "arbitrary"),
        name="sb_attention",
    )(proj, proj, proj, u)


def _rglru_kernel(xr_ref, xg_ref, cw_ref, cb_ref, wa_ref, ba_ref, wx_ref, bx_ref, lam_ref,
                  o_ref, tail_ref, h_ref, a_scr, b_scr, *, ts):
    @pl.when(pl.program_id(1) == 0)
    def _():
        tail_ref[...] = jnp.zeros_like(tail_ref)
        h_ref[...] = jnp.zeros_like(h_ref)

    groups = ts // SUBLANES
    dr = xr_ref.shape[1]
    xr = xr_ref[...].astype(F32).reshape(groups, SUBLANES, dr)
    tail = tail_ref[...]
    tail_ref[...] = xr[groups - 1]
    in_group = lax.broadcasted_iota(jnp.int32, xr.shape, 1)
    xc = cb_ref[...].reshape(1, 1, dr)
    for j in range(CONV_WIDTH):
        back = CONV_WIDTH - 1 - j
        if back == 0:
            shifted = xr
        else:
            rot = pltpu.roll(xr, back, 1)
            rot_before = jnp.concatenate(
                [pltpu.roll(tail, back, 0).reshape(1, SUBLANES, dr), rot[:groups - 1]], axis=0)
            shifted = jnp.where(in_group >= back, rot, rot_before)
        xc = xc + cw_ref[j:j + 1, :].reshape(1, 1, dr) * shifted
    xc = xc.reshape(ts, dr)
    xcb = xc.astype(BF16)
    r = 0.5 + 0.5 * jnp.tanh(0.5 * (_dot(xcb, wa_ref[...]) + ba_ref[...]))
    ig = 0.5 + 0.5 * jnp.tanh(0.5 * (_dot(xcb, wx_ref[...]) + bx_ref[...]))
    lam = lam_ref[...]
    softplus_neg_lam = jnp.maximum(-lam, 0.0) + jnp.log(1.0 + jnp.exp(-jnp.abs(lam)))
    log_a = (-RG_C * softplus_neg_lam) * r
    a = jnp.exp(log_a)
    v = 1.0 - a * a
    root = jnp.where(v > 0.0, v * lax.rsqrt(v), 0.0)
    b = root * (ig * xc)

    a = a.reshape(groups, SUBLANES, dr)
    b = b.reshape(groups, SUBLANES, dr)
    for s in (1, 2, 4):
        has_prev = in_group >= s
        b = jnp.where(has_prev, a * pltpu.roll(b, s, 1) + b, b)
        a = jnp.where(has_prev, a * pltpu.roll(a, s, 1), a)
    a_scr[...] = a.reshape(ts, dr)
    b_scr[...] = b.reshape(ts, dr)

    def group(gidx, h):
        r0 = pl.multiple_of(gidx * SUBLANES, SUBLANES)
        hg = a_scr[pl.ds(r0, SUBLANES), :] * h + b_scr[pl.ds(r0, SUBLANES), :]
        b_scr[pl.ds(r0, SUBLANES), :] = hg
        return hg[SUBLANES - 1:, :]

    h_ref[...] = lax.fori_loop(0, ts // SUBLANES, group, h_ref[...], unroll=4)
    xg = xg_ref[...].astype(F32)
    gelu = 0.5 * xg * (1.0 + jnp.tanh(0.7978845608028654 * (xg + 0.044715 * (xg * xg * xg))))
    o_ref[...] = (b_scr[...] * gelu).astype(BF16)


def _rglru(proj, cw, cb, wa, ba, wx, bx, lam, layer, *, xr_blk, xg_blk):
    b, s, _ = proj.shape
    dr = cw.shape[2]
    return pl.pallas_call(
        functools.partial(_rglru_kernel, ts=RNN_TS),
        out_shape=jax.ShapeDtypeStruct((b, s, dr), BF16),
        grid=(b, s // RNN_TS),
        in_specs=[
            pl.BlockSpec((None, RNN_TS, dr), lambda bi, si: (bi, si, xr_blk)),
            pl.BlockSpec((None, RNN_TS, dr), lambda bi, si: (bi, si, xg_blk)),
            _layer_spec(cw, layer), _layer_spec(cb, layer),
            _layer_spec(wa, layer), _layer_spec(ba, layer),
            _layer_spec(wx, layer), _layer_spec(bx, layer),
            _layer_spec(lam, layer),
        ],
        out_specs=pl.BlockSpec((None, RNN_TS, dr), lambda bi, si: (bi, si, 0)),
        scratch_shapes=[
            pltpu.VMEM((SUBLANES, dr), F32),
            pltpu.VMEM((1, dr), F32),
            pltpu.VMEM((RNN_TS, dr), F32),
            pltpu.VMEM((RNN_TS, dr), F32),
        ],
        compiler_params=_params("arbitrary", "arbitrary"),
        name="rglru",
    )(proj, proj, cw, cb, wa, ba, wx, bx, lam)


def _first_argmax(vals):
    m = vals[0]
    for v in vals[1:]:
        m = jnp.maximum(m, v)
    idx = jnp.full_like(m, float(len(vals) - 1))
    for k in range(len(vals) - 2, -1, -1):
        idx = jnp.where(vals[k] == m, float(k), idx)
    return m, idx


def _merge_kernel(x_ref, at_ref, rn_ref, ga_ref, gr_ref, wao_ref, wro_ref, wo_ref, g_ref,
                  wr_ref, br_ref, x1_ref, hr_ref, cls_ref):
    d = x_ref.shape[1]
    ya = _dot(at_ref[...], wao_ref[...])
    yr = _dot(rn_ref[...], wro_ref[...])
    merged = _sigmoid(ga_ref[...].astype(F32)) * ya + _sigmoid(gr_ref[...].astype(F32)) * yr
    x1 = x_ref[...] + _dot(merged.astype(BF16), wo_ref[...])
    x1_ref[...] = x1
    h = _rms(x1, g_ref[...])
    hr_ref[:, :d] = h

    h_hi = h.astype(BF16)
    h_lo = (h - h_hi.astype(F32)).astype(BF16)
    wr = wr_ref[...]
    la = _dot(h_hi, wr)
    lb = _dot(h_lo, wr)
    logits = la + pltpu.roll(la, LANES - CLASS_ROWS, 1) + lb
    lt = logits.T[:CLASS_ROWS, :] + br_ref[...]
    g = [lt[k:k + 1, :] for k in range(N_GROUPS)]
    e = [lt[N_GROUPS + k:N_GROUPS + k + 1, :] for k in range(N_EXPERTS)]

    gmax, gi = _first_argmax(g)
    den = jnp.exp(g[0] - gmax)
    for k in range(1, N_GROUPS):
        den = den + jnp.exp(g[k] - gmax)
    gate = 1.0 / den
    sel = []
    for j in range(EXPERTS_PER_GROUP):
        s = e[(N_GROUPS - 1) * EXPERTS_PER_GROUP + j]
        for k in range(N_GROUPS - 2, -1, -1):
            s = jnp.where(gi == float(k), e[k * EXPERTS_PER_GROUP + j], s)
        sel.append(s)
    m1, i1 = _first_argmax(sel)
    sel2 = [jnp.where(i1 == float(j), -jnp.inf, sel[j]) for j in range(EXPERTS_PER_GROUP)]
    m2, i2 = _first_argmax(sel2)
    t = jnp.exp(m2 - m1)
    w1 = gate * (1.0 / (1.0 + t))
    w2 = gate * (t / (1.0 + t))
    first_is_lo = i1 < i2
    a = jnp.minimum(i1, i2)
    b = jnp.maximum(i1, i2)
    w_lo = jnp.where(first_is_lo, w1, w2)
    w_hi = jnp.where(first_is_lo, w2, w1)
    pair = jnp.where(a == 0.0, b - 1.0, jnp.where(a == 1.0, b + 1.0, 5.0))
    cls = gi * float(PAIRS_PER_GROUP) + pair

    n = cls.shape[1]
    cls_ref[...] = jnp.concatenate([cls, jnp.zeros((SUBLANES - 1, n), F32)], axis=0)
    rows = jnp.concatenate([w_lo, w_hi, jnp.zeros((ROUTE_LANES - 2, n), F32)], axis=0)
    hr_ref[:, d:] = rows.T


def _merge(x, attn, rnn, proj, wao, wro, wo, g, wr, br, layer, *, ga_blk, gr_blk):
    t, d = x.shape
    da = attn.shape[1]
    dr = rnn.shape[1]
    tm = ROW_TILE
    return pl.pallas_call(
        _merge_kernel,
        out_shape=(
            jax.ShapeDtypeStruct((t, d), F32),
            jax.ShapeDtypeStruct((t, d + ROUTE_LANES), F32),
            jax.ShapeDtypeStruct((SUBLANES, t), F32),
        ),
        grid=(t // tm,),
        in_specs=[
            pl.BlockSpec((tm, d), lambda i: (i, 0)),
            pl.BlockSpec((tm, da), lambda i: (i, 0)),
            pl.BlockSpec((tm, dr), lambda i: (i, 0)),
            pl.BlockSpec((tm, d), lambda i: (i, ga_blk)),
            pl.BlockSpec((tm, d), lambda i: (i, gr_blk)),
            _layer_spec(wao, layer), _layer_spec(wro, layer), _layer_spec(wo, layer),
            _layer_spec(g, layer), _layer_spec(wr, layer), _layer_spec(br, layer),
        ],
        out_specs=(
            pl.BlockSpec((tm, d), lambda i: (i, 0)),
            pl.BlockSpec((tm, d + ROUTE_LANES), lambda i: (i, 0)),
            pl.BlockSpec((SUBLANES, tm), lambda i: (0, i)),
        ),
        compiler_params=_params("arbitrary"),
        name="merge_router",
    )(x, attn, rnn, proj, proj, wao, wro, wo, g, wr, br)


def _rank_kernel(cls_ref, lt_ref, pos_ref, tinfo_ref, cnt_ref, off_ref, run_ref, *, tm):
    ph = pl.program_id(0)
    bi = pl.program_id(1)
    cls = cls_ref[0:1, :]
    crow = lax.broadcasted_iota(jnp.int32, (CLASS_ROWS, tm), 0).astype(F32)
    member = crow == cls
    onehot = jnp.where(member, 1.0, 0.0)
    block_count = jnp.sum(onehot, axis=1, keepdims=True)

    @pl.when(jnp.logical_and(ph == 0, bi == 0))
    def _():
        cnt_ref[...] = jnp.zeros_like(cnt_ref)

    @pl.when(ph == 0)
    def _():
        cnt_ref[...] += block_count

    @pl.when(jnp.logical_and(ph == 1, bi == 0))
    def _():
        ntile = jnp.floor((cnt_ref[...] + float(MOE_TILE - 1)) * (1.0 / MOE_TILE))
        rid = lax.broadcasted_iota(jnp.int32, (CLASS_ROWS, LANES), 0)
        toff = jnp.zeros((CLASS_ROWS, LANES), F32)
        for c in range(1, CLASS_ROWS):
            toff = toff + jnp.where(rid >= c, ntile[c - 1:c, :], 0.0)
        off_ref[...] = toff * float(MOE_TILE)
        run_ref[...] = jnp.zeros_like(run_ref)
        ti = lax.broadcasted_iota(jnp.int32, (CLASS_ROWS, LANES), 1).astype(F32)
        inside = jnp.where(ti >= toff, jnp.where(ti < toff + ntile, 1.0, 0.0), 0.0)
        c = rid.astype(F32)
        grp = (jnp.where(c >= 6.0, 1.0, 0.0) + jnp.where(c >= 12.0, 1.0, 0.0)
               + jnp.where(c >= 18.0, 1.0, 0.0))
        pair = c - float(PAIRS_PER_GROUP) * grp
        a = jnp.where(pair >= 3.0, 1.0, 0.0) + jnp.where(pair >= 5.0, 1.0, 0.0)
        b = jnp.where(pair < 3.0, pair + 1.0, jnp.where(pair < 5.0, pair - 1.0, 3.0))
        e_lo = jnp.sum(inside * (float(EXPERTS_PER_GROUP) * grp + a), axis=0, keepdims=True)
        e_hi = jnp.sum(inside * (float(EXPERTS_PER_GROUP) * grp + b), axis=0, keepdims=True)
        first = (ti - toff) == 0.0
        tile_c = toff + jnp.where(first, ntile - 1.0, ti - toff - 1.0)
        rows_c = jnp.where(first, cnt_ref[...] - (ntile - 1.0) * float(MOE_TILE), float(MOE_TILE))
        used = jnp.sum(ntile, axis=0, keepdims=True)
        tile = jnp.sum(inside * tile_c, axis=0, keepdims=True)
        tile = tile + jnp.where(ti[0:1, :] >= used, ti[0:1, :], 0.0)
        rows = jnp.sum(inside * rows_c, axis=0, keepdims=True)
        left = cnt_ref[...] - (ti - toff) * float(MOE_TILE)
        tile_rows = jnp.sum(inside * jnp.minimum(left, float(MOE_TILE)), axis=0, keepdims=True)
        info = jnp.concatenate([e_lo, e_hi, rows, used, tile, tile_rows,
                                jnp.zeros((SUBLANES - 6, LANES), F32)], axis=0)
        tinfo_ref[...] = info.astype(jnp.int32)

    @pl.when(ph == 1)
    def _():
        before = _dot(onehot.astype(BF16), lt_ref[...])
        val = before + run_ref[:, 0:1] + off_ref[:, 0:1]
        pos = jnp.sum(jnp.where(member, val, 0.0), axis=0, keepdims=True)
        pos_ref[...] = pos.astype(jnp.int32)
        run_ref[...] += block_count


def _rank(cls, strict_lt):
    t = cls.shape[1]
    tm = RANK_TILE
    return pl.pallas_call(
        functools.partial(_rank_kernel, tm=tm),
        out_shape=(
            jax.ShapeDtypeStruct((1, t), jnp.int32),
            jax.ShapeDtypeStruct((SUBLANES, LANES), jnp.int32),
        ),
        grid=(2, t // tm),
        in_specs=[
            pl.BlockSpec((SUBLANES, tm), lambda ph, bi: (0, bi)),
            pl.BlockSpec((tm, tm), lambda ph, bi: (0, 0)),
        ],
        out_specs=(
            pl.BlockSpec((1, tm), lambda ph, bi: (0, bi * ph)),
            pl.BlockSpec((SUBLANES, LANES), lambda ph, bi: (0, 0)),
        ),
        scratch_shapes=[pltpu.VMEM((CLASS_ROWS, LANES), F32)] * 3,
        compiler_params=_params("arbitrary", "arbitrary"),
        name="rank_tokens",
    )(cls, strict_lt)


def _scatter_kernel(info_ref, pos_ref, h_ref, o_ref, zero_ref, sem, zsem, *, tm):
    tile_groups = MOE_TILE // SUBLANES

    @pl.when(pl.program_id(0) == 0)
    def _():
        zero_ref[...] = jnp.zeros_like(zero_ref)

        def tile_clear(ti):
            dst = o_ref.at[pl.ds(ti * tile_groups, tile_groups)]
            return pltpu.make_async_copy(zero_ref, dst, zsem)

        n_tiles = o_ref.shape[0] // tile_groups
        for ti in range(n_tiles):
            @pl.when(info_ref[5, ti] < MOE_TILE)
            def _():
                tile_clear(ti).start()
        for ti in range(n_tiles):
            @pl.when(info_ref[5, ti] < MOE_TILE)
            def _():
                tile_clear(ti).wait()

    def row_copy(grp, j, p):
        dst = o_ref.at[lax.shift_right_logical(p, 3), pl.ds(jnp.bitwise_and(p, SUBLANES - 1), 1)]
        return pltpu.make_async_copy(h_ref.at[grp, pl.ds(j, 1)], dst, sem)

    def start(grp, c):
        for j in range(SUBLANES):
            row_copy(grp, j, pos_ref[0, 0, grp * SUBLANES + j]).start(priority=j % 2)
        return c

    def wait(r, c):
        row_copy(0, 0, 0).wait()
        return c

    lax.fori_loop(0, tm // SUBLANES, start, 0)
    lax.fori_loop(0, tm, wait, 0, unroll=DMA_UNROLL)


def _scatter_rows(tinfo, pos, rows, n_sorted):
    t, w = rows.shape
    tm = SCATTER_TILE
    return pl.pallas_call(
        functools.partial(_scatter_kernel, tm=tm),
        out_shape=jax.ShapeDtypeStruct((n_sorted // SUBLANES, SUBLANES, w), rows.dtype),
        grid_spec=pltpu.PrefetchScalarGridSpec(
            num_scalar_prefetch=1,
            grid=(t // tm,),
            in_specs=[
                pl.BlockSpec((1, 1, tm), lambda i, info: (i, 0, 0), memory_space=pltpu.SMEM),
                pl.BlockSpec((tm // SUBLANES, SUBLANES, w), lambda i, info: (i, 0, 0)),
            ],
            out_specs=pl.BlockSpec(memory_space=pl.ANY),
            scratch_shapes=[
                pltpu.VMEM((MOE_TILE // SUBLANES, SUBLANES, w), rows.dtype),
                pltpu.SemaphoreType.DMA(()),
                pltpu.SemaphoreType.DMA(()),
            ],
        ),
        compiler_params=_params("arbitrary"),
        name="scatter_rows",
    )(tinfo, pos.reshape(t // tm, 1, tm), rows.reshape(t // SUBLANES, SUBLANES, w))


def _ffn_kernel(info_ref, x_ref, wgl_ref, wul_ref, wdl_ref, wgh_ref, wuh_ref, wdh_ref, o_ref,
                wg_s, wu_s, wd_s):
    i = pl.program_id(0)
    tm, d = o_ref.shape
    half = tm // 2
    live = i < info_ref[3, 0]
    prev = jnp.maximum(i - 1, 0)
    for k, (wg, wu, wd) in enumerate(((wgl_ref, wul_ref, wdl_ref), (wgh_ref, wuh_ref, wdh_ref))):
        changed = jnp.logical_or(i == 0, info_ref[k, i] != info_ref[k, prev])

        @pl.when(jnp.logical_and(live, changed))
        def _():
            wg_s[k] = wg[...].astype(BF16)
            wu_s[k] = wu[...].astype(BF16)
            wd_s[k] = wd[...].astype(BF16)

    def ffn(rows):
        xt = x_ref[rows, :]
        xb = xt[:, :d].astype(BF16)
        y = None
        for k in range(2):
            gte = _dot(xb, wg_s[k])
            up = _dot(xb, wu_s[k])
            he = (gte * _sigmoid(gte)) * up
            part = xt[:, d + k:d + k + 1] * _dot(he.astype(BF16), wd_s[k])
            y = part if y is None else y + part
        o_ref[rows, :] = y

    @pl.when(jnp.logical_not(live))
    def _():
        o_ref[...] = jnp.zeros_like(o_ref)

    @pl.when(jnp.logical_and(live, info_ref[2, i] > half))
    def _():
        ffn(slice(0, tm))

    @pl.when(jnp.logical_and(live, info_ref[2, i] <= half))
    def _():
        ffn(slice(0, half))
        o_ref[half:, :] = jnp.zeros((tm - half, d), F32)


def _expert_ffn(tinfo, xs, wg, wu, wd, layer):
    npad, w = xs.shape
    _, _, d, de = wg.shape
    tm = MOE_TILE

    def step(i, info):
        return jnp.minimum(i, info[3, 0] - 1)

    w_in = lambda row: pl.BlockSpec((None, None, d, de),
                                    lambda i, info: (layer, info[row, step(i, info)], 0, 0))
    w_out = lambda row: pl.BlockSpec((None, None, de, d),
                                     lambda i, info: (layer, info[row, step(i, info)], 0, 0))
    return pl.pallas_call(
        _ffn_kernel,
        out_shape=jax.ShapeDtypeStruct((npad, d), F32),
        grid_spec=pltpu.PrefetchScalarGridSpec(
            num_scalar_prefetch=1,
            grid=(npad // tm,),
            in_specs=[
                pl.BlockSpec((tm, w), lambda i, info: (info[4, step(i, info)], 0)),
                w_in(0), w_in(0), w_out(0), w_in(1), w_in(1), w_out(1),
            ],
            out_specs=pl.BlockSpec((tm, d), lambda i, info: (info[4, i], 0)),
            scratch_shapes=[
                pltpu.VMEM((2, d, de), BF16),
                pltpu.VMEM((2, d, de), BF16),
                pltpu.VMEM((2, de, d), BF16),
            ],
        ),
        compiler_params=_params("arbitrary"),
        name="expert_ffn",
    )(tinfo, xs, wg, wu, wd, wg, wu, wd)


def _ple_kernel(pos_ref, posn_ref, x_ref, p_ref, ys_ref, g_ref, wg_ref, wp_ref, gn_ref, *rest,
                tm, nsteps, col_map):
    if col_map is None:
        o_ref, ybuf, sem = rest
    else:
        win_ref, o_ref, proj_ref, ybuf, sem = rest
    i = pl.program_id(0)
    slot = lax.rem(i, 2)
    d = x_ref.shape[1]

    def row_copy(grp, j, p, s):
        src = ys_ref.at[lax.shift_right_logical(p, 3), pl.ds(jnp.bitwise_and(p, SUBLANES - 1), 1)]
        return pltpu.make_async_copy(src, ybuf.at[s, grp, pl.ds(j, 1)], sem.at[s])

    def wait_slot(s):
        def wait(r, c):
            row_copy(0, 0, 0, s).wait()
            return c
        lax.fori_loop(0, tm, wait, 0, unroll=DMA_UNROLL)

    @pl.when(i == 0)
    def _():
        def start(grp, c):
            for j in range(SUBLANES):
                row_copy(grp, j, pos_ref[0, 0, grp * SUBLANES + j], 0).start()
            return c
        lax.fori_loop(0, tm // SUBLANES, start, 0)

    wait_slot(slot)

    n_stage = 1 if col_map is None else (len(col_map) + 1) // 2
    per_stage = -(-(tm // SUBLANES) // n_stage)

    def prefetch(stage):
        for grp in range(stage * per_stage, min((stage + 1) * per_stage, tm // SUBLANES)):
            for j in range(SUBLANES):
                row_copy(grp, j, posn_ref[0, 0, grp * SUBLANES + j], 1 - slot).start(priority=j % 2)

    x2 = x_ref[...] + ybuf[slot].reshape(tm, d)
    if col_map is None:
        prefetch(0)
    gate = _sigmoid(_dot(_rms(x2, g_ref[...]).astype(BF16), wg_ref[...]))
    x3 = x2 + gate * _dot(p_ref[...].astype(BF16), wp_ref[...])
    if col_map is None:
        o_ref[...] = _rms(x3, gn_ref[...])
    else:
        o_ref[...] = x3
        h = _rms(x3, gn_ref[...]).astype(BF16)
        for stage, (src, dst) in enumerate(col_map):
            prefetch(stage)
            proj_ref[:, dst:dst + COL_CHUNK] = _dot(h, win_ref[:, src:src + COL_CHUNK]).astype(BF16)

    @pl.when(i == nsteps - 1)
    def _():
        wait_slot(1 - slot)


def _ple(pos, x, p, ys, g, wg, wp, layer, *, g_next, next_layer, w_in=None, col_map=None):
    t, d = x.shape
    dp = p.shape[2]
    tm = PLE_TILE
    nsteps = t // tm
    pos3 = pos.reshape(nsteps, 1, tm)
    in_specs = [
        pl.BlockSpec((1, 1, tm), lambda i: (i, 0, 0), memory_space=pltpu.SMEM),
        pl.BlockSpec((1, 1, tm), lambda i: (jnp.minimum(i + 1, nsteps - 1), 0, 0),
                     memory_space=pltpu.SMEM),
        pl.BlockSpec((tm, d), lambda i: (i, 0)),
        pl.BlockSpec((None, tm, dp), lambda i: (layer, i, 0)),
        pl.BlockSpec(memory_space=pl.ANY),
        _layer_spec(g, layer), _layer_spec(wg, layer), _layer_spec(wp, layer),
        _layer_spec(g_next, next_layer),
    ]
    args = [pos3, pos3, x, p, ys.reshape(-1, SUBLANES, d), g, wg, wp, g_next]
    out_shape = jax.ShapeDtypeStruct((t, d), F32)
    out_specs = pl.BlockSpec((tm, d), lambda i: (i, 0))
    if col_map is not None:
        n = w_in.shape[2]
        in_specs.append(_layer_spec(w_in, next_layer))
        args.append(w_in)
        out_shape = (out_shape, jax.ShapeDtypeStruct((t, n), BF16))
        out_specs = (out_specs, pl.BlockSpec((tm, n), lambda i: (i, 0)))
    return pl.pallas_call(
        functools.partial(_ple_kernel, tm=tm, nsteps=nsteps, col_map=col_map),
        out_shape=out_shape,
        grid=(nsteps,),
        in_specs=in_specs,
        out_specs=out_specs,
        scratch_shapes=[pltpu.VMEM((2, tm // SUBLANES, SUBLANES, d), F32),
                        pltpu.SemaphoreType.DMA((2,))],
        compiler_params=_params("arbitrary"),
        name="gather_ple",
    )(*args)


def _block_diag(w):
    l, g, n, _ = w.shape
    eye = jnp.eye(g, dtype=w.dtype)
    return (eye[None, :, None, :, None] * w[:, :, :, None, :]).reshape(l, g * n, g * n)


def kernel(x, p, norm_mix, w_in, conv_w, conv_b, w_rg_a, b_rg_a, w_rg_x, b_rg_x, rg_lambda,
           w_attn_o, w_rnn_o, w_out, norm_moe, w_router_group, b_router_group, w_router_expert,
           b_router_expert, w_exp_gate, w_exp_up, w_exp_down, norm_ple, w_ple_gate, w_ple_proj,
           norm_final):
    bsz, seq, d = x.shape
    depth = w_in.shape[0]
    t = bsz * seq
    d_attn = N_HEADS * HEAD_DIM
    d_rnn = conv_w.shape[2]
    assert d_attn == d_rnn and d == 2 * d_attn and d_attn == COL_CHUNK
    assert t % ROW_TILE == 0 and seq % (2 * ATT_TILE) == 0 and seq % RNN_TS == 0
    assert t % RANK_TILE == 0
    assert t % SCATTER_TILE == 0 and t % PLE_TILE == 0

    n_in = w_in.shape[2]
    gates_at = 3 * d_attn + 2 * d_rnn
    col_map = tuple((src, (src - gates_at) % n_in) for src in range(0, n_in, COL_CHUNK))
    ga_blk, gr_blk = 0, 1
    q_col = 2 * d
    q_blk, k_blk, v_blk = (q_col // LANES, (q_col + d_attn) // LANES, (q_col + 2 * d_attn) // LANES)
    xr_blk, xg_blk = (q_col + 3 * d_attn) // d_rnn, (q_col + 3 * d_attn + d_rnn) // d_rnn

    tri_incl = jnp.tril(jnp.ones((ATT_TILE, ATT_TILE), F32)).astype(BF16)
    strict_lt = jnp.triu(jnp.ones((RANK_TILE, RANK_TILE), F32), k=1).astype(BF16)

    n_tiles = (t + N_CLASSES * (MOE_TILE - 1)) // MOE_TILE
    assert n_tiles <= LANES
    n_sorted = n_tiles * MOE_TILE

    vec = lambda v: v.reshape(depth, 1, -1)
    w_in_b = w_in.astype(BF16)
    wa_bd = _block_diag(w_rg_a).astype(BF16)
    wx_bd = _block_diag(w_rg_x).astype(BF16)
    w_r = jnp.concatenate([w_router_group, w_router_expert], axis=2)
    w_r = jnp.pad(w_r, ((0, 0), (0, 0), (0, CLASS_ROWS - w_r.shape[2])))
    w_r_hi = w_r.astype(BF16)
    w_r_lo = (w_r - w_r_hi.astype(F32)).astype(BF16)
    w_r_cat = jnp.pad(jnp.concatenate([w_r_hi, w_r_lo], axis=2),
                      ((0, 0), (0, 0), (0, LANES - 2 * CLASS_ROWS)))
    b_r = jnp.concatenate([b_router_group, b_router_expert], axis=1)
    b_r = jnp.pad(b_r, ((0, 0), (0, CLASS_ROWS - b_r.shape[1]))).reshape(depth, CLASS_ROWS, 1)
    wao_b, wro_b, wo_b = w_attn_o.astype(BF16), w_rnn_o.astype(BF16), w_out.astype(BF16)
    wpg_b, wpp_b = w_ple_gate.astype(BF16), w_ple_proj.astype(BF16)
    g_mix, g_moe, g_ple = vec(norm_mix), vec(norm_moe), vec(norm_ple)
    g_fin = norm_final.reshape(1, 1, -1)
    p3 = p.reshape(depth, t, -1)

    x2d = x.reshape(t, d)
    proj = _inproj(x2d, g_mix, w_in_b, 0, col_map)
    for i in range(depth):
        proj3 = proj.reshape(bsz, seq, -1)
        attn = _attention(proj3, tri_incl, q_blk=q_blk, k_blk=k_blk, v_blk=v_blk)
        rnn = _rglru(proj3, conv_w, vec(conv_b), wa_bd, vec(b_rg_a), wx_bd, vec(b_rg_x),
                     vec(rg_lambda), i, xr_blk=xr_blk, xg_blk=xg_blk)
        x1, routed, cls = _merge(
            x2d, attn.reshape(t, d_attn), rnn.reshape(t, d_rnn), proj, wao_b, wro_b, wo_b,
            g_moe, w_r_cat, b_r, i, ga_blk=ga_blk, gr_blk=gr_blk)
        pos, tinfo = _rank(cls, strict_lt)
        pos = pos.reshape(t)
        xs = _scatter_rows(tinfo, pos, routed, n_sorted)
        ys = _expert_ffn(tinfo, xs.reshape(n_sorted, -1), w_exp_gate, w_exp_up, w_exp_down, i)
        if i + 1 < depth:
            x2d, proj = _ple(pos, x1, p3, ys, g_ple, wpg_b, wpp_b, i, g_next=g_mix,
                             next_layer=i + 1, w_in=w_in_b, col_map=col_map)
        else:
            x2d = _ple(pos, x1, p3, ys, g_ple, wpg_b, wpp_b, i, g_next=g_fin, next_layer=0)
    return x2d.reshape(bsz, seq, d)
```

```python
import functools

import jax
import jax.numpy as jnp
from jax import lax
from jax.experimental import pallas as pl
from jax.experimental.pallas import tpu as pltpu

F32 = jnp.float32
BF16 = jnp.bfloat16
EPS = 1e-6

N_HEADS = 8
HEAD_DIM = 64
N_GROUPS = 4
EXPERTS_PER_GROUP = 4
N_EXPERTS = N_GROUPS * EXPERTS_PER_GROUP
PAIRS_PER_GROUP = 6
N_CLASSES = N_GROUPS * PAIRS_PER_GROUP
CLASS_ROWS = 32
CONV_WIDTH = 4
RG_C = 8.0

LANES = 128
SUBLANES = 8
VMEM_LIMIT = 56 * 1024 * 1024

ROW_TILE = 512
RANK_TILE = 2048
ATT_TILE = 256
RNN_TS = 512
MOE_TILE = 512
SCATTER_TILE = 2048
PLE_TILE = 512
ROUTE_LANES = 128
DMA_UNROLL = 8
COL_CHUNK = 512

_NT = (((1,), (1,)), ((), ()))
LOG2E = 1.4426950408889634
UNDERFLOW_EXPONENT = 151.0 / LOG2E


def _params(*sem):
    return pltpu.CompilerParams(dimension_semantics=sem, vmem_limit_bytes=VMEM_LIMIT)


def _layer_spec(arr, layer):
    nd = arr.ndim
    return pl.BlockSpec((None,) + arr.shape[1:], lambda *_: (layer,) + (0,) * (nd - 1))


def _rms(x, g):
    ms = jnp.mean(x * x, axis=-1, keepdims=True)
    return x * lax.rsqrt(ms + EPS) * g


def _sigmoid(x):
    return 1.0 / (1.0 + jnp.exp(-x))


def _dot(a, b):
    return jnp.dot(a, b, preferred_element_type=F32)


def _in_projection(x, g, w_ref, o_ref, col_map):
    h = _rms(x, g).astype(BF16)
    for src, dst in col_map:
        o_ref[:, dst:dst + COL_CHUNK] = _dot(h, w_ref[:, src:src + COL_CHUNK]).astype(BF16)


def _inproj_kernel(x_ref, g_ref, w_ref, o_ref, *, col_map):
    _in_projection(x_ref[...], g_ref[...], w_ref, o_ref, col_map)


def _inproj(x, g, w, layer, col_map):
    t, d = x.shape
    n = w.shape[2]
    return pl.pallas_call(
        functools.partial(_inproj_kernel, col_map=col_map),
        out_shape=jax.ShapeDtypeStruct((t, n), BF16),
        grid=(t // ROW_TILE,),
        in_specs=[
            pl.BlockSpec((ROW_TILE, d), lambda i: (i, 0)),
            _layer_spec(g, layer),
            _layer_spec(w, layer),
        ],
        out_specs=pl.BlockSpec((ROW_TILE, n), lambda i: (i, 0)),
        compiler_params=_params("arbitrary"),
        name="inproj",
    )(x, g, w)


def _attn_tile(q_heads, kb, vb, u, acc_ref, carry_ref, rows, causal):
    tk = kb.shape[0]
    lane = lax.broadcasted_iota(jnp.int32, (1, LANES), 1)
    first_head = lane < HEAD_DIM
    vz = jnp.zeros_like(vb)
    v_heads = (jnp.where(first_head, vb, vz), jnp.where(first_head, vz, vb))
    pv = None
    for h in range(2):
        z = lax.dot_general(q_heads[h], kb, _NT, preferred_element_type=F32)
        sp = jnp.maximum(z, 0.0) + jnp.log(1.0 + jnp.exp2(jnp.abs(z) * (-LOG2E)))
        if causal is not None:
            sp = jnp.where(causal, sp, 0.0)
        suffix = _dot(sp.astype(BF16), u)
        carry = carry_ref[h, rows, :]
        w = jnp.exp2((z - suffix - jnp.tile(carry, (1, tk // LANES))) * LOG2E)
        if causal is not None:
            w = jnp.where(causal, w, 0.0)
        d = _dot(w.astype(BF16), v_heads[h])
        pv = d if pv is None else pv + d
        carry_ref[h, rows, :] = carry + suffix[:, 0:1]
    acc_ref[rows, :] += pv


def _attn_kernel(q_ref, k_ref, v_ref, u_ref, o_ref, acc_ref, carry_ref, *, tk):
    s = q_ref.shape[0]
    nb = s // tk
    lane = lax.broadcasted_iota(jnp.int32, (1, LANES), 1)
    first_head = lane < HEAD_DIM
    u = u_ref[...]
    causal = (lax.broadcasted_iota(jnp.int32, (2 * tk, tk), 1)
              < lax.broadcasted_iota(jnp.int32, (2 * tk, tk), 0))
    acc_ref[...] = jnp.zeros_like(acc_ref)
    carry_ref[...] = jnp.zeros_like(carry_ref)

    def tile(r0, nrows, kb, mask):
        rows = pl.ds(pl.multiple_of(r0, tk), nrows)
        ks = pl.multiple_of(kb * tk, tk)
        q = q_ref[rows, :] * jnp.asarray(HEAD_DIM ** -0.5, BF16)
        qz = jnp.zeros_like(q)
        q_heads = (jnp.where(first_head, q, qz), jnp.where(first_head, qz, q))
        _attn_tile(q_heads, k_ref[pl.ds(ks, tk), :], v_ref[pl.ds(ks, tk), :],
                   u, acc_ref, carry_ref, rows, mask)

    tile((nb - 1) * tk, tk, nb - 1, causal[:tk])

    def key_block(i, c):
        kb = nb - 2 - i
        tile(kb * tk, 2 * tk, kb, causal)
        return c

    lax.fori_loop(0, nb - 1, key_block, 0)

    def query_block(j, c):
        block_rows = pl.ds(pl.multiple_of(j * tk, tk), tk)

        def more(state):
            kb, min_carry = state
            return jnp.logical_and(kb >= 0, min_carry < UNDERFLOW_EXPONENT)

        def body(state):
            kb, _ = state
            tile(j * tk, tk, kb, None)
            return kb - 1, jnp.min(carry_ref[:, block_rows, :])

        lax.while_loop(more, body, (j - 2, jnp.min(carry_ref[:, block_rows, :])))
        return c

    lax.fori_loop(2, nb, query_block, 0)
    o_ref[...] = acc_ref[...].astype(BF16)


def _attention(proj, u, *, q_blk, k_blk, v_blk):
    b, s, _ = proj.shape
    hp = N_HEADS * HEAD_DIM // LANES
    cols = lambda blk: pl.BlockSpec((None, s, LANES), lambda bi, hi: (bi, 0, blk + hi))
    return pl.pallas_call(
        functools.partial(_attn_kernel, tk=ATT_TILE),
        out_shape=jax.ShapeDtypeStruct((b, s, N_HEADS * HEAD_DIM), BF16),
        grid=(b, hp),
        in_specs=[cols(q_blk), cols(k_blk), cols(v_blk),
                  pl.BlockSpec((ATT_TILE, ATT_TILE), lambda bi, hi: (0, 0))],
        out_specs=pl.BlockSpec((None, s, LANES), lambda bi, hi: (bi, 0, hi)),
        scratch_shapes=[
            pltpu.VMEM((s, LANES), F32),
            pltpu.VMEM((2, s, LANES), F32),
        ],
        compiler_params=_params("arbitrary", "arbitrary"),
        name="sb_attention",
    )(proj, proj, proj, u)


def _rglru_kernel(xr_ref, xg_ref, cw_ref, cb_ref, wa_ref, ba_ref, wx_ref, bx_ref, lam_ref,
                  o_ref, tail_ref, h_ref, a_scr, b_scr, *, ts):
    @pl.when(pl.program_id(1) == 0)
    def _():
        tail_ref[...] = jnp.zeros_like(tail_ref)
        h_ref[...] = jnp.zeros_like(h_ref)

    groups = ts // SUBLANES
    dr = xr_ref.shape[1]
    xr = xr_ref[...].astype(F32).reshape(groups, SUBLANES, dr)
    tail = tail_ref[...]
    tail_ref[...] = xr[groups - 1]
    in_group = lax.broadcasted_iota(jnp.int32, xr.shape, 1)
    xc = cb_ref[...].reshape(1, 1, dr)
    for j in range(CONV_WIDTH):
        back = CONV_WIDTH - 1 - j
        if back == 0:
            shifted = xr
        else:
            rot = pltpu.roll(xr, back, 1)
            rot_before = jnp.concatenate(
                [pltpu.roll(tail, back, 0).reshape(1, SUBLANES, dr), rot[:groups - 1]], axis=0)
            shifted = jnp.where(in_group >= back, rot, rot_before)
        xc = xc + cw_ref[j:j + 1, :].reshape(1, 1, dr) * shifted
    xc = xc.reshape(ts, dr)
    xcb = xc.astype(BF16)
    r = 0.5 + 0.5 * jnp.tanh(0.5 * (_dot(xcb, wa_ref[...]) + ba_ref[...]))
    ig = 0.5 + 0.5 * jnp.tanh(0.5 * (_dot(xcb, wx_ref[...]) + bx_ref[...]))
    lam = lam_ref[...]
    softplus_neg_lam = jnp.maximum(-lam, 0.0) + jnp.log(1.0 + jnp.exp(-jnp.abs(lam)))
    log_a = (-RG_C * softplus_neg_lam) * r
    a = jnp.exp(log_a)
    v = 1.0 - a * a
    root = jnp.where(v > 0.0, v * lax.rsqrt(v), 0.0)
    b = root * (ig * xc)

    a = a.reshape(groups, SUBLANES, dr)
    b = b.reshape(groups, SUBLANES, dr)
    for s in (1, 2, 4):
        has_prev = in_group >= s
        b = jnp.where(has_prev, a * pltpu.roll(b, s, 1) + b, b)
        a = jnp.where(has_prev, a * pltpu.roll(a, s, 1), a)
    a_scr[...] = a.reshape(ts, dr)
    b_scr[...] = b.reshape(ts, dr)

    def group(gidx, h):
        r0 = pl.multiple_of(gidx * SUBLANES, SUBLANES)
        hg = a_scr[pl.ds(r0, SUBLANES), :] * h + b_scr[pl.ds(r0, SUBLANES), :]
        b_scr[pl.ds(r0, SUBLANES), :] = hg
        return hg[SUBLANES - 1:, :]

    h_ref[...] = lax.fori_loop(0, ts // SUBLANES, group, h_ref[...], unroll=4)
    xg = xg_ref[...].astype(F32)
    gelu = 0.5 * xg * (1.0 + jnp.tanh(0.7978845608028654 * (xg + 0.044715 * (xg * xg * xg))))
    o_ref[...] = (b_scr[...] * gelu).astype(BF16)


def _rglru(proj, cw, cb, wa, ba, wx, bx, lam, layer, *, xr_blk, xg_blk):
    b, s, _ = proj.shape
    dr = cw.shape[2]
    return pl.pallas_call(
        functools.partial(_rglru_kernel, ts=RNN_TS),
        out_shape=jax.ShapeDtypeStruct((b, s, dr), BF16),
        grid=(b, s // RNN_TS),
        in_specs=[
            pl.BlockSpec((None, RNN_TS, dr), lambda bi, si: (bi, si, xr_blk)),
            pl.BlockSpec((None, RNN_TS, dr), lambda bi, si: (bi, si, xg_blk)),
            _layer_spec(cw, layer), _layer_spec(cb, layer),
            _layer_spec(wa, layer), _layer_spec(ba, layer),
            _layer_spec(wx, layer), _layer_spec(bx, layer),
            _layer_spec(lam, layer),
        ],
        out_specs=pl.BlockSpec((None, RNN_TS, dr), lambda bi, si: (bi, si, 0)),
        scratch_shapes=[
            pltpu.VMEM((SUBLANES, dr), F32),
            pltpu.VMEM((1, dr), F32),
            pltpu.VMEM((RNN_TS, dr), F32),
            pltpu.VMEM((RNN_TS, dr), F32),
        ],
        compiler_params=_params("arbitrary", "arbitrary"),
        name="rglru",
    )(proj, proj, cw, cb, wa, ba, wx, bx, lam)


def _first_argmax(vals):
    m = vals[0]
    for v in vals[1:]:
        m = jnp.maximum(m, v)
    idx = jnp.full_like(m, float(len(vals) - 1))
    for k in range(len(vals) - 2, -1, -1):
        idx = jnp.where(vals[k] == m, float(k), idx)
    return m, idx


def _merge_kernel(x_ref, at_ref, rn_ref, ga_ref, gr_ref, wao_ref, wro_ref, wo_ref, g_ref,
                  wr_ref, br_ref, x1_ref, hr_ref, cls_ref):
    d = x_ref.shape[1]
    ya = _dot(at_ref[...], wao_ref[...])
    yr = _dot(rn_ref[...], wro_ref[...])
    merged = _sigmoid(ga_ref[...].astype(F32)) * ya + _sigmoid(gr_ref[...].astype(F32)) * yr
    x1 = x_ref[...] + _dot(merged.astype(BF16), wo_ref[...])
    x1_ref[...] = x1
    h = _rms(x1, g_ref[...])
    hr_ref[:, :d] = h

    h_hi = h.astype(BF16)
    h_lo = (h - h_hi.astype(F32)).astype(BF16)
    wr = wr_ref[...]
    la = _dot(h_hi, wr)
    lb = _dot(h_lo, wr)
    logits = la + pltpu.roll(la, LANES - CLASS_ROWS, 1) + lb
    lt = logits.T[:CLASS_ROWS, :] + br_ref[...]
    g = [lt[k:k + 1, :] for k in range(N_GROUPS)]
    e = [lt[N_GROUPS + k:N_GROUPS + k + 1, :] for k in range(N_EXPERTS)]

    gmax, gi = _first_argmax(g)
    den = jnp.exp(g[0] - gmax)
    for k in range(1, N_GROUPS):
        den = den + jnp.exp(g[k] - gmax)
    gate = 1.0 / den
    sel = []
    for j in range(EXPERTS_PER_GROUP):
        s = e[(N_GROUPS - 1) * EXPERTS_PER_GROUP + j]
        for k in range(N_GROUPS - 2, -1, -1):
            s = jnp.where(gi == float(k), e[k * EXPERTS_PER_GROUP + j], s)
        sel.append(s)
    m1, i1 = _first_argmax(sel)
    sel2 = [jnp.where(i1 == float(j), -jnp.inf, sel[j]) for j in range(EXPERTS_PER_GROUP)]
    m2, i2 = _first_argmax(sel2)
    t = jnp.exp(m2 - m1)
    w1 = gate * (1.0 / (1.0 + t))
    w2 = gate * (t / (1.0 + t))
    first_is_lo = i1 < i2
    a = jnp.minimum(i1, i2)
    b = jnp.maximum(i1, i2)
    w_lo = jnp.where(first_is_lo, w1, w2)
    w_hi = jnp.where(first_is_lo, w2, w1)
    pair = jnp.where(a == 0.0, b - 1.0, jnp.where(a == 1.0, b + 1.0, 5.0))
    cls = gi * float(PAIRS_PER_GROUP) + pair

    n = cls.shape[1]
    cls_ref[...] = jnp.concatenate([cls, jnp.zeros((SUBLANES - 1, n), F32)], axis=0)
    rows = jnp.concatenate([w_lo, w_hi, jnp.zeros((ROUTE_LANES - 2, n), F32)], axis=0)
    hr_ref[:, d:] = rows.T


def _merge(x, attn, rnn, proj, wao, wro, wo, g, wr, br, layer, *, ga_blk, gr_blk):
    t, d = x.shape
    da = attn.shape[1]
    dr = rnn.shape[1]
    tm = ROW_TILE
    return pl.pallas_call(
        _merge_kernel,
        out_shape=(
            jax.ShapeDtypeStruct((t, d), F32),
            jax.ShapeDtypeStruct((t, d + ROUTE_LANES), F32),
            jax.ShapeDtypeStruct((SUBLANES, t), F32),
        ),
        grid=(t // tm,),
        in_specs=[
            pl.BlockSpec((tm, d), lambda i: (i, 0)),
            pl.BlockSpec((tm, da), lambda i: (i, 0)),
            pl.BlockSpec((tm, dr), lambda i: (i, 0)),
            pl.BlockSpec((tm, d), lambda i: (i, ga_blk)),
            pl.BlockSpec((tm, d), lambda i: (i, gr_blk)),
            _layer_spec(wao, layer), _layer_spec(wro, layer), _layer_spec(wo, layer),
            _layer_spec(g, layer), _layer_spec(wr, layer), _layer_spec(br, layer),
        ],
        out_specs=(
            pl.BlockSpec((tm, d), lambda i: (i, 0)),
            pl.BlockSpec((tm, d + ROUTE_LANES), lambda i: (i, 0)),
            pl.BlockSpec((SUBLANES, tm), lambda i: (0, i)),
        ),
        compiler_params=_params("arbitrary"),
        name="merge_router",
    )(x, attn, rnn, proj, proj, wao, wro, wo, g, wr, br)


def _rank_kernel(cls_ref, lt_ref, pos_ref, tinfo_ref, cnt_ref, off_ref, run_ref, *, tm):
    ph = pl.program_id(0)
    bi = pl.program_id(1)
    cls = cls_ref[0:1, :]
    crow = lax.broadcasted_iota(jnp.int32, (CLASS_ROWS, tm), 0).astype(F32)
    member = crow == cls
    onehot = jnp.where(member, 1.0, 0.0)
    block_count = jnp.sum(onehot, axis=1, keepdims=True)

    @pl.when(jnp.logical_and(ph == 0, bi == 0))
    def _():
        cnt_ref[...] = jnp.zeros_like(cnt_ref)

    @pl.when(ph == 0)
    def _():
        cnt_ref[...] += block_count

    @pl.when(jnp.logical_and(ph == 1, bi == 0))
    def _():
        ntile = jnp.floor((cnt_ref[...] + float(MOE_TILE - 1)) * (1.0 / MOE_TILE))
        rid = lax.broadcasted_iota(jnp.int32, (CLASS_ROWS, LANES), 0)
        toff = jnp.zeros((CLASS_ROWS, LANES), F32)
        for c in range(1, CLASS_ROWS):
            toff = toff + jnp.where(rid >= c, ntile[c - 1:c, :], 0.0)
        off_ref[...] = toff * float(MOE_TILE)
        run_ref[...] = jnp.zeros_like(run_ref)
        ti = lax.broadcasted_iota(jnp.int32, (CLASS_ROWS, LANES), 1).astype(F32)
        inside = jnp.where(ti >= toff, jnp.where(ti < toff + ntile, 1.0, 0.0), 0.0)
        c = rid.astype(F32)
        grp = (jnp.where(c >= 6.0, 1.0, 0.0) + jnp.where(c >= 12.0, 1.0, 0.0)
               + jnp.where(c >= 18.0, 1.0, 0.0))
        pair = c - float(PAIRS_PER_GROUP) * grp
        a = jnp.where(pair >= 3.0, 1.0, 0.0) + jnp.where(pair >= 5.0, 1.0, 0.0)
        b = jnp.where(pair < 3.0, pair + 1.0, jnp.where(pair < 5.0, pair - 1.0, 3.0))
        e_lo = jnp.sum(inside * (float(EXPERTS_PER_GROUP) * grp + a), axis=0, keepdims=True)
        e_hi = jnp.sum(inside * (float(EXPERTS_PER_GROUP) * grp + b), axis=0, keepdims=True)
        first = (ti - toff) == 0.0
        tile_c = toff + jnp.where(first, ntile - 1.0, ti - toff - 1.0)
        rows_c = jnp.where(first, cnt_ref[...] - (ntile - 1.0) * float(MOE_TILE), float(MOE_TILE))
        used = jnp.sum(ntile, axis=0, keepdims=True)
        tile = jnp.sum(inside * tile_c, axis=0, keepdims=True)
        tile = tile + jnp.where(ti[0:1, :] >= used, ti[0:1, :], 0.0)
        rows = jnp.sum(inside * rows_c, axis=0, keepdims=True)
        left = cnt_ref[...] - (ti - toff) * float(MOE_TILE)
        tile_rows = jnp.sum(inside * jnp.minimum(left, float(MOE_TILE)), axis=0, keepdims=True)
        info = jnp.concatenate([e_lo, e_hi, rows, used, tile, tile_rows,
                                jnp.zeros((SUBLANES - 6, LANES), F32)], axis=0)
        tinfo_ref[...] = info.astype(jnp.int32)

    @pl.when(ph == 1)
    def _():
        before = _dot(onehot.astype(BF16), lt_ref[...])
        val = before + run_ref[:, 0:1] + off_ref[:, 0:1]
        pos = jnp.sum(jnp.where(member, val, 0.0), axis=0, keepdims=True)
        pos_ref[...] = pos.astype(jnp.int32)
        run_ref[...] += block_count


def _rank(cls, strict_lt):
    t = cls.shape[1]
    tm = RANK_TILE
    return pl.pallas_call(
        functools.partial(_rank_kernel, tm=tm),
        out_shape=(
            jax.ShapeDtypeStruct((1, t), jnp.int32),
            jax.ShapeDtypeStruct((SUBLANES, LANES), jnp.int32),
        ),
        grid=(2, t // tm),
        in_specs=[
            pl.BlockSpec((SUBLANES, tm), lambda ph, bi: (0, bi)),
            pl.BlockSpec((tm, tm), lambda ph, bi: (0, 0)),
        ],
        out_specs=(
            pl.BlockSpec((1, tm), lambda ph, bi: (0, bi * ph)),
            pl.BlockSpec((SUBLANES, LANES), lambda ph, bi: (0, 0)),
        ),
        scratch_shapes=[pltpu.VMEM((CLASS_ROWS, LANES), F32)] * 3,
        compiler_params=_params("arbitrary", "arbitrary"),
        name="rank_tokens",
    )(cls, strict_lt)


def _scatter_kernel(info_ref, pos_ref, h_ref, o_ref, zero_ref, sem, zsem, *, tm):
    tile_groups = MOE_TILE // SUBLANES

    @pl.when(pl.program_id(0) == 0)
    def _():
        zero_ref[...] = jnp.zeros_like(zero_ref)

        def tile_clear(ti):
            dst = o_ref.at[pl.ds(ti * tile_groups, tile_groups)]
            return pltpu.make_async_copy(zero_ref, dst, zsem)

        n_tiles = o_ref.shape[0] // tile_groups
        for ti in range(n_tiles):
            @pl.when(info_ref[5, ti] < MOE_TILE)
            def _():
                tile_clear(ti).start()
        for ti in range(n_tiles):
            @pl.when(info_ref[5, ti] < MOE_TILE)
            def _():
                tile_clear(ti).wait()

    def row_copy(grp, j, p):
        dst = o_ref.at[lax.shift_right_logical(p, 3), pl.ds(jnp.bitwise_and(p, SUBLANES - 1), 1)]
        return pltpu.make_async_copy(h_ref.at[grp, pl.ds(j, 1)], dst, sem)

    def start(grp, c):
        for j in range(SUBLANES):
            row_copy(grp, j, pos_ref[0, 0, grp * SUBLANES + j]).start(priority=j % 2)
        return c

    def wait(r, c):
        row_copy(0, 0, 0).wait()
        return c

    lax.fori_loop(0, tm // SUBLANES, start, 0)
    lax.fori_loop(0, tm, wait, 0, unroll=DMA_UNROLL)


def _scatter_rows(tinfo, pos, rows, n_sorted):
    t, w = rows.shape
    tm = SCATTER_TILE
    return pl.pallas_call(
        functools.partial(_scatter_kernel, tm=tm),
        out_shape=jax.ShapeDtypeStruct((n_sorted // SUBLANES, SUBLANES, w), rows.dtype),
        grid_spec=pltpu.PrefetchScalarGridSpec(
            num_scalar_prefetch=1,
            grid=(t // tm,),
            in_specs=[
                pl.BlockSpec((1, 1, tm), lambda i, info: (i, 0, 0), memory_space=pltpu.SMEM),
                pl.BlockSpec((tm // SUBLANES, SUBLANES, w), lambda i, info: (i, 0, 0)),
            ],
            out_specs=pl.BlockSpec(memory_space=pl.ANY),
            scratch_shapes=[
                pltpu.VMEM((MOE_TILE // SUBLANES, SUBLANES, w), rows.dtype),
                pltpu.SemaphoreType.DMA(()),
                pltpu.SemaphoreType.DMA(()),
            ],
        ),
        compiler_params=_params("arbitrary"),
        name="scatter_rows",
    )(tinfo, pos.reshape(t // tm, 1, tm), rows.reshape(t // SUBLANES, SUBLANES, w))


def _ffn_kernel(info_ref, x_ref, wgl_ref, wul_ref, wdl_ref, wgh_ref, wuh_ref, wdh_ref, o_ref,
                wg_s, wu_s, wd_s):
    i = pl.program_id(0)
    tm, d = o_ref.shape
    half = tm // 2
    live = i < info_ref[3, 0]
    prev = jnp.maximum(i - 1, 0)
    for k, (wg, wu, wd) in enumerate(((wgl_ref, wul_ref, wdl_ref), (wgh_ref, wuh_ref, wdh_ref))):
        changed = jnp.logical_or(i == 0, info_ref[k, i] != info_ref[k, prev])

        @pl.when(jnp.logical_and(live, changed))
        def _():
            wg_s[k] = wg[...].astype(BF16)
            wu_s[k] = wu[...].astype(BF16)
            wd_s[k] = wd[...].astype(BF16)

    def ffn(rows):
        xt = x_ref[rows, :]
        xb = xt[:, :d].astype(BF16)
        y = None
        for k in range(2):
            gte = _dot(xb, wg_s[k])
            up = _dot(xb, wu_s[k])
            he = (gte * _sigmoid(gte)) * up
            part = xt[:, d + k:d + k + 1] * _dot(he.astype(BF16), wd_s[k])
            y = part if y is None else y + part
        o_ref[rows, :] = y

    @pl.when(jnp.logical_not(live))
    def _():
        o_ref[...] = jnp.zeros_like(o_ref)

    @pl.when(jnp.logical_and(live, info_ref[2, i] > half))
    def _():
        ffn(slice(0, tm))

    @pl.when(jnp.logical_and(live, info_ref[2, i] <= half))
    def _():
        ffn(slice(0, half))
        o_ref[half:, :] = jnp.zeros((tm - half, d), F32)


def _expert_ffn(tinfo, xs, wg, wu, wd, layer):
    npad, w = xs.shape
    _, _, d, de = wg.shape
    tm = MOE_TILE

    def step(i, info):
        return jnp.minimum(i, info[3, 0] - 1)

    w_in = lambda row: pl.BlockSpec((None, None, d, de),
                                    lambda i, info: (layer, info[row, step(i, info)], 0, 0))
    w_out = lambda row: pl.BlockSpec((None, None, de, d),
                                     lambda i, info: (layer, info[row, step(i, info)], 0, 0))
    return pl.pallas_call(
        _ffn_kernel,
        out_shape=jax.ShapeDtypeStruct((npad, d), F32),
        grid_spec=pltpu.PrefetchScalarGridSpec(
            num_scalar_prefetch=1,
            grid=(npad // tm,),
            in_specs=[
                pl.BlockSpec((tm, w), lambda i, info: (info[4, step(i, info)], 0)),
                w_in(0), w_in(0), w_out(0), w_in(1), w_in(1), w_out(1),
            ],
            out_specs=pl.BlockSpec((tm, d), lambda i, info: (info[4, i], 0)),
            scratch_shapes=[
                pltpu.VMEM((2, d, de), BF16),
                pltpu.VMEM((2, d, de), BF16),
                pltpu.VMEM((2, de, d), BF16),
            ],
        ),
        compiler_params=_params("arbitrary"),
        name="expert_ffn",
    )(tinfo, xs, wg, wu, wd, wg, wu, wd)


def _ple_kernel(pos_ref, posn_ref, x_ref, p_ref, ys_ref, g_ref, wg_ref, wp_ref, gn_ref, *rest,
                tm, nsteps, col_map):
    if col_map is None:
        o_ref, ybuf, sem = rest
    else:
        win_ref, o_ref, proj_ref, ybuf, sem = rest
    i = pl.program_id(0)
    slot = lax.rem(i, 2)
    d = x_ref.shape[1]

    def row_copy(grp, j, p, s):
        src = ys_ref.at[lax.shift_right_logical(p, 3), pl.ds(jnp.bitwise_and(p, SUBLANES - 1), 1)]
        return pltpu.make_async_copy(src, ybuf.at[s, grp, pl.ds(j, 1)], sem.at[s])

    def wait_slot(s):
        def wait(r, c):
            row_copy(0, 0, 0, s).wait()
            return c
        lax.fori_loop(0, tm, wait, 0, unroll=DMA_UNROLL)

    @pl.when(i == 0)
    def _():
        def start(grp, c):
            for j in range(SUBLANES):
                row_copy(grp, j, pos_ref[0, 0, grp * SUBLANES + j], 0).start()
            return c
        lax.fori_loop(0, tm // SUBLANES, start, 0)

    wait_slot(slot)

    n_stage = 1 if col_map is None else (len(col_map) + 1) // 2
    per_stage = -(-(tm // SUBLANES) // n_stage)

    def prefetch(stage):
        for grp in range(stage * per_stage, min((stage + 1) * per_stage, tm // SUBLANES)):
            for j in range(SUBLANES):
                row_copy(grp, j, posn_ref[0, 0, grp * SUBLANES + j], 1 - slot).start(priority=j % 2)

    x2 = x_ref[...] + ybuf[slot].reshape(tm, d)
    if col_map is None:
        prefetch(0)
    gate = _sigmoid(_dot(_rms(x2, g_ref[...]).astype(BF16), wg_ref[...]))
    x3 = x2 + gate * _dot(p_ref[...].astype(BF16), wp_ref[...])
    if col_map is None:
        o_ref[...] = _rms(x3, gn_ref[...])
    else:
        o_ref[...] = x3
        h = _rms(x3, gn_ref[...]).astype(BF16)
        for stage, (src, dst) in enumerate(col_map):
            prefetch(stage)
            proj_ref[:, dst:dst + COL_CHUNK] = _dot(h, win_ref[:, src:src + COL_CHUNK]).astype(BF16)

    @pl.when(i == nsteps - 1)
    def _():
        wait_slot(1 - slot)


def _ple(pos, x, p, ys, g, wg, wp, layer, *, g_next, next_layer, w_in=None, col_map=None):
    t, d = x.shape
    dp = p.shape[2]
    tm = PLE_TILE
    nsteps = t // tm
    pos3 = pos.reshape(nsteps, 1, tm)
    in_specs = [
        pl.BlockSpec((1, 1, tm), lambda i: (i, 0, 0), memory_space=pltpu.SMEM),
        pl.BlockSpec((1, 1, tm), lambda i: (jnp.minimum(i + 1, nsteps - 1), 0, 0),
                     memory_space=pltpu.SMEM),
        pl.BlockSpec((tm, d), lambda i: (i, 0)),
        pl.BlockSpec((None, tm, dp), lambda i: (layer, i, 0)),
        pl.BlockSpec(memory_space=pl.ANY),
        _layer_spec(g, layer), _layer_spec(wg, layer), _layer_spec(wp, layer),
        _layer_spec(g_next, next_layer),
    ]
    args = [pos3, pos3, x, p, ys.reshape(-1, SUBLANES, d), g, wg, wp, g_next]
    out_shape = jax.ShapeDtypeStruct((t, d), F32)
    out_specs = pl.BlockSpec((tm, d), lambda i: (i, 0))
    if col_map is not None:
        n = w_in.shape[2]
        in_specs.append(_layer_spec(w_in, next_layer))
        args.append(w_in)
        out_shape = (out_shape, jax.ShapeDtypeStruct((t, n), BF16))
        out_specs = (out_specs, pl.BlockSpec((tm, n), lambda i: (i, 0)))
    return pl.pallas_call(
        functools.partial(_ple_kernel, tm=tm, nsteps=nsteps, col_map=col_map),
        out_shape=out_shape,
        grid=(nsteps,),
        in_specs=in_specs,
        out_specs=out_specs,
        scratch_shapes=[pltpu.VMEM((2, tm // SUBLANES, SUBLANES, d), F32),
                        pltpu.SemaphoreType.DMA((2,))],
        compiler_params=_params("arbitrary"),
        name="gather_ple",
    )(*args)


def _block_diag(w):
    l, g, n, _ = w.shape
    eye = jnp.eye(g, dtype=w.dtype)
    return (eye[None, :, None, :, None] * w[:, :, :, None, :]).reshape(l, g * n, g * n)


def kernel(x, p, norm_mix, w_in, conv_w, conv_b, w_rg_a, b_rg_a, w_rg_x, b_rg_x, rg_lambda,
           w_attn_o, w_rnn_o, w_out, norm_moe, w_router_group, b_router_group, w_router_expert,
           b_router_expert, w_exp_gate, w_exp_up, w_exp_down, norm_ple, w_ple_gate, w_ple_proj,
           norm_final):
    bsz, seq, d = x.shape
    depth = w_in.shape[0]
    t = bsz * seq
    d_attn = N_HEADS * HEAD_DIM
    d_rnn = conv_w.shape[2]
    assert d_attn == d_rnn and d == 2 * d_attn and d_attn == COL_CHUNK
    assert t % ROW_TILE == 0 and seq % (2 * ATT_TILE) == 0 and seq % RNN_TS == 0
    assert t % RANK_TILE == 0
    assert t % SCATTER_TILE == 0 and t % PLE_TILE == 0

    n_in = w_in.shape[2]
    gates_at = 3 * d_attn + 2 * d_rnn
    col_map = tuple((src, (src - gates_at) % n_in) for src in range(0, n_in, COL_CHUNK))
    ga_blk, gr_blk = 0, 1
    q_col = 2 * d
    q_blk, k_blk, v_blk = (q_col // LANES, (q_col + d_attn) // LANES, (q_col + 2 * d_attn) // LANES)
    xr_blk, xg_blk = (q_col + 3 * d_attn) // d_rnn, (q_col + 3 * d_attn + d_rnn) // d_rnn

    tri_incl = jnp.tril(jnp.ones((ATT_TILE, ATT_TILE), F32)).astype(BF16)
    strict_lt = jnp.triu(jnp.ones((RANK_TILE, RANK_TILE), F32), k=1).astype(BF16)

    n_tiles = (t + N_CLASSES * (MOE_TILE - 1)) // MOE_TILE
    assert n_tiles <= LANES
    n_sorted = n_tiles * MOE_TILE

    vec = lambda v: v.reshape(depth, 1, -1)
    w_in_b = w_in.astype(BF16)
    wa_bd = _block_diag(w_rg_a).astype(BF16)
    wx_bd = _block_diag(w_rg_x).astype(BF16)
    w_r = jnp.concatenate([w_router_group, w_router_expert], axis=2)
    w_r = jnp.pad(w_r, ((0, 0), (0, 0), (0, CLASS_ROWS - w_r.shape[2])))
    w_r_hi = w_r.astype(BF16)
    w_r_lo = (w_r - w_r_hi.astype(F32)).astype(BF16)
    w_r_cat = jnp.pad(jnp.concatenate([w_r_hi, w_r_lo], axis=2),
                      ((0, 0), (0, 0), (0, LANES - 2 * CLASS_ROWS)))
    b_r = jnp.concatenate([b_router_group, b_router_expert], axis=1)
    b_r = jnp.pad(b_r, ((0, 0), (0, CLASS_ROWS - b_r.shape[1]))).reshape(depth, CLASS_ROWS, 1)
    wao_b, wro_b, wo_b = w_attn_o.astype(BF16), w_rnn_o.astype(BF16), w_out.astype(BF16)
    wpg_b, wpp_b = w_ple_gate.astype(BF16), w_ple_proj.astype(BF16)
    g_mix, g_moe, g_ple = vec(norm_mix), vec(norm_moe), vec(norm_ple)
    g_fin = norm_final.reshape(1, 1, -1)
    p3 = p.reshape(depth, t, -1)

    x2d = x.reshape(t, d)
    proj = _inproj(x2d, g_mix, w_in_b, 0, col_map)
    for i in range(depth):
        proj3 = proj.reshape(bsz, seq, -1)
        attn = _attention(proj3, tri_incl, q_blk=q_blk, k_blk=k_blk, v_blk=v_blk)
        rnn = _rglru(proj3, conv_w, vec(conv_b), wa_bd, vec(b_rg_a), wx_bd, vec(b_rg_x),
                     vec(rg_lambda), i, xr_blk=xr_blk, xg_blk=xg_blk)
        x1, routed, cls = _merge(
            x2d, attn.reshape(t, d_attn), rnn.reshape(t, d_rnn), proj, wao_b, wro_b, wo_b,
            g_moe, w_r_cat, b_r, i, ga_blk=ga_blk, gr_blk=gr_blk)
        pos, tinfo = _rank(cls, strict_lt)
        pos = pos.reshape(t)
        xs = _scatter_rows(tinfo, pos, routed, n_sorted)
        ys = _expert_ffn(tinfo, xs.reshape(n_sorted, -1), w_exp_gate, w_exp_up, w_exp_down, i)
        if i + 1 < depth:
            x2d, proj = _ple(pos, x1, p3, ys, g_ple, wpg_b, wpp_b, i, g_next=g_mix,
                             next_layer=i + 1, w_in=w_in_b, col_map=col_map)
        else:
            x2d = _ple(pos, x1, p3, ys, g_ple, wpg_b, wpp_b, i, g_next=g_fin, next_layer=0)
    return x2d.reshape(bsz, seq, d)
```

```python
import functools

import jax
import jax.numpy as jnp
from jax import lax
from jax.experimental import pallas as pl
from jax.experimental.pallas import tpu as pltpu

F32 = jnp.float32
BF16 = jnp.bfloat16
EPS = 1e-6

N_HEADS = 8
HEAD_DIM = 64
N_GROUPS = 4
EXPERTS_PER_GROUP = 4
N_EXPERTS = N_GROUPS * EXPERTS_PER_GROUP
PAIRS_PER_GROUP = 6
N_CLASSES = N_GROUPS * PAIRS_PER_GROUP
CLASS_ROWS = 32
CONV_WIDTH = 4
RG_C = 8.0

LANES = 128
SUBLANES = 8
VMEM_LIMIT = 56 * 1024 * 1024

ROW_TILE = 512
RANK_TILE = 1024
ATT_TILE = 256
RNN_TS = 512
MOE_TILE = 512
SCATTER_TILE = 1024
PLE_TILE = 512
ROUTE_LANES = 128
DMA_UNROLL = 8
COL_CHUNK = 512

_NT = (((1,), (1,)), ((), ()))
LOG2E = 1.4426950408889634
UNDERFLOW_EXPONENT = 151.0 / LOG2E


def _params(*sem):
    return pltpu.CompilerParams(dimension_semantics=sem, vmem_limit_bytes=VMEM_LIMIT)


def _layer_spec(arr, layer):
    nd = arr.ndim
    return pl.BlockSpec((None,) + arr.shape[1:], lambda *_: (layer,) + (0,) * (nd - 1))


def _rms(x, g):
    ms = jnp.mean(x * x, axis=-1, keepdims=True)
    return x * lax.rsqrt(ms + EPS) * g


def _sigmoid(x):
    return 1.0 / (1.0 + jnp.exp(-x))


def _dot(a, b):
    return jnp.dot(a, b, preferred_element_type=F32)


def _in_projection(x, g, w_ref, o_ref, col_map):
    h = _rms(x, g).astype(BF16)
    for src, dst in col_map:
        o_ref[:, dst:dst + COL_CHUNK] = _dot(h, w_ref[:, src:src + COL_CHUNK]).astype(BF16)


def _inproj_kernel(x_ref, g_ref, w_ref, o_ref, *, col_map):
    _in_projection(x_ref[...], g_ref[...], w_ref, o_ref, col_map)


def _inproj(x, g, w, layer, col_map):
    t, d = x.shape
    n = w.shape[2]
    return pl.pallas_call(
        functools.partial(_inproj_kernel, col_map=col_map),
        out_shape=jax.ShapeDtypeStruct((t, n), BF16),
        grid=(t // ROW_TILE,),
        in_specs=[
            pl.BlockSpec((ROW_TILE, d), lambda i: (i, 0)),
            _layer_spec(g, layer),
            _layer_spec(w, layer),
        ],
        out_specs=pl.BlockSpec((ROW_TILE, n), lambda i: (i, 0)),
        compiler_params=_params("arbitrary"),
        name="inproj",
    )(x, g, w)


def _attn_tile(q_heads, kb, vb, u, acc_ref, carry_ref, rows, causal):
    tk = kb.shape[0]
    lane = lax.broadcasted_iota(jnp.int32, (1, LANES), 1)
    first_head = lane < HEAD_DIM
    vz = jnp.zeros_like(vb)
    v_heads = (jnp.where(first_head, vb, vz), jnp.where(first_head, vz, vb))
    pv = None
    for h in range(2):
        z = lax.dot_general(q_heads[h], kb, _NT, preferred_element_type=F32)
        sp = jnp.maximum(z, 0.0) + jnp.log(1.0 + jnp.exp2(jnp.abs(z) * (-LOG2E)))
        if causal is not None:
            sp = jnp.where(causal, sp, 0.0)
        suffix = _dot(sp.astype(BF16), u)
        carry = carry_ref[h, rows, :]
        w = jnp.exp2((z - suffix - jnp.tile(carry, (1, tk // LANES))) * LOG2E)
        if causal is not None:
            w = jnp.where(causal, w, 0.0)
        d = _dot(w.astype(BF16), v_heads[h])
        pv = d if pv is None else pv + d
        carry_ref[h, rows, :] = carry + suffix[:, 0:1]
    acc_ref[rows, :] += pv


def _attn_kernel(q_ref, k_ref, v_ref, u_ref, o_ref, acc_ref, carry_ref, *, tk):
    s = q_ref.shape[0]
    nb = s // tk
    lane = lax.broadcasted_iota(jnp.int32, (1, LANES), 1)
    first_head = lane < HEAD_DIM
    u = u_ref[...]
    causal = (lax.broadcasted_iota(jnp.int32, (2 * tk, tk), 1)
              < lax.broadcasted_iota(jnp.int32, (2 * tk, tk), 0))
    acc_ref[...] = jnp.zeros_like(acc_ref)
    carry_ref[...] = jnp.zeros_like(carry_ref)

    def tile(r0, nrows, kb, mask):
        rows = pl.ds(pl.multiple_of(r0, tk), nrows)
        ks = pl.multiple_of(kb * tk, tk)
        q = q_ref[rows, :] * jnp.asarray(HEAD_DIM ** -0.5, BF16)
        qz = jnp.zeros_like(q)
        q_heads = (jnp.where(first_head, q, qz), jnp.where(first_head, qz, q))
        _attn_tile(q_heads, k_ref[pl.ds(ks, tk), :], v_ref[pl.ds(ks, tk), :],
                   u, acc_ref, carry_ref, rows, mask)

    tile((nb - 1) * tk, tk, nb - 1, causal[:tk])

    def key_block(i, low):
        kb = nb - 2 - i
        tile(kb * tk, 2 * tk, kb, causal)
        done = carry_ref[:, pl.ds(pl.multiple_of((kb + 1) * tk, tk), tk), :]
        return jnp.minimum(low, jnp.min(done.reshape(-1, SUBLANES, LANES), axis=0))

    low = lax.fori_loop(0, nb - 2, key_block, jnp.full((SUBLANES, LANES), jnp.inf, F32))
    tile(0, 2 * tk, 0, causal)

    def query_block(j, c):
        block_rows = pl.ds(pl.multiple_of(j * tk, tk), tk)

        def more(state):
            kb, min_carry = state
            return jnp.logical_and(kb >= 0, min_carry < UNDERFLOW_EXPONENT)

        def body(state):
            kb, _ = state
            tile(j * tk, tk, kb, None)
            return kb - 1, jnp.min(carry_ref[:, block_rows, :])

        lax.while_loop(more, body, (j - 2, jnp.min(carry_ref[:, block_rows, :])))
        return c

    @pl.when(jnp.min(low) < UNDERFLOW_EXPONENT)
    def _():
        lax.fori_loop(2, nb, query_block, 0)

    o_ref[...] = acc_ref[...].astype(BF16)


def _attention(proj, u, *, q_blk, k_blk, v_blk):
    b, s, _ = proj.shape
    hp = N_HEADS * HEAD_DIM // LANES
    cols = lambda blk: pl.BlockSpec((None, s, LANES), lambda bi, hi: (bi, 0, blk + hi))
    return pl.pallas_call(
        functools.partial(_attn_kernel, tk=ATT_TILE),
        out_shape=jax.ShapeDtypeStruct((b, s, N_HEADS * HEAD_DIM), BF16),
        grid=(b, hp),
        in_specs=[cols(q_blk), cols(k_blk), cols(v_blk),
                  pl.BlockSpec((ATT_TILE, ATT_TILE), lambda bi, hi: (0, 0))],
        out_specs=pl.BlockSpec((None, s, LANES), lambda bi, hi: (bi, 0, hi)),
        scratch_shapes=[
            pltpu.VMEM((s, LANES), F32),
            pltpu.VMEM((2, s, LANES), F32),
        ],
        compiler_params=_params("arbitrary", "arbitrary"),
        name="sb_attention",
    )(proj, proj, proj, u)


def _rglru_kernel(xr_ref, xg_ref, cw_ref, cb_ref, wa_ref, ba_ref, wx_ref, bx_ref, lam_ref,
                  o_ref, tail_ref, h_ref, a_scr, b_scr, *, ts):
    @pl.when(pl.program_id(1) == 0)
    def _():
        tail_ref[...] = jnp.zeros_like(tail_ref)
        h_ref[...] = jnp.zeros_like(h_ref)

    groups = ts // SUBLANES
    dr = xr_ref.shape[1]
    xr = xr_ref[...].astype(F32).reshape(groups, SUBLANES, dr)
    tail = tail_ref[...]
    tail_ref[...] = xr[groups - 1]
    in_group = lax.broadcasted_iota(jnp.int32, xr.shape, 1)
    xc = cb_ref[...].reshape(1, 1, dr)
    for j in range(CONV_WIDTH):
        back = CONV_WIDTH - 1 - j
        if back == 0:
            shifted = xr
        else:
            rot = pltpu.roll(xr, back, 1)
            rot_before = jnp.concatenate(
                [pltpu.roll(tail, back, 0).reshape(1, SUBLANES, dr), rot[:groups - 1]], axis=0)
            shifted = jnp.where(in_group >= back, rot, rot_before)
        xc = xc + cw_ref[j:j + 1, :].reshape(1, 1, dr) * shifted
    xc = xc.reshape(ts, dr)
    xcb = xc.astype(BF16)
    r = 0.5 + 0.5 * jnp.tanh(0.5 * (_dot(xcb, wa_ref[...]) + ba_ref[...]))
    ig = 0.5 + 0.5 * jnp.tanh(0.5 * (_dot(xcb, wx_ref[...]) + bx_ref[...]))
    lam = lam_ref[...]
    softplus_neg_lam = jnp.maximum(-lam, 0.0) + jnp.log(1.0 + jnp.exp(-jnp.abs(lam)))
    log_a = (-RG_C * softplus_neg_lam) * r
    a = jnp.exp(log_a)
    v = 1.0 - a * a
    root = jnp.where(v > 0.0, v * lax.rsqrt(v), 0.0)
    b = root * (ig * xc)

    a = a.reshape(groups, SUBLANES, dr)
    b = b.reshape(groups, SUBLANES, dr)
    for s in (1, 2, 4):
        has_prev = in_group >= s
        b = jnp.where(has_prev, a * pltpu.roll(b, s, 1) + b, b)
        a = jnp.where(has_prev, a * pltpu.roll(a, s, 1), a)
    a_scr[...] = a.reshape(ts, dr)
    b_scr[...] = b.reshape(ts, dr)

    def group(gidx, h):
        r0 = pl.multiple_of(gidx * SUBLANES, SUBLANES)
        hg = a_scr[pl.ds(r0, SUBLANES), :] * h + b_scr[pl.ds(r0, SUBLANES), :]
        b_scr[pl.ds(r0, SUBLANES), :] = hg
        return hg[SUBLANES - 1:, :]

    h_ref[...] = lax.fori_loop(0, ts // SUBLANES, group, h_ref[...], unroll=4)
    xg = xg_ref[...].astype(F32)
    gelu = 0.5 * xg * (1.0 + jnp.tanh(0.7978845608028654 * (xg + 0.044715 * (xg * xg * xg))))
    o_ref[...] = (b_scr[...] * gelu).astype(BF16)


def _rglru(proj, cw, cb, wa, ba, wx, bx, lam, layer, *, xr_blk, xg_blk):
    b, s, _ = proj.shape
    dr = cw.shape[2]
    return pl.pallas_call(
        functools.partial(_rglru_kernel, ts=RNN_TS),
        out_shape=jax.ShapeDtypeStruct((b, s, dr), BF16),
        grid=(b, s // RNN_TS),
        in_specs=[
            pl.BlockSpec((None, RNN_TS, dr), lambda bi, si: (bi, si, xr_blk)),
            pl.BlockSpec((None, RNN_TS, dr), lambda bi, si: (bi, si, xg_blk)),
            _layer_spec(cw, layer), _layer_spec(cb, layer),
            _layer_spec(wa, layer), _layer_spec(ba, layer),
            _layer_spec(wx, layer), _layer_spec(bx, layer),
            _layer_spec(lam, layer),
        ],
        out_specs=pl.BlockSpec((None, RNN_TS, dr), lambda bi, si: (bi, si, 0)),
        scratch_shapes=[
            pltpu.VMEM((SUBLANES, dr), F32),
            pltpu.VMEM((1, dr), F32),
            pltpu.VMEM((RNN_TS, dr), F32),
            pltpu.VMEM((RNN_TS, dr), F32),
        ],
        compiler_params=_params("arbitrary", "arbitrary"),
        name="rglru",
    )(proj, proj, cw, cb, wa, ba, wx, bx, lam)


def _first_argmax(vals):
    m = vals[0]
    for v in vals[1:]:
        m = jnp.maximum(m, v)
    idx = jnp.full_like(m, float(len(vals) - 1))
    for k in range(len(vals) - 2, -1, -1):
        idx = jnp.where(vals[k] == m, float(k), idx)
    return m, idx


def _merge_kernel(x_ref, at_ref, rn_ref, ga_ref, gr_ref, wao_ref, wro_ref, wo_ref, g_ref,
                  wr_ref, br_ref, x1_ref, hr_ref, cls_ref):
    d = x_ref.shape[1]
    ya = _dot(at_ref[...], wao_ref[...])
    yr = _dot(rn_ref[...], wro_ref[...])
    merged = _sigmoid(ga_ref[...].astype(F32)) * ya + _sigmoid(gr_ref[...].astype(F32)) * yr
    x1 = x_ref[...] + _dot(merged.astype(BF16), wo_ref[...])
    x1_ref[...] = x1
    h = _rms(x1, g_ref[...])
    hr_ref[:, :d] = h

    h_hi = h.astype(BF16)
    h_lo = (h - h_hi.astype(F32)).astype(BF16)
    wr = wr_ref[...]
    la = _dot(h_hi, wr)
    lb = _dot(h_lo, wr)
    logits = la + pltpu.roll(la, LANES - CLASS_ROWS, 1) + lb
    lt = logits.T[:CLASS_ROWS, :] + br_ref[...]
    g = [lt[k:k + 1, :] for k in range(N_GROUPS)]
    e = [lt[N_GROUPS + k:N_GROUPS + k + 1, :] for k in range(N_EXPERTS)]

    gmax, gi = _first_argmax(g)
    den = jnp.exp(g[0] - gmax)
    for k in range(1, N_GROUPS):
        den = den + jnp.exp(g[k] - gmax)
    gate = 1.0 / den
    sel = []
    for j in range(EXPERTS_PER_GROUP):
        s = e[(N_GROUPS - 1) * EXPERTS_PER_GROUP + j]
        for k in range(N_GROUPS - 2, -1, -1):
            s = jnp.where(gi == float(k), e[k * EXPERTS_PER_GROUP + j], s)
        sel.append(s)
    m1, i1 = _first_argmax(sel)
    sel2 = [jnp.where(i1 == float(j), -jnp.inf, sel[j]) for j in range(EXPERTS_PER_GROUP)]
    m2, i2 = _first_argmax(sel2)
    t = jnp.exp(m2 - m1)
    w1 = gate * (1.0 / (1.0 + t))
    w2 = gate * (t / (1.0 + t))
    first_is_lo = i1 < i2
    a = jnp.minimum(i1, i2)
    b = jnp.maximum(i1, i2)
    w_lo = jnp.where(first_is_lo, w1, w2)
    w_hi = jnp.where(first_is_lo, w2, w1)
    pair = jnp.where(a == 0.0, b - 1.0, jnp.where(a == 1.0, b + 1.0, 5.0))
    cls = gi * float(PAIRS_PER_GROUP) + pair

    n = cls.shape[1]
    cls_ref[...] = jnp.concatenate([cls, jnp.zeros((SUBLANES - 1, n), F32)], axis=0)
    rows = jnp.concatenate([w_lo, w_hi, jnp.zeros((ROUTE_LANES - 2, n), F32)], axis=0)
    hr_ref[:, d:] = rows.T


def _merge(x, attn, rnn, proj, wao, wro, wo, g, wr, br, layer, *, ga_blk, gr_blk):
    t, d = x.shape
    da = attn.shape[1]
    dr = rnn.shape[1]
    tm = ROW_TILE
    return pl.pallas_call(
        _merge_kernel,
        out_shape=(
            jax.ShapeDtypeStruct((t, d), F32),
            jax.ShapeDtypeStruct((t, d + ROUTE_LANES), F32),
            jax.ShapeDtypeStruct((SUBLANES, t), F32),
        ),
        grid=(t // tm,),
        in_specs=[
            pl.BlockSpec((tm, d), lambda i: (i, 0)),
            pl.BlockSpec((tm, da), lambda i: (i, 0)),
            pl.BlockSpec((tm, dr), lambda i: (i, 0)),
            pl.BlockSpec((tm, d), lambda i: (i, ga_blk)),
            pl.BlockSpec((tm, d), lambda i: (i, gr_blk)),
            _layer_spec(wao, layer), _layer_spec(wro, layer), _layer_spec(wo, layer),
            _layer_spec(g, layer), _layer_spec(wr, layer), _layer_spec(br, layer),
        ],
        out_specs=(
            pl.BlockSpec((tm, d), lambda i: (i, 0)),
            pl.BlockSpec((tm, d + ROUTE_LANES), lambda i: (i, 0)),
            pl.BlockSpec((SUBLANES, tm), lambda i: (0, i)),
        ),
        compiler_params=_params("arbitrary"),
        name="merge_router",
    )(x, attn, rnn, proj, proj, wao, wro, wo, g, wr, br)


def _rank_kernel(cls_ref, lt_ref, pos_ref, tinfo_ref, cnt_ref, off_ref, run_ref, *, tm):
    ph = pl.program_id(0)
    bi = pl.program_id(1)
    cls = cls_ref[0:1, :]
    crow = lax.broadcasted_iota(jnp.int32, (CLASS_ROWS, tm), 0).astype(F32)
    member = crow == cls
    onehot = jnp.where(member, 1.0, 0.0)
    block_count = jnp.sum(onehot, axis=1, keepdims=True)

    @pl.when(jnp.logical_and(ph == 0, bi == 0))
    def _():
        cnt_ref[...] = jnp.zeros_like(cnt_ref)

    @pl.when(ph == 0)
    def _():
        cnt_ref[...] += block_count

    @pl.when(jnp.logical_and(ph == 1, bi == 0))
    def _():
        ntile = jnp.floor((cnt_ref[...] + float(MOE_TILE - 1)) * (1.0 / MOE_TILE))
        rid = lax.broadcasted_iota(jnp.int32, (CLASS_ROWS, LANES), 0)
        toff = jnp.zeros((CLASS_ROWS, LANES), F32)
        for c in range(1, CLASS_ROWS):
            toff = toff + jnp.where(rid >= c, ntile[c - 1:c, :], 0.0)
        off_ref[...] = toff * float(MOE_TILE)
        run_ref[...] = jnp.zeros_like(run_ref)
        ti = lax.broadcasted_iota(jnp.int32, (CLASS_ROWS, LANES), 1).astype(F32)
        inside = jnp.where(ti >= toff, jnp.where(ti < toff + ntile, 1.0, 0.0), 0.0)
        c = rid.astype(F32)
        grp = (jnp.where(c >= 6.0, 1.0, 0.0) + jnp.where(c >= 12.0, 1.0, 0.0)
               + jnp.where(c >= 18.0, 1.0, 0.0))
        pair = c - float(PAIRS_PER_GROUP) * grp
        a = jnp.where(pair >= 3.0, 1.0, 0.0) + jnp.where(pair >= 5.0, 1.0, 0.0)
        b = jnp.where(pair < 3.0, pair + 1.0, jnp.where(pair < 5.0, pair - 1.0, 3.0))
        e_lo = jnp.sum(inside * (float(EXPERTS_PER_GROUP) * grp + a), axis=0, keepdims=True)
        e_hi = jnp.sum(inside * (float(EXPERTS_PER_GROUP) * grp + b), axis=0, keepdims=True)
        first = (ti - toff) == 0.0
        tile_c = toff + jnp.where(first, ntile - 1.0, ti - toff - 1.0)
        rows_c = jnp.where(first, cnt_ref[...] - (ntile - 1.0) * float(MOE_TILE), float(MOE_TILE))
        used = jnp.sum(ntile, axis=0, keepdims=True)
        tile = jnp.sum(inside * tile_c, axis=0, keepdims=True)
        tile = tile + jnp.where(ti[0:1, :] >= used, ti[0:1, :], 0.0)
        rows = jnp.sum(inside * rows_c, axis=0, keepdims=True)
        left = cnt_ref[...] - (ti - toff) * float(MOE_TILE)
        tile_rows = jnp.sum(inside * jnp.minimum(left, float(MOE_TILE)), axis=0, keepdims=True)
        info = jnp.concatenate([e_lo, e_hi, rows, used, tile, tile_rows,
                                jnp.zeros((SUBLANES - 6, LANES), F32)], axis=0)
        tinfo_ref[...] = info.astype(jnp.int32)

    @pl.when(ph == 1)
    def _():
        before = _dot(onehot.astype(BF16), lt_ref[...])
        val = before + run_ref[:, 0:1] + off_ref[:, 0:1]
        pos = jnp.sum(jnp.where(member, val, 0.0), axis=0, keepdims=True)
        pos_ref[...] = pos.astype(jnp.int32)
        run_ref[...] += block_count


def _rank(cls, strict_lt):
    t = cls.shape[1]
    tm = RANK_TILE
    return pl.pallas_call(
        functools.partial(_rank_kernel, tm=tm),
        out_shape=(
            jax.ShapeDtypeStruct((1, t), jnp.int32),
            jax.ShapeDtypeStruct((SUBLANES, LANES), jnp.int32),
        ),
        grid=(2, t // tm),
        in_specs=[
            pl.BlockSpec((SUBLANES, tm), lambda ph, bi: (0, bi)),
            pl.BlockSpec((tm, tm), lambda ph, bi: (0, 0)),
        ],
        out_specs=(
            pl.BlockSpec((1, tm), lambda ph, bi: (0, bi * ph)),
            pl.BlockSpec((SUBLANES, LANES), lambda ph, bi: (0, 0)),
        ),
        scratch_shapes=[pltpu.VMEM((CLASS_ROWS, LANES), F32)] * 3,
        compiler_params=_params("arbitrary", "arbitrary"),
        name="rank_tokens",
    )(cls, strict_lt)


def _scatter_kernel(info_ref, pos_ref, h_ref, o_ref, zero_ref, sem, zsem, *, tm):
    tile_groups = MOE_TILE // SUBLANES

    @pl.when(pl.program_id(0) == 0)
    def _():
        zero_ref[...] = jnp.zeros_like(zero_ref)

        def tile_clear(ti):
            dst = o_ref.at[pl.ds(ti * tile_groups, tile_groups)]
            return pltpu.make_async_copy(zero_ref, dst, zsem)

        n_tiles = o_ref.shape[0] // tile_groups
        for ti in range(n_tiles):
            @pl.when(info_ref[5, ti] < MOE_TILE)
            def _():
                tile_clear(ti).start()
        for ti in range(n_tiles):
            @pl.when(info_ref[5, ti] < MOE_TILE)
            def _():
                tile_clear(ti).wait()

    def row_copy(grp, j, p):
        dst = o_ref.at[lax.shift_right_logical(p, 3), pl.ds(jnp.bitwise_and(p, SUBLANES - 1), 1)]
        return pltpu.make_async_copy(h_ref.at[grp, pl.ds(j, 1)], dst, sem)

    def start(grp, c):
        for j in range(SUBLANES):
            row_copy(grp, j, pos_ref[0, 0, grp * SUBLANES + j]).start(priority=j % 2)
        return c

    def wait(r, c):
        row_copy(0, 0, 0).wait()
        return c

    lax.fori_loop(0, tm // SUBLANES, start, 0)
    lax.fori_loop(0, tm, wait, 0, unroll=DMA_UNROLL)


def _scatter_rows(tinfo, pos, rows, n_sorted):
    t, w = rows.shape
    tm = SCATTER_TILE
    return pl.pallas_call(
        functools.partial(_scatter_kernel, tm=tm),
        out_shape=jax.ShapeDtypeStruct((n_sorted // SUBLANES, SUBLANES, w), rows.dtype),
        grid_spec=pltpu.PrefetchScalarGridSpec(
            num_scalar_prefetch=1,
            grid=(t // tm,),
            in_specs=[
                pl.BlockSpec((1, 1, tm), lambda i, info: (i, 0, 0), memory_space=pltpu.SMEM),
                pl.BlockSpec((tm // SUBLANES, SUBLANES, w), lambda i, info: (i, 0, 0)),
            ],
            out_specs=pl.BlockSpec(memory_space=pl.ANY),
            scratch_shapes=[
                pltpu.VMEM((MOE_TILE // SUBLANES, SUBLANES, w), rows.dtype),
                pltpu.SemaphoreType.DMA(()),
                pltpu.SemaphoreType.DMA(()),
            ],
        ),
        compiler_params=_params("arbitrary"),
        name="scatter_rows",
    )(tinfo, pos.reshape(t // tm, 1, tm), rows.reshape(t // SUBLANES, SUBLANES, w))


def _ffn_kernel(info_ref, x_ref, wgl_ref, wul_ref, wdl_ref, wgh_ref, wuh_ref, wdh_ref, o_ref,
                wg_s, wu_s, wd_s):
    i = pl.program_id(0)
    tm, d = o_ref.shape
    half = tm // 2
    live = i < info_ref[3, 0]
    prev = jnp.maximum(i - 1, 0)
    for k, (wg, wu, wd) in enumerate(((wgl_ref, wul_ref, wdl_ref), (wgh_ref, wuh_ref, wdh_ref))):
        changed = jnp.logical_or(i == 0, info_ref[k, i] != info_ref[k, prev])

        @pl.when(jnp.logical_and(live, changed))
        def _():
            wg_s[k] = wg[...].astype(BF16)
            wu_s[k] = wu[...].astype(BF16)
            wd_s[k] = wd[...].astype(BF16)

    def ffn(rows):
        xt = x_ref[rows, :]
        xb = xt[:, :d].astype(BF16)
        y = None
        for k in range(2):
            gte = _dot(xb, wg_s[k])
            up = _dot(xb, wu_s[k])
            he = (gte * _sigmoid(gte)) * up
            part = xt[:, d + k:d + k + 1] * _dot(he.astype(BF16), wd_s[k])
            y = part if y is None else y + part
        o_ref[rows, :] = y

    @pl.when(jnp.logical_not(live))
    def _():
        o_ref[...] = jnp.zeros_like(o_ref)

    @pl.when(jnp.logical_and(live, info_ref[2, i] > half))
    def _():
        ffn(slice(0, tm))

    @pl.when(jnp.logical_and(live, info_ref[2, i] <= half))
    def _():
        ffn(slice(0, half))
        o_ref[half:, :] = jnp.zeros((tm - half, d), F32)


def _expert_ffn(tinfo, xs, wg, wu, wd, layer):
    npad, w = xs.shape
    _, _, d, de = wg.shape
    tm = MOE_TILE

    def step(i, info):
        return jnp.minimum(i, info[3, 0] - 1)

    w_in = lambda row: pl.BlockSpec((None, None, d, de),
                                    lambda i, info: (layer, info[row, step(i, info)], 0, 0))
    w_out = lambda row: pl.BlockSpec((None, None, de, d),
                                     lambda i, info: (layer, info[row, step(i, info)], 0, 0))
    return pl.pallas_call(
        _ffn_kernel,
        out_shape=jax.ShapeDtypeStruct((npad, d), F32),
        grid_spec=pltpu.PrefetchScalarGridSpec(
            num_scalar_prefetch=1,
            grid=(npad // tm,),
            in_specs=[
                pl.BlockSpec((tm, w), lambda i, info: (info[4, step(i, info)], 0)),
                w_in(0), w_in(0), w_out(0), w_in(1), w_in(1), w_out(1),
            ],
            out_specs=pl.BlockSpec((tm, d), lambda i, info: (info[4, i], 0)),
            scratch_shapes=[
                pltpu.VMEM((2, d, de), BF16),
                pltpu.VMEM((2, d, de), BF16),
                pltpu.VMEM((2, de, d), BF16),
            ],
        ),
        compiler_params=_params("arbitrary"),
        name="expert_ffn",
    )(tinfo, xs, wg, wu, wd, wg, wu, wd)


def _ple_kernel(pos_ref, posn_ref, x_ref, p_ref, ys_ref, g_ref, wg_ref, wp_ref, gn_ref, *rest,
                tm, nsteps, col_map):
    if col_map is None:
        o_ref, ybuf, sem = rest
    else:
        win_ref, o_ref, proj_ref, ybuf, sem = rest
    i = pl.program_id(0)
    slot = lax.rem(i, 2)
    d = x_ref.shape[1]

    def row_copy(grp, j, p, s):
        src = ys_ref.at[lax.shift_right_logical(p, 3), pl.ds(jnp.bitwise_and(p, SUBLANES - 1), 1)]
        return pltpu.make_async_copy(src, ybuf.at[s, grp, pl.ds(j, 1)], sem.at[s])

    def wait_slot(s):
        def wait(r, c):
            row_copy(0, 0, 0, s).wait()
            return c
        lax.fori_loop(0, tm, wait, 0, unroll=DMA_UNROLL)

    @pl.when(i == 0)
    def _():
        def start(grp, c):
            for j in range(SUBLANES):
                row_copy(grp, j, pos_ref[0, 0, grp * SUBLANES + j], 0).start()
            return c
        lax.fori_loop(0, tm // SUBLANES, start, 0)

    wait_slot(slot)

    n_stage = 1 if col_map is None else (len(col_map) + 1) // 2
    per_stage = -(-(tm // SUBLANES) // n_stage)

    def prefetch(stage):
        for grp in range(stage * per_stage, min((stage + 1) * per_stage, tm // SUBLANES)):
            for j in range(SUBLANES):
                row_copy(grp, j, posn_ref[0, 0, grp * SUBLANES + j], 1 - slot).start(priority=j % 2)

    x2 = x_ref[...] + ybuf[slot].reshape(tm, d)
    if col_map is None:
        prefetch(0)
    gate = _sigmoid(_dot(_rms(x2, g_ref[...]).astype(BF16), wg_ref[...]))
    x3 = x2 + gate * _dot(p_ref[...].astype(BF16), wp_ref[...])
    if col_map is None:
        o_ref[...] = _rms(x3, gn_ref[...])
    else:
        o_ref[...] = x3
        h = _rms(x3, gn_ref[...]).astype(BF16)
        for stage, (src, dst) in enumerate(col_map):
            prefetch(stage)
            proj_ref[:, dst:dst + COL_CHUNK] = _dot(h, win_ref[:, src:src + COL_CHUNK]).astype(BF16)

    @pl.when(i == nsteps - 1)
    def _():
        wait_slot(1 - slot)


def _ple(pos, x, p, ys, g, wg, wp, layer, *, g_next, next_layer, w_in=None, col_map=None):
    t, d = x.shape
    dp = p.shape[2]
    tm = PLE_TILE
    nsteps = t // tm
    pos3 = pos.reshape(nsteps, 1, tm)
    in_specs = [
        pl.BlockSpec((1, 1, tm), lambda i: (i, 0, 0), memory_space=pltpu.SMEM),
        pl.BlockSpec((1, 1, tm), lambda i: (jnp.minimum(i + 1, nsteps - 1), 0, 0),
                     memory_space=pltpu.SMEM),
        pl.BlockSpec((tm, d), lambda i: (i, 0)),
        pl.BlockSpec((None, tm, dp), lambda i: (layer, i, 0)),
        pl.BlockSpec(memory_space=pl.ANY),
        _layer_spec(g, layer), _layer_spec(wg, layer), _layer_spec(wp, layer),
        _layer_spec(g_next, next_layer),
    ]
    args = [pos3, pos3, x, p, ys.reshape(-1, SUBLANES, d), g, wg, wp, g_next]
    out_shape = jax.ShapeDtypeStruct((t, d), F32)
    out_specs = pl.BlockSpec((tm, d), lambda i: (i, 0))
    if col_map is not None:
        n = w_in.shape[2]
        in_specs.append(_layer_spec(w_in, next_layer))
        args.append(w_in)
        out_shape = (out_shape, jax.ShapeDtypeStruct((t, n), BF16))
        out_specs = (out_specs, pl.BlockSpec((tm, n), lambda i: (i, 0)))
    return pl.pallas_call(
        functools.partial(_ple_kernel, tm=tm, nsteps=nsteps, col_map=col_map),
        out_shape=out_shape,
        grid=(nsteps,),
        in_specs=in_specs,
        out_specs=out_specs,
        scratch_shapes=[pltpu.VMEM((2, tm // SUBLANES, SUBLANES, d), F32),
                        pltpu.SemaphoreType.DMA((2,))],
        compiler_params=_params("arbitrary"),
        name="gather_ple",
    )(*args)


def _block_diag(w):
    l, g, n, _ = w.shape
    eye = jnp.eye(g, dtype=w.dtype)
    return (eye[None, :, None, :, None] * w[:, :, :, None, :]).reshape(l, g * n, g * n)


def kernel(x, p, norm_mix, w_in, conv_w, conv_b, w_rg_a, b_rg_a, w_rg_x, b_rg_x, rg_lambda,
           w_attn_o, w_rnn_o, w_out, norm_moe, w_router_group, b_router_group, w_router_expert,
           b_router_expert, w_exp_gate, w_exp_up, w_exp_down, norm_ple, w_ple_gate, w_ple_proj,
           norm_final):
    bsz, seq, d = x.shape
    depth = w_in.shape[0]
    t = bsz * seq
    d_attn = N_HEADS * HEAD_DIM
    d_rnn = conv_w.shape[2]
    assert d_attn == d_rnn and d == 2 * d_attn and d_attn == COL_CHUNK
    assert t % ROW_TILE == 0 and seq % (2 * ATT_TILE) == 0 and seq % RNN_TS == 0
    assert t % RANK_TILE == 0
    assert t % SCATTER_TILE == 0 and t % PLE_TILE == 0

    n_in = w_in.shape[2]
    gates_at = 3 * d_attn + 2 * d_rnn
    col_map = tuple((src, (src - gates_at) % n_in) for src in range(0, n_in, COL_CHUNK))
    ga_blk, gr_blk = 0, 1
    q_col = 2 * d
    q_blk, k_blk, v_blk = (q_col // LANES, (q_col + d_attn) // LANES, (q_col + 2 * d_attn) // LANES)
    xr_blk, xg_blk = (q_col + 3 * d_attn) // d_rnn, (q_col + 3 * d_attn + d_rnn) // d_rnn

    tri_incl = jnp.tril(jnp.ones((ATT_TILE, ATT_TILE), F32)).astype(BF16)
    strict_lt = jnp.triu(jnp.ones((RANK_TILE, RANK_TILE), F32), k=1).astype(BF16)

    n_tiles = (t + N_CLASSES * (MOE_TILE - 1)) // MOE_TILE
    assert n_tiles <= LANES
    n_sorted = n_tiles * MOE_TILE

    vec = lambda v: v.reshape(depth, 1, -1)
    w_in_b = w_in.astype(BF16)
    wa_bd = _block_diag(w_rg_a).astype(BF16)
    wx_bd = _block_diag(w_rg_x).astype(BF16)
    w_r = jnp.concatenate([w_router_group, w_router_expert], axis=2)
    w_r = jnp.pad(w_r, ((0, 0), (0, 0), (0, CLASS_ROWS - w_r.shape[2])))
    w_r_hi = w_r.astype(BF16)
    w_r_lo = (w_r - w_r_hi.astype(F32)).astype(BF16)
    w_r_cat = jnp.pad(jnp.concatenate([w_r_hi, w_r_lo], axis=2),
                      ((0, 0), (0, 0), (0, LANES - 2 * CLASS_ROWS)))
    b_r = jnp.concatenate([b_router_group, b_router_expert], axis=1)
    b_r = jnp.pad(b_r, ((0, 0), (0, CLASS_ROWS - b_r.shape[1]))).reshape(depth, CLASS_ROWS, 1)
    wao_b, wro_b, wo_b = w_attn_o.astype(BF16), w_rnn_o.astype(BF16), w_out.astype(BF16)
    wpg_b, wpp_b = w_ple_gate.astype(BF16), w_ple_proj.astype(BF16)
    g_mix, g_moe, g_ple = vec(norm_mix), vec(norm_moe), vec(norm_ple)
    g_fin = norm_final.reshape(1, 1, -1)
    p3 = p.reshape(depth, t, -1)

    x2d = x.reshape(t, d)
    proj = _inproj(x2d, g_mix, w_in_b, 0, col_map)
    for i in range(depth):
        proj3 = proj.reshape(bsz, seq, -1)
        attn = _attention(proj3, tri_incl, q_blk=q_blk, k_blk=k_blk, v_blk=v_blk)
        rnn = _rglru(proj3, conv_w, vec(conv_b), wa_bd, vec(b_rg_a), wx_bd, vec(b_rg_x),
                     vec(rg_lambda), i, xr_blk=xr_blk, xg_blk=xg_blk)
        x1, routed, cls = _merge(
            x2d, attn.reshape(t, d_attn), rnn.reshape(t, d_rnn), proj, wao_b, wro_b, wo_b,
            g_moe, w_r_cat, b_r, i, ga_blk=ga_blk, gr_blk=gr_blk)
        pos, tinfo = _rank(cls, strict_lt)
        pos = pos.reshape(t)
        xs = _scatter_rows(tinfo, pos, routed, n_sorted)
        ys = _expert_ffn(tinfo, xs.reshape(n_sorted, -1), w_exp_gate, w_exp_up, w_exp_down, i)
        if i + 1 < depth:
            x2d, proj = _ple(pos, x1, p3, ys, g_ple, wpg_b, wpp_b, i, g_next=g_mix,
                             next_layer=i + 1, w_in=w_in_b, col_map=col_map)
        else:
            x2d = _ple(pos, x1, p3, ys, g_ple, wpg_b, wpp_b, i, g_next=g_fin, next_layer=0)
    return x2d.reshape(bsz, seq, d)
```

```python
import functools

import jax
import jax.numpy as jnp
from jax import lax
from jax.experimental import pallas as pl
from jax.experimental.pallas import tpu as pltpu

F32 = jnp.float32
BF16 = jnp.bfloat16
EPS = 1e-6

N_HEADS = 8
HEAD_DIM = 64
N_GROUPS = 4
EXPERTS_PER_GROUP = 4
N_EXPERTS = N_GROUPS * EXPERTS_PER_GROUP
PAIRS_PER_GROUP = 6
N_CLASSES = N_GROUPS * PAIRS_PER_GROUP
CLASS_ROWS = 32
CONV_WIDTH = 4
RG_C = 8.0

LANES = 128
SUBLANES = 8
VMEM_LIMIT = 56 * 1024 * 1024

ROW_TILE = 512
RANK_TILE = 1024
ATT_TILE = 256
RNN_TS = 512
MOE_TILE = 512
SCATTER_TILE = 1024
PLE_TILE = 512
ROUTE_LANES = 128
DMA_UNROLL = 8
COL_CHUNK = 512

_NT = (((1,), (1,)), ((), ()))
LOG2E = 1.4426950408889634
UNDERFLOW_EXPONENT = 151.0 / LOG2E


def _params(*sem):
    return pltpu.CompilerParams(dimension_semantics=sem, vmem_limit_bytes=VMEM_LIMIT)


def _layer_spec(arr, layer):
    nd = arr.ndim
    return pl.BlockSpec((None,) + arr.shape[1:], lambda *_: (layer,) + (0,) * (nd - 1))


def _rms(x, g):
    ms = jnp.mean(x * x, axis=-1, keepdims=True)
    return x * lax.rsqrt(ms + EPS) * g


def _sigmoid(x):
    return 1.0 / (1.0 + jnp.exp(-x))


def _dot(a, b):
    return jnp.dot(a, b, preferred_element_type=F32)


def _in_projection(x, g, w_ref, o_ref, col_map):
    h = _rms(x, g).astype(BF16)
    for src, dst in col_map:
        o_ref[:, dst:dst + COL_CHUNK] = _dot(h, w_ref[:, src:src + COL_CHUNK]).astype(BF16)


def _inproj_kernel(x_ref, g_ref, w_ref, o_ref, *, col_map):
    _in_projection(x_ref[...], g_ref[...], w_ref, o_ref, col_map)


def _inproj(x, g, w, layer, col_map):
    t, d = x.shape
    n = w.shape[2]
    return pl.pallas_call(
        functools.partial(_inproj_kernel, col_map=col_map),
        out_shape=jax.ShapeDtypeStruct((t, n), BF16),
        grid=(t // ROW_TILE,),
        in_specs=[
            pl.BlockSpec((ROW_TILE, d), lambda i: (i, 0)),
            _layer_spec(g, layer),
            _layer_spec(w, layer),
        ],
        out_specs=pl.BlockSpec((ROW_TILE, n), lambda i: (i, 0)),
        compiler_params=_params("arbitrary"),
        name="inproj",
    )(x, g, w)


def _attn_tile(q_heads, kb, vb, u, acc_ref, carry_ref, rows, causal):
    tk = kb.shape[0]
    lane = lax.broadcasted_iota(jnp.int32, (1, LANES), 1)
    first_head = lane < HEAD_DIM
    vz = jnp.zeros_like(vb)
    v_heads = (jnp.where(first_head, vb, vz), jnp.where(first_head, vz, vb))
    pv = None
    for h in range(2):
        z = lax.dot_general(q_heads[h], kb, _NT, preferred_element_type=F32)
        sp = jnp.maximum(z, 0.0) + jnp.log(1.0 + jnp.exp2(jnp.abs(z) * (-LOG2E)))
        if causal is not None:
            sp = jnp.where(causal, sp, 0.0)
        suffix = _dot(sp.astype(BF16), u)
        carry = carry_ref[h, rows, :]
        w = jnp.exp2((z - suffix - jnp.tile(carry, (1, tk // LANES))) * LOG2E)
        if causal is not None:
            w = jnp.where(causal, w, 0.0)
        d = _dot(w.astype(BF16), v_heads[h])
        pv = d if pv is None else pv + d
        carry_ref[h, rows, :] = carry + suffix[:, 0:1]
    acc_ref[rows, :] += pv


def _attn_kernel(q_ref, k_ref, v_ref, u_ref, o_ref, acc_ref, carry_ref, *, tk):
    s = q_ref.shape[0]
    nb = s // tk
    lane = lax.broadcasted_iota(jnp.int32, (1, LANES), 1)
    first_head = lane < HEAD_DIM
    u = u_ref[...]
    causal = (lax.broadcasted_iota(jnp.int32, (2 * tk, tk), 1)
              < lax.broadcasted_iota(jnp.int32, (2 * tk, tk), 0))
    acc_ref[...] = jnp.zeros_like(acc_ref)
    carry_ref[...] = jnp.zeros_like(carry_ref)

    def tile(r0, nrows, kb, mask):
        rows = pl.ds(pl.multiple_of(r0, tk), nrows)
        ks = pl.multiple_of(kb * tk, tk)
        q = q_ref[rows, :] * jnp.asarray(HEAD_DIM ** -0.5, BF16)
        qz = jnp.zeros_like(q)
        q_heads = (jnp.where(first_head, q, qz), jnp.where(first_head, qz, q))
        _attn_tile(q_heads, k_ref[pl.ds(ks, tk), :], v_ref[pl.ds(ks, tk), :],
                   u, acc_ref, carry_ref, rows, mask)

    tile((nb - 1) * tk, tk, nb - 1, causal[:tk])

    def key_block(i, low):
        kb = nb - 2 - i
        tile(kb * tk, 2 * tk, kb, causal)
        done = carry_ref[:, pl.ds(pl.multiple_of((kb + 1) * tk, tk), tk), :]
        return jnp.minimum(low, jnp.min(done.reshape(-1, SUBLANES, LANES), axis=0))

    low = lax.fori_loop(0, nb - 2, key_block, jnp.full((SUBLANES, LANES), jnp.inf, F32))
    tile(0, 2 * tk, 0, causal)

    def query_block(j, c):
        block_rows = pl.ds(pl.multiple_of(j * tk, tk), tk)

        def more(state):
            kb, min_carry = state
            return jnp.logical_and(kb >= 0, min_carry < UNDERFLOW_EXPONENT)

        def body(state):
            kb, _ = state
            tile(j * tk, tk, kb, None)
            return kb - 1, jnp.min(carry_ref[:, block_rows, :])

        lax.while_loop(more, body, (j - 2, jnp.min(carry_ref[:, block_rows, :])))
        return c

    @pl.when(jnp.min(low) < UNDERFLOW_EXPONENT)
    def _():
        lax.fori_loop(2, nb, query_block, 0)

    o_ref[...] = acc_ref[...].astype(BF16)


def _attention(proj, u, *, q_blk, k_blk, v_blk):
    b, s, _ = proj.shape
    hp = N_HEADS * HEAD_DIM // LANES
    cols = lambda blk: pl.BlockSpec((None, s, LANES), lambda bi, hi: (bi, 0, blk + hi))
    return pl.pallas_call(
        functools.partial(_attn_kernel, tk=ATT_TILE),
        out_shape=jax.ShapeDtypeStruct((b, s, N_HEADS * HEAD_DIM), BF16),
        grid=(b, hp),
        in_specs=[cols(q_blk), cols(k_blk), cols(v_blk),
                  pl.BlockSpec((ATT_TILE, ATT_TILE), lambda bi, hi: (0, 0))],
        out_specs=pl.BlockSpec((None, s, LANES), lambda bi, hi: (bi, 0, hi)),
        scratch_shapes=[
            pltpu.VMEM((s, LANES), F32),
            pltpu.VMEM((2, s, LANES), F32),
        ],
        compiler_params=_params("arbitrary", "arbitrary"),
        name="sb_attention",
    )(proj, proj, proj, u)


def _rglru_kernel(xr_ref, xg_ref, cw_ref, cb_ref, wa_ref, ba_ref, wx_ref, bx_ref, lam_ref,
                  o_ref, tail_ref, h_ref, a_scr, b_scr, *, ts):
    @pl.when(pl.program_id(1) == 0)
    def _():
        tail_ref[...] = jnp.zeros_like(tail_ref)
        h_ref[...] = jnp.zeros_like(h_ref)

    groups = ts // SUBLANES
    dr = xr_ref.shape[1]
    xr = xr_ref[...].astype(F32).reshape(groups, SUBLANES, dr)
    tail = tail_ref[...]
    tail_ref[...] = xr[groups - 1]
    in_group = lax.broadcasted_iota(jnp.int32, xr.shape, 1)
    xc = cb_ref[...].reshape(1, 1, dr)
    for j in range(CONV_WIDTH):
        back = CONV_WIDTH - 1 - j
        if back == 0:
            shifted = xr
        else:
            rot = pltpu.roll(xr, back, 1)
            rot_before = jnp.concatenate(
                [pltpu.roll(tail, back, 0).reshape(1, SUBLANES, dr), rot[:groups - 1]], axis=0)
            shifted = jnp.where(in_group >= back, rot, rot_before)
        xc = xc + cw_ref[j:j + 1, :].reshape(1, 1, dr) * shifted
    xc = xc.reshape(ts, dr)
    xcb = xc.astype(BF16)
    r = 0.5 + 0.5 * jnp.tanh(0.5 * (_dot(xcb, wa_ref[...]) + ba_ref[...]))
    ig = 0.5 + 0.5 * jnp.tanh(0.5 * (_dot(xcb, wx_ref[...]) + bx_ref[...]))
    lam = lam_ref[...]
    softplus_neg_lam = jnp.maximum(-lam, 0.0) + jnp.log(1.0 + jnp.exp(-jnp.abs(lam)))
    log_a = (-RG_C * softplus_neg_lam) * r
    a = jnp.exp(log_a)
    v = 1.0 - a * a
    root = jnp.where(v > 0.0, v * lax.rsqrt(v), 0.0)
    b = root * (ig * xc)

    a = a.reshape(groups, SUBLANES, dr)
    b = b.reshape(groups, SUBLANES, dr)
    for s in (1, 2, 4):
        has_prev = in_group >= s
        b = jnp.where(has_prev, a * pltpu.roll(b, s, 1) + b, b)
        a = jnp.where(has_prev, a * pltpu.roll(a, s, 1), a)
    a_scr[...] = a.reshape(ts, dr)
    b_scr[...] = b.reshape(ts, dr)

    def group(gidx, h):
        r0 = pl.multiple_of(gidx * SUBLANES, SUBLANES)
        hg = a_scr[pl.ds(r0, SUBLANES), :] * h + b_scr[pl.ds(r0, SUBLANES), :]
        b_scr[pl.ds(r0, SUBLANES), :] = hg
        return hg[SUBLANES - 1:, :]

    h_ref[...] = lax.fori_loop(0, ts // SUBLANES, group, h_ref[...], unroll=4)
    xg = xg_ref[...].astype(F32)
    gelu = 0.5 * xg * (1.0 + jnp.tanh(0.7978845608028654 * (xg + 0.044715 * (xg * xg * xg))))
    o_ref[...] = (b_scr[...] * gelu).astype(BF16)


def _rglru(proj, cw, cb, wa, ba, wx, bx, lam, layer, *, xr_blk, xg_blk):
    b, s, _ = proj.shape
    dr = cw.shape[2]
    return pl.pallas_call(
        functools.partial(_rglru_kernel, ts=RNN_TS),
        out_shape=jax.ShapeDtypeStruct((b, s, dr), BF16),
        grid=(b, s // RNN_TS),
        in_specs=[
            pl.BlockSpec((None, RNN_TS, dr), lambda bi, si: (bi, si, xr_blk)),
            pl.BlockSpec((None, RNN_TS, dr), lambda bi, si: (bi, si, xg_blk)),
            _layer_spec(cw, layer), _layer_spec(cb, layer),
            _layer_spec(wa, layer), _layer_spec(ba, layer),
            _layer_spec(wx, layer), _layer_spec(bx, layer),
            _layer_spec(lam, layer),
        ],
        out_specs=pl.BlockSpec((None, RNN_TS, dr), lambda bi, si: (bi, si, 0)),
        scratch_shapes=[
            pltpu.VMEM((SUBLANES, dr), F32),
            pltpu.VMEM((1, dr), F32),
            pltpu.VMEM((RNN_TS, dr), F32),
            pltpu.VMEM((RNN_TS, dr), F32),
        ],
        compiler_params=_params("arbitrary", "arbitrary"),
        name="rglru",
    )(proj, proj, cw, cb, wa, ba, wx, bx, lam)


def _first_argmax(vals):
    m = vals[0]
    for v in vals[1:]:
        m = jnp.maximum(m, v)
    idx = jnp.full_like(m, float(len(vals) - 1))
    for k in range(len(vals) - 2, -1, -1):
        idx = jnp.where(vals[k] == m, float(k), idx)
    return m, idx


def _merge_kernel(x_ref, at_ref, rn_ref, ga_ref, gr_ref, wao_ref, wro_ref, wo_ref, g_ref,
                  wr_ref, br_ref, x1_ref, hr_ref, cls_ref):
    d = x_ref.shape[1]
    ya = _dot(at_ref[...], wao_ref[...])
    yr = _dot(rn_ref[...], wro_ref[...])
    merged = _sigmoid(ga_ref[...].astype(F32)) * ya + _sigmoid(gr_ref[...].astype(F32)) * yr
    x1 = x_ref[...] + _dot(merged.astype(BF16), wo_ref[...])
    x1_ref[...] = x1
    h = _rms(x1, g_ref[...])
    hr_ref[:, :d] = h

    h_hi = h.astype(BF16)
    h_lo = (h - h_hi.astype(F32)).astype(BF16)
    wr = wr_ref[...]
    la = _dot(h_hi, wr)
    lb = _dot(h_lo, wr)
    logits = la + pltpu.roll(la, LANES - CLASS_ROWS, 1) + lb
    lt = logits.T[:CLASS_ROWS, :] + br_ref[...]
    g = [lt[k:k + 1, :] for k in range(N_GROUPS)]
    e = [lt[N_GROUPS + k:N_GROUPS + k + 1, :] for k in range(N_EXPERTS)]

    gmax, gi = _first_argmax(g)
    den = jnp.exp(g[0] - gmax)
    for k in range(1, N_GROUPS):
        den = den + jnp.exp(g[k] - gmax)
    gate = 1.0 / den
    sel = []
    for j in range(EXPERTS_PER_GROUP):
        s = e[(N_GROUPS - 1) * EXPERTS_PER_GROUP + j]
        for k in range(N_GROUPS - 2, -1, -1):
            s = jnp.where(gi == float(k), e[k * EXPERTS_PER_GROUP + j], s)
        sel.append(s)
    m1, i1 = _first_argmax(sel)
    sel2 = [jnp.where(i1 == float(j), -jnp.inf, sel[j]) for j in range(EXPERTS_PER_GROUP)]
    m2, i2 = _first_argmax(sel2)
    t = jnp.exp(m2 - m1)
    w1 = gate * (1.0 / (1.0 + t))
    w2 = gate * (t / (1.0 + t))
    first_is_lo = i1 < i2
    a = jnp.minimum(i1, i2)
    b = jnp.maximum(i1, i2)
    w_lo = jnp.where(first_is_lo, w1, w2)
    w_hi = jnp.where(first_is_lo, w2, w1)
    pair = jnp.where(a == 0.0, b - 1.0, jnp.where(a == 1.0, b + 1.0, 5.0))
    cls = gi * float(PAIRS_PER_GROUP) + pair

    n = cls.shape[1]
    cls_ref[...] = jnp.concatenate([cls, jnp.zeros((SUBLANES - 1, n), F32)], axis=0)
    rows = jnp.concatenate([w_lo, w_hi, jnp.zeros((ROUTE_LANES - 2, n), F32)], axis=0)
    hr_ref[:, d:] = rows.T


def _merge(x, attn, rnn, proj, wao, wro, wo, g, wr, br, layer, *, ga_blk, gr_blk):
    t, d = x.shape
    da = attn.shape[1]
    dr = rnn.shape[1]
    tm = ROW_TILE
    return pl.pallas_call(
        _merge_kernel,
        out_shape=(
            jax.ShapeDtypeStruct((t, d), F32),
            jax.ShapeDtypeStruct((t, d + ROUTE_LANES), F32),
            jax.ShapeDtypeStruct((SUBLANES, t), F32),
        ),
        grid=(t // tm,),
        in_specs=[
            pl.BlockSpec((tm, d), lambda i: (i, 0)),
            pl.BlockSpec((tm, da), lambda i: (i, 0)),
            pl.BlockSpec((tm, dr), lambda i: (i, 0)),
            pl.BlockSpec((tm, d), lambda i: (i, ga_blk)),
            pl.BlockSpec((tm, d), lambda i: (i, gr_blk)),
            _layer_spec(wao, layer), _layer_spec(wro, layer), _layer_spec(wo, layer),
            _layer_spec(g, layer), _layer_spec(wr, layer), _layer_spec(br, layer),
        ],
        out_specs=(
            pl.BlockSpec((tm, d), lambda i: (i, 0)),
            pl.BlockSpec((tm, d + ROUTE_LANES), lambda i: (i, 0)),
            pl.BlockSpec((SUBLANES, tm), lambda i: (0, i)),
        ),
        compiler_params=_params("arbitrary"),
        name="merge_router",
    )(x, attn, rnn, proj, proj, wao, wro, wo, g, wr, br)


def _rank_kernel(cls_ref, lt_ref, pos_ref, tinfo_ref, cnt_ref, off_ref, run_ref, *, tm):
    ph = pl.program_id(0)
    bi = pl.program_id(1)
    cls = cls_ref[0:1, :]
    crow = lax.broadcasted_iota(jnp.int32, (CLASS_ROWS, tm), 0).astype(F32)
    member = crow == cls
    onehot = jnp.where(member, 1.0, 0.0)
    block_count = jnp.sum(onehot, axis=1, keepdims=True)

    @pl.when(jnp.logical_and(ph == 0, bi == 0))
    def _():
        cnt_ref[...] = jnp.zeros_like(cnt_ref)

    @pl.when(ph == 0)
    def _():
        cnt_ref[...] += block_count

    @pl.when(jnp.logical_and(ph == 1, bi == 0))
    def _():
        ntile = jnp.floor((cnt_ref[...] + float(MOE_TILE - 1)) * (1.0 / MOE_TILE))
        rid = lax.broadcasted_iota(jnp.int32, (CLASS_ROWS, LANES), 0)
        toff = jnp.zeros((CLASS_ROWS, LANES), F32)
        for c in range(1, CLASS_ROWS):
            toff = toff + jnp.where(rid >= c, ntile[c - 1:c, :], 0.0)
        off_ref[...] = toff * float(MOE_TILE)
        run_ref[...] = jnp.zeros_like(run_ref)
        ti = lax.broadcasted_iota(jnp.int32, (CLASS_ROWS, LANES), 1).astype(F32)
        inside = jnp.where(ti >= toff, jnp.where(ti < toff + ntile, 1.0, 0.0), 0.0)
        c = rid.astype(F32)
        grp = (jnp.where(c >= 6.0, 1.0, 0.0) + jnp.where(c >= 12.0, 1.0, 0.0)
               + jnp.where(c >= 18.0, 1.0, 0.0))
        pair = c - float(PAIRS_PER_GROUP) * grp
        a = jnp.where(pair >= 3.0, 1.0, 0.0) + jnp.where(pair >= 5.0, 1.0, 0.0)
        b = jnp.where(pair < 3.0, pair + 1.0, jnp.where(pair < 5.0, pair - 1.0, 3.0))
        e_lo = jnp.sum(inside * (float(EXPERTS_PER_GROUP) * grp + a), axis=0, keepdims=True)
        e_hi = jnp.sum(inside * (float(EXPERTS_PER_GROUP) * grp + b), axis=0, keepdims=True)
        first = (ti - toff) == 0.0
        tile_c = toff + jnp.where(first, ntile - 1.0, ti - toff - 1.0)
        rows_c = jnp.where(first, cnt_ref[...] - (ntile - 1.0) * float(MOE_TILE), float(MOE_TILE))
        used = jnp.sum(ntile, axis=0, keepdims=True)
        tile = jnp.sum(inside * tile_c, axis=0, keepdims=True)
        tile = tile + jnp.where(ti[0:1, :] >= used, ti[0:1, :], 0.0)
        rows = jnp.sum(inside * rows_c, axis=0, keepdims=True)
        left = cnt_ref[...] - (ti - toff) * float(MOE_TILE)
        tile_rows = jnp.sum(inside * jnp.minimum(left, float(MOE_TILE)), axis=0, keepdims=True)
        info = jnp.concatenate([e_lo, e_hi, rows, used, tile, tile_rows,
                                jnp.zeros((SUBLANES - 6, LANES), F32)], axis=0)
        tinfo_ref[...] = info.astype(jnp.int32)

    @pl.when(ph == 1)
    def _():
        before = _dot(onehot.astype(BF16), lt_ref[...])
        val = before + run_ref[:, 0:1] + off_ref[:, 0:1]
        pos = jnp.sum(jnp.where(member, val, 0.0), axis=0, keepdims=True)
        pos_ref[...] = pos.astype(jnp.int32)
        run_ref[...] += block_count


def _rank(cls, strict_lt):
    t = cls.shape[1]
    tm = RANK_TILE
    return pl.pallas_call(
        functools.partial(_rank_kernel, tm=tm),
        out_shape=(
            jax.ShapeDtypeStruct((1, t), jnp.int32),
            jax.ShapeDtypeStruct((SUBLANES, LANES), jnp.int32),
        ),
        grid=(2, t // tm),
        in_specs=[
            pl.BlockSpec((SUBLANES, tm), lambda ph, bi: (0, bi)),
            pl.BlockSpec((tm, tm), lambda ph, bi: (0, 0)),
        ],
        out_specs=(
            pl.BlockSpec((1, tm), lambda ph, bi: (0, bi * ph)),
            pl.BlockSpec((SUBLANES, LANES), lambda ph, bi: (0, 0)),
        ),
        scratch_shapes=[pltpu.VMEM((CLASS_ROWS, LANES), F32)] * 3,
        compiler_params=_params("arbitrary", "arbitrary"),
        name="rank_tokens",
    )(cls, strict_lt)


def _scatter_kernel(info_ref, pos_ref, h_ref, o_ref, zero_ref, sem, zsem, *, tm):
    tile_groups = MOE_TILE // SUBLANES

    @pl.when(pl.program_id(0) == 0)
    def _():
        zero_ref[...] = jnp.zeros_like(zero_ref)

        def tile_clear(ti):
            dst = o_ref.at[pl.ds(ti * tile_groups, tile_groups)]
            return pltpu.make_async_copy(zero_ref, dst, zsem)

        n_tiles = o_ref.shape[0] // tile_groups
        for ti in range(n_tiles):
            @pl.when(info_ref[5, ti] < MOE_TILE)
            def _():
                tile_clear(ti).start()
        for ti in range(n_tiles):
            @pl.when(info_ref[5, ti] < MOE_TILE)
            def _():
                tile_clear(ti).wait()

    def row_copy(grp, j, p):
        dst = o_ref.at[lax.shift_right_logical(p, 3), pl.ds(jnp.bitwise_and(p, SUBLANES - 1), 1)]
        return pltpu.make_async_copy(h_ref.at[grp, pl.ds(j, 1)], dst, sem)

    def start(grp, c):
        for j in range(SUBLANES):
            row_copy(grp, j, pos_ref[0, 0, grp * SUBLANES + j]).start(priority=j % 2)
        return c

    def wait(r, c):
        row_copy(0, 0, 0).wait()
        return c

    lax.fori_loop(0, tm // SUBLANES, start, 0)
    lax.fori_loop(0, tm, wait, 0, unroll=DMA_UNROLL)


def _scatter_rows(tinfo, pos, rows, n_sorted):
    t, w = rows.shape
    tm = SCATTER_TILE
    return pl.pallas_call(
        functools.partial(_scatter_kernel, tm=tm),
        out_shape=jax.ShapeDtypeStruct((n_sorted // SUBLANES, SUBLANES, w), rows.dtype),
        grid_spec=pltpu.PrefetchScalarGridSpec(
            num_scalar_prefetch=1,
            grid=(t // tm,),
            in_specs=[
                pl.BlockSpec((1, 1, tm), lambda i, info: (i, 0, 0), memory_space=pltpu.SMEM),
                pl.BlockSpec((tm // SUBLANES, SUBLANES, w), lambda i, info: (i, 0, 0)),
            ],
            out_specs=pl.BlockSpec(memory_space=pl.ANY),
            scratch_shapes=[
                pltpu.VMEM((MOE_TILE // SUBLANES, SUBLANES, w), rows.dtype),
                pltpu.SemaphoreType.DMA(()),
                pltpu.SemaphoreType.DMA(()),
            ],
        ),
        compiler_params=_params("arbitrary"),
        name="scatter_rows",
    )(tinfo, pos.reshape(t // tm, 1, tm), rows.reshape(t // SUBLANES, SUBLANES, w))


def _ffn_kernel(info_ref, x_ref, wgl_ref, wul_ref, wdl_ref, wgh_ref, wuh_ref, wdh_ref, o_ref,
                wg_s, wu_s, wd_s):
    i = pl.program_id(0)
    tm, d = o_ref.shape
    half = tm // 2
    live = i < info_ref[3, 0]
    prev = jnp.maximum(i - 1, 0)
    for k, (wg, wu, wd) in enumerate(((wgl_ref, wul_ref, wdl_ref), (wgh_ref, wuh_ref, wdh_ref))):
        changed = jnp.logical_or(i == 0, info_ref[k, i] != info_ref[k, prev])

        @pl.when(jnp.logical_and(live, changed))
        def _():
            wg_s[k] = wg[...].astype(BF16)
            wu_s[k] = wu[...].astype(BF16)
            wd_s[k] = wd[...].astype(BF16)

    def ffn(rows):
        xt = x_ref[rows, :]
        xb = xt[:, :d].astype(BF16)
        y = None
        de = wg_s.shape[2]
        for k in range(2):
            yk = None
            for c0 in range(0, de, de // 2):
                cols = slice(c0, c0 + de // 2)
                gte = _dot(xb, wg_s[k, :, cols])
                up = _dot(xb, wu_s[k, :, cols])
                he = (gte * _sigmoid(gte)) * up
                dk = _dot(he.astype(BF16), wd_s[k, cols, :])
                yk = dk if yk is None else yk + dk
            part = xt[:, d + k:d + k + 1] * yk
            y = part if y is None else y + part
        o_ref[rows, :] = y

    @pl.when(jnp.logical_not(live))
    def _():
        o_ref[...] = jnp.zeros_like(o_ref)

    @pl.when(jnp.logical_and(live, info_ref[2, i] > half))
    def _():
        ffn(slice(0, tm))

    @pl.when(jnp.logical_and(live, info_ref[2, i] <= half))
    def _():
        ffn(slice(0, half))
        o_ref[half:, :] = jnp.zeros((tm - half, d), F32)


def _expert_ffn(tinfo, xs, wg, wu, wd, layer):
    npad, w = xs.shape
    _, _, d, de = wg.shape
    tm = MOE_TILE

    def step(i, info):
        return jnp.minimum(i, info[3, 0] - 1)

    w_in = lambda row: pl.BlockSpec((None, None, d, de),
                                    lambda i, info: (layer, info[row, step(i, info)], 0, 0))
    w_out = lambda row: pl.BlockSpec((None, None, de, d),
                                     lambda i, info: (layer, info[row, step(i, info)], 0, 0))
    return pl.pallas_call(
        _ffn_kernel,
        out_shape=jax.ShapeDtypeStruct((npad, d), F32),
        grid_spec=pltpu.PrefetchScalarGridSpec(
            num_scalar_prefetch=1,
            grid=(npad // tm,),
            in_specs=[
                pl.BlockSpec((tm, w), lambda i, info: (info[4, step(i, info)], 0)),
                w_in(0), w_in(0), w_out(0), w_in(1), w_in(1), w_out(1),
            ],
            out_specs=pl.BlockSpec((tm, d), lambda i, info: (info[4, i], 0)),
            scratch_shapes=[
                pltpu.VMEM((2, d, de), BF16),
                pltpu.VMEM((2, d, de), BF16),
                pltpu.VMEM((2, de, d), BF16),
            ],
        ),
        compiler_params=_params("arbitrary"),
        name="expert_ffn",
    )(tinfo, xs, wg, wu, wd, wg, wu, wd)


def _ple_kernel(pos_ref, posn_ref, x_ref, p_ref, ys_ref, g_ref, wg_ref, wp_ref, gn_ref, *rest,
                tm, nsteps, col_map):
    if col_map is None:
        o_ref, ybuf, sem = rest
    else:
        win_ref, o_ref, proj_ref, ybuf, sem = rest
    i = pl.program_id(0)
    slot = lax.rem(i, 2)
    d = x_ref.shape[1]

    def row_copy(grp, j, p, s):
        src = ys_ref.at[lax.shift_right_logical(p, 3), pl.ds(jnp.bitwise_and(p, SUBLANES - 1), 1)]
        return pltpu.make_async_copy(src, ybuf.at[s, grp, pl.ds(j, 1)], sem.at[s])

    def wait_slot(s):
        def wait(r, c):
            row_copy(0, 0, 0, s).wait()
            return c
        lax.fori_loop(0, tm, wait, 0, unroll=DMA_UNROLL)

    @pl.when(i == 0)
    def _():
        def start(grp, c):
            for j in range(SUBLANES):
                row_copy(grp, j, pos_ref[0, 0, grp * SUBLANES + j], 0).start()
            return c
        lax.fori_loop(0, tm // SUBLANES, start, 0)

    wait_slot(slot)

    n_stage = 1 if col_map is None else (len(col_map) + 1) // 2
    per_stage = -(-(tm // SUBLANES) // n_stage)

    def prefetch(stage):
        for grp in range(stage * per_stage, min((stage + 1) * per_stage, tm // SUBLANES)):
            for j in range(SUBLANES):
                row_copy(grp, j, posn_ref[0, 0, grp * SUBLANES + j], 1 - slot).start(priority=j % 2)

    x2 = x_ref[...] + ybuf[slot].reshape(tm, d)
    if col_map is None:
        prefetch(0)
    gate = _sigmoid(_dot(_rms(x2, g_ref[...]).astype(BF16), wg_ref[...]))
    x3 = x2 + gate * _dot(p_ref[...].astype(BF16), wp_ref[...])
    if col_map is None:
        o_ref[...] = _rms(x3, gn_ref[...])
    else:
        o_ref[...] = x3
        h = _rms(x3, gn_ref[...]).astype(BF16)
        for stage, (src, dst) in enumerate(col_map):
            prefetch(stage)
            proj_ref[:, dst:dst + COL_CHUNK] = _dot(h, win_ref[:, src:src + COL_CHUNK]).astype(BF16)

    @pl.when(i == nsteps - 1)
    def _():
        wait_slot(1 - slot)


def _ple(pos, x, p, ys, g, wg, wp, layer, *, g_next, next_layer, w_in=None, col_map=None):
    t, d = x.shape
    dp = p.shape[2]
    tm = PLE_TILE
    nsteps = t // tm
    pos3 = pos.reshape(nsteps, 1, tm)
    in_specs = [
        pl.BlockSpec((1, 1, tm), lambda i: (i, 0, 0), memory_space=pltpu.SMEM),
        pl.BlockSpec((1, 1, tm), lambda i: (jnp.minimum(i + 1, nsteps - 1), 0, 0),
                     memory_space=pltpu.SMEM),
        pl.BlockSpec((tm, d), lambda i: (i, 0)),
        pl.BlockSpec((None, tm, dp), lambda i: (layer, i, 0)),
        pl.BlockSpec(memory_space=pl.ANY),
        _layer_spec(g, layer), _layer_spec(wg, layer), _layer_spec(wp, layer),
        _layer_spec(g_next, next_layer),
    ]
    args = [pos3, pos3, x, p, ys.reshape(-1, SUBLANES, d), g, wg, wp, g_next]
    out_shape = jax.ShapeDtypeStruct((t, d), F32)
    out_specs = pl.BlockSpec((tm, d), lambda i: (i, 0))
    if col_map is not None:
        n = w_in.shape[2]
        in_specs.append(_layer_spec(w_in, next_layer))
        args.append(w_in)
        out_shape = (out_shape, jax.ShapeDtypeStruct((t, n), BF16))
        out_specs = (out_specs, pl.BlockSpec((tm, n), lambda i: (i, 0)))
    return pl.pallas_call(
        functools.partial(_ple_kernel, tm=tm, nsteps=nsteps, col_map=col_map),
        out_shape=out_shape,
        grid=(nsteps,),
        in_specs=in_specs,
        out_specs=out_specs,
        scratch_shapes=[pltpu.VMEM((2, tm // SUBLANES, SUBLANES, d), F32),
                        pltpu.SemaphoreType.DMA((2,))],
        compiler_params=_params("arbitrary"),
        name="gather_ple",
    )(*args)


def _block_diag(w):
    l, g, n, _ = w.shape
    eye = jnp.eye(g, dtype=w.dtype)
    return (eye[None, :, None, :, None] * w[:, :, :, None, :]).reshape(l, g * n, g * n)


def kernel(x, p, norm_mix, w_in, conv_w, conv_b, w_rg_a, b_rg_a, w_rg_x, b_rg_x, rg_lambda,
           w_attn_o, w_rnn_o, w_out, norm_moe, w_router_group, b_router_group, w_router_expert,
           b_router_expert, w_exp_gate, w_exp_up, w_exp_down, norm_ple, w_ple_gate, w_ple_proj,
           norm_final):
    bsz, seq, d = x.shape
    depth = w_in.shape[0]
    t = bsz * seq
    d_attn = N_HEADS * HEAD_DIM
    d_rnn = conv_w.shape[2]
    assert d_attn == d_rnn and d == 2 * d_attn and d_attn == COL_CHUNK
    assert t % ROW_TILE == 0 and seq % (2 * ATT_TILE) == 0 and seq % RNN_TS == 0
    assert t % RANK_TILE == 0
    assert t % SCATTER_TILE == 0 and t % PLE_TILE == 0

    n_in = w_in.shape[2]
    gates_at = 3 * d_attn + 2 * d_rnn
    col_map = tuple((src, (src - gates_at) % n_in) for src in range(0, n_in, COL_CHUNK))
    ga_blk, gr_blk = 0, 1
    q_col = 2 * d
    q_blk, k_blk, v_blk = (q_col // LANES, (q_col + d_attn) // LANES, (q_col + 2 * d_attn) // LANES)
    xr_blk, xg_blk = (q_col + 3 * d_attn) // d_rnn, (q_col + 3 * d_attn + d_rnn) // d_rnn

    tri_incl = jnp.tril(jnp.ones((ATT_TILE, ATT_TILE), F32)).astype(BF16)
    strict_lt = jnp.triu(jnp.ones((RANK_TILE, RANK_TILE), F32), k=1).astype(BF16)

    n_tiles = (t + N_CLASSES * (MOE_TILE - 1)) // MOE_TILE
    assert n_tiles <= LANES
    n_sorted = n_tiles * MOE_TILE

    vec = lambda v: v.reshape(depth, 1, -1)
    w_in_b = w_in.astype(BF16)
    wa_bd = _block_diag(w_rg_a).astype(BF16)
    wx_bd = _block_diag(w_rg_x).astype(BF16)
    w_r = jnp.concatenate([w_router_group, w_router_expert], axis=2)
    w_r = jnp.pad(w_r, ((0, 0), (0, 0), (0, CLASS_ROWS - w_r.shape[2])))
    w_r_hi = w_r.astype(BF16)
    w_r_lo = (w_r - w_r_hi.astype(F32)).astype(BF16)
    w_r_cat = jnp.pad(jnp.concatenate([w_r_hi, w_r_lo], axis=2),
                      ((0, 0), (0, 0), (0, LANES - 2 * CLASS_ROWS)))
    b_r = jnp.concatenate([b_router_group, b_router_expert], axis=1)
    b_r = jnp.pad(b_r, ((0, 0), (0, CLASS_ROWS - b_r.shape[1]))).reshape(depth, CLASS_ROWS, 1)
    wao_b, wro_b, wo_b = w_attn_o.astype(BF16), w_rnn_o.astype(BF16), w_out.astype(BF16)
    wpg_b, wpp_b = w_ple_gate.astype(BF16), w_ple_proj.astype(BF16)
    g_mix, g_moe, g_ple = vec(norm_mix), vec(norm_moe), vec(norm_ple)
    g_fin = norm_final.reshape(1, 1, -1)
    p3 = p.reshape(depth, t, -1)

    x2d = x.reshape(t, d)
    proj = _inproj(x2d, g_mix, w_in_b, 0, col_map)
    for i in range(depth):
        proj3 = proj.reshape(bsz, seq, -1)
        attn = _attention(proj3, tri_incl, q_blk=q_blk, k_blk=k_blk, v_blk=v_blk)
        rnn = _rglru(proj3, conv_w, vec(conv_b), wa_bd, vec(b_rg_a), wx_bd, vec(b_rg_x),
                     vec(rg_lambda), i, xr_blk=xr_blk, xg_blk=xg_blk)
        x1, routed, cls = _merge(
            x2d, attn.reshape(t, d_attn), rnn.reshape(t, d_rnn), proj, wao_b, wro_b, wo_b,
            g_moe, w_r_cat, b_r, i, ga_blk=ga_blk, gr_blk=gr_blk)
        pos, tinfo = _rank(cls, strict_lt)
        pos = pos.reshape(t)
        xs = _scatter_rows(tinfo, pos, routed, n_sorted)
        ys = _expert_ffn(tinfo, xs.reshape(n_sorted, -1), w_exp_gate, w_exp_up, w_exp_down, i)
        if i + 1 < depth:
            x2d, proj = _ple(pos, x1, p3, ys, g_ple, wpg_b, wpp_b, i, g_next=g_mix,
                             next_layer=i + 1, w_in=w_in_b, col_map=col_map)
        else:
            x2d = _ple(pos, x1, p3, ys, g_ple, wpg_b, wpp_b, i, g_next=g_fin, next_layer=0)
    return x2d.reshape(bsz, seq, d)
```
